```python
import math
import jax, jax.numpy as jnp
from jax import lax
import numpy as np

D_MODEL = 1024
BATCH = 4
SEQ = 4096
DEPTH = 2
DEC_BATCH = 16
DEC_SEQ = 16
PAST_LEN = 2048

CHUNK = 64
N_MIXERS = 2
CONV_WIDTH = 31
SSM_GROUP = 16
SSM_GROUPS = D_MODEL // SSM_GROUP
SSM_STATE = 64
DT_MIN = 1e-3
DT_MAX = 1e-1
N_MEM = 256
N_MEM_HEADS = 4
MEM_HEAD_DIM = D_MODEL // N_MEM_HEADS
D_FF = 2816
N_EXPERTS = 8
TOP_K = 2
D_FF_EXPERT = 3584
N_CONV_LAYERS = (DEPTH + 1) // 2
N_SSM_LAYERS = DEPTH // 2
N_DENSE_LAYERS = (DEPTH + 1) // 2
N_MOE_LAYERS = DEPTH // 2
N_NORMS = 6
EPS = 1e-6

kernel_name = "hybrid_conv_s5_stream_encoder_step"


def rmsnorm(x, g):
    xf = x.astype(jnp.float32)
    y = xf * lax.rsqrt(jnp.mean(xf * xf, axis=-1, keepdims=True) + EPS)
    return (y * g.astype(jnp.float32)).astype(x.dtype)


def layernorm(x, g, b):
    xf = x.astype(jnp.float32)
    xc = xf - jnp.mean(xf, axis=-1, keepdims=True)
    var = jnp.mean(xc * xc, axis=-1, keepdims=True)
    return (xc * lax.rsqrt(var + EPS) * g.astype(jnp.float32) + b.astype(jnp.float32)).astype(x.dtype)


def conformer_conv(h, conv_state, w_pw1, b_pw1, w_dw, b_dw, ln_g, ln_b, w_pw2, b_pw2):
    z = h @ w_pw1 + b_pw1
    u = z[..., :D_MODEL] * jax.nn.sigmoid(z[..., D_MODEL:])
    u_ext = jnp.concatenate([conv_state.astype(u.dtype), u], axis=1)
    y = lax.conv_general_dilated(
        u_ext, w_dw[:, None, :].astype(u.dtype), window_strides=(1,), padding='VALID',
        dimension_numbers=('NWC', 'WIO', 'NWC'), feature_group_count=D_MODEL) + b_dw
    y = jax.nn.silu(layernorm(y, ln_g, ln_b))
    return y @ w_pw2 + b_pw2, u_ext[:, -(CONV_WIDTH - 1):]


def _ssm_combine(e1, e2):
    a1r, a1i, b1r, b1i = e1
    a2r, a2i, b2r, b2i = e2
    return (a2r * a1r - a2i * a1i,
            a2r * a1i + a2i * a1r,
            a2r * b1r - a2i * b1i + b2r,
            a2r * b1i + a2i * b1r + b2i)


def s5_ssm(h, s_re, s_im, a_re, a_im, log_dt, b_re, b_im, c_re, c_im, d_skip, w_glu, b_glu):
    f32 = jnp.float32
    bsz, seq, _ = h.shape
    u = h.astype(f32)
    ug = u.reshape(bsz, seq, SSM_GROUPS, SSM_GROUP)
    dt = jnp.exp(log_dt.astype(f32))[:, None]
    lr = a_re.astype(f32)
    li = a_im.astype(f32)
    mag = jnp.exp(lr * dt)
    ab_r = mag * jnp.cos(li * dt)
    ab_i = mag * jnp.sin(li * dt)
    den = lr * lr + li * li
    nr = ab_r - 1.0
    k_r = (nr * lr + ab_i * li) / den
    k_i = (ab_i * lr - nr * li) / den
    br = b_re.astype(f32)
    bi = b_im.astype(f32)
    bb_r = k_r[..., None] * br - k_i[..., None] * bi
    bb_i = k_r[..., None] * bi + k_i[..., None] * br
    x_r = jnp.einsum('blgc,gpc->lbgp', ug, bb_r)
    x_i = jnp.einsum('blgc,gpc->lbgp', ug, bb_i)
    s_re = s_re.astype(f32)
    s_im = s_im.astype(f32)
    x_r = x_r.at[0].add(ab_r * s_re - ab_i * s_im)
    x_i = x_i.at[0].add(ab_r * s_im + ab_i * s_re)
    shape_a = (seq, 1, SSM_GROUPS, SSM_STATE)
    _, _, x_r, x_i = lax.associative_scan(
        _ssm_combine,
        (jnp.broadcast_to(ab_r, shape_a), jnp.broadcast_to(ab_i, shape_a), x_r, x_i),
        axis=0)
    y = (jnp.einsum('lbgp,gcp->blgc', x_r, c_re.astype(f32))
         - jnp.einsum('lbgp,gcp->blgc', x_i, c_im.astype(f32)))
    y = y.reshape(bsz, seq, D_MODEL) + d_skip.astype(f32) * u
    y = jax.nn.gelu(y).astype(h.dtype)
    z = y @ w_glu + b_glu
    out = z[..., :D_MODEL] * jax.nn.sigmoid(z[..., D_MODEL:])
    return out, x_r[-1], x_i[-1]


def mem_kv(mem, g, w_k, w_v):
    bsz = mem.shape[0]
    m = rmsnorm(mem, g)
    k = (m @ w_k).reshape(bsz, N_MEM, N_MEM_HEADS, MEM_HEAD_DIM)
    v = (m @ w_v).reshape(bsz, N_MEM, N_MEM_HEADS, MEM_HEAD_DIM)
    return k, v


def mem_cross_attn(h, k, v, w_q, w_o):
    bsz, seq, _ = h.shape
    q = (h @ w_q).reshape(bsz, seq, N_MEM_HEADS, MEM_HEAD_DIM)
    s = jnp.einsum('blhd,bmhd->bhlm', q.astype(jnp.float32), k.astype(jnp.float32)) * (MEM_HEAD_DIM ** -0.5)
    p = jax.nn.softmax(s, axis=-1)
    o = jnp.einsum('bhlm,bmhd->blhd', p, v.astype(jnp.float32)).astype(h.dtype)
    return o.reshape(bsz, seq, D_MODEL) @ w_o


def swiglu(h, w_gate, w_up, w_down):
    return (jax.nn.silu(h @ w_gate) * (h @ w_up)) @ w_down


def moe_swiglu(h, w_router, w_gate, w_up, w_down):
    shp = h.shape
    x = h.reshape(-1, D_MODEL)
    logits = (x @ w_router).astype(jnp.float32)
    top_v, top_i = lax.top_k(logits, TOP_K)
    gates = jax.nn.softmax(top_v, axis=-1)
    combine = jnp.sum(jax.nn.one_hot(top_i, N_EXPERTS, dtype=jnp.float32) * gates[..., None], axis=1)
    out = jnp.zeros(x.shape, jnp.float32)
    for e in range(N_EXPERTS):
        y_e = swiglu(x, w_gate[e], w_up[e], w_down[e]).astype(jnp.float32)
        out = out + combine[:, e:e + 1] * y_e
    return out.astype(h.dtype).reshape(shp)


def setup_inputs(seed: int = 0) -> dict:
    key = jax.random.key(seed)
    ks = iter(jax.random.split(key, 48))
    f32 = jnp.float32

    def nrm(shape, scale):
        return jax.random.normal(next(ks), shape, f32) * scale

    D = D_MODEL
    NC, NS, ND, NM = N_CONV_LAYERS, N_SSM_LAYERS, N_DENSE_LAYERS, N_MOE_LAYERS
    G, P, C = SSM_GROUPS, SSM_STATE, SSM_GROUP
    kv_shape = (DEPTH, DEC_BATCH, N_MEM, N_MEM_HEADS, MEM_HEAD_DIM)
    a_im = jnp.broadcast_to(jnp.pi * jnp.arange(P, dtype=f32), (NS, G, P))
    return {
        "x_prompt": nrm((BATCH, SEQ, D), 1.0),
        "x_sample": nrm((DEC_BATCH, DEC_SEQ, D), 1.0),
        "cache_conv": nrm((NC, DEC_BATCH, CONV_WIDTH - 1, D), 0.5),
        "cache_mem_k": nrm(kv_shape, 1.0),
        "cache_mem_v": nrm(kv_shape, 1.0),
        "state_ssm_re": nrm((NS, DEC_BATCH, G, P), 0.1),
        "state_ssm_im": nrm((NS, DEC_BATCH, G, P), 0.1),
        "mem_prompt": nrm((BATCH, N_MEM, D), 1.0),
        "norm_g": 1.0 + nrm((DEPTH, N_NORMS, D), 0.02),
        "mem_norm_g": 1.0 + nrm((DEPTH, D), 0.02),
        "w_xq": nrm((DEPTH, D, D), D ** -0.5),
        "w_xk": nrm((DEPTH, D, D), D ** -0.5),
        "w_xv": nrm((DEPTH, D, D), D ** -0.5),
        "w_xo": nrm((DEPTH, D, D), D ** -0.5),
        "conv_w_pw1": nrm((NC, D, 2 * D), D ** -0.5),
        "conv_b_pw1": nrm((NC, 2 * D), 0.02),
        "conv_w_dw": nrm((NC, CONV_WIDTH, D), CONV_WIDTH ** -0.5),
        "conv_b_dw": nrm((NC, D), 0.02),
        "conv_ln_g": 1.0 + nrm((NC, D), 0.02),
        "conv_ln_b": nrm((NC, D), 0.02),
        "conv_w_pw2": nrm((NC, D, D), D ** -0.5),
        "conv_b_pw2": nrm((NC, D), 0.02),
        "ssm_a_re": -0.5 + nrm((NS, G, P), 0.01),
        "ssm_a_im": a_im + nrm((NS, G, P), 0.01),
        "ssm_log_dt": jax.random.uniform(next(ks), (NS, G), f32, math.log(DT_MIN), math.log(DT_MAX)),
        "ssm_b_re": nrm((NS, G, P, C), (2 * C) ** -0.5),
        "ssm_b_im": nrm((NS, G, P, C), (2 * C) ** -0.5),
        "ssm_c_re": nrm((NS, G, C, P), P ** -0.5),
        "ssm_c_im": nrm((NS, G, C, P), P ** -0.5),
        "ssm_d": nrm((NS, D), 1.0),
        "ssm_w_glu": nrm((NS, D, 2 * D), D ** -0.5),
        "ssm_b_glu": nrm((NS, 2 * D), 0.02),
        "ffn_w_gate": nrm((ND, D, D_FF), D ** -0.5),
        "ffn_w_up": nrm((ND, D, D_FF), D ** -0.5),
        "ffn_w_down": nrm((ND, D_FF, D), D_FF ** -0.5),
        "moe_w_router": nrm((NM, D, N_EXPERTS), D ** -0.5),
        "moe_w_gate": nrm((NM, N_EXPERTS, D, D_FF_EXPERT), D ** -0.5),
        "moe_w_up": nrm((NM, N_EXPERTS, D, D_FF_EXPERT), D ** -0.5),
        "moe_w_down": nrm((NM, N_EXPERTS, D_FF_EXPERT, D), D_FF_EXPERT ** -0.5),
    }


def reference(x_prompt, x_sample, cache_conv, cache_mem_k, cache_mem_v, state_ssm_re, state_ssm_im,
              mem_prompt, norm_g, mem_norm_g, w_xq, w_xk, w_xv, w_xo,
              conv_w_pw1, conv_b_pw1, conv_w_dw, conv_b_dw, conv_ln_g, conv_ln_b, conv_w_pw2, conv_b_pw2,
              ssm_a_re, ssm_a_im, ssm_log_dt, ssm_b_re, ssm_b_im, ssm_c_re, ssm_c_im, ssm_d,
              ssm_w_glu, ssm_b_glu, ffn_w_gate, ffn_w_up, ffn_w_down,
              moe_w_router, moe_w_gate, moe_w_up, moe_w_down):

    def run_group(x, conv_in, ssm_re_in, ssm_im_in, mem_k, mem_v):
        conv_out, re_out, im_out = [], [], []
        for i in range(DEPTH):
            j = i // N_MIXERS
            g = norm_g[i]
            h = rmsnorm(x, g[0])
            if i % N_MIXERS == 0:
                t, cs = conformer_conv(h, conv_in[j], conv_w_pw1[j], conv_b_pw1[j], conv_w_dw[j],
                                       conv_b_dw[j], conv_ln_g[j], conv_ln_b[j], conv_w_pw2[j], conv_b_pw2[j])
                conv_out.append(cs)
            else:
                t, sr, si = s5_ssm(h, ssm_re_in[j], ssm_im_in[j], ssm_a_re[j], ssm_a_im[j], ssm_log_dt[j],
                                   ssm_b_re[j], ssm_b_im[j], ssm_c_re[j], ssm_c_im[j], ssm_d[j],
                                   ssm_w_glu[j], ssm_b_glu[j])
                re_out.append(sr)
                im_out.append(si)
            x = x + rmsnorm(t, g[1])
            h = rmsnorm(x, g[2])
            x = x + rmsnorm(mem_cross_attn(h, mem_k[i], mem_v[i], w_xq[i], w_xo[i]), g[3])
            h = rmsnorm(x, g[4])
            if i % 2 == 0:
                f = swiglu(h, ffn_w_gate[j], ffn_w_up[j], ffn_w_down[j])
            else:
                f = moe_swiglu(h, moe_w_router[j], moe_w_gate[j], moe_w_up[j], moe_w_down[j])
            x = x + rmsnorm(f, g[5])
        return x, jnp.stack(conv_out), jnp.stack(re_out), jnp.stack(im_out)

    kv = [mem_kv(mem_prompt, mem_norm_g[i], w_xk[i], w_xv[i]) for i in range(DEPTH)]
    p_mem_k = jnp.stack([k for k, _ in kv])
    p_mem_v = jnp.stack([v for _, v in kv])
    conv0 = jnp.zeros((N_CONV_LAYERS, BATCH, CONV_WIDTH - 1, D_MODEL), x_prompt.dtype)
    ssm0 = jnp.zeros((N_SSM_LAYERS, BATCH, SSM_GROUPS, SSM_STATE), jnp.float32)
    y_prompt, p_conv, p_ssm_re, p_ssm_im = run_group(x_prompt, conv0, ssm0, ssm0, p_mem_k, p_mem_v)

    y_sample, s_conv, s_ssm_re, s_ssm_im = run_group(x_sample, cache_conv, state_ssm_re, state_ssm_im,
                                                     cache_mem_k, cache_mem_v)
    return (y_prompt, y_sample, p_conv, p_mem_k, p_mem_v, p_ssm_re, p_ssm_im, s_conv, s_ssm_re, s_ssm_im)
```

```python
import functools
import math

import jax
import jax.numpy as jnp
from jax import lax
from jax.experimental import pallas as pl
from jax.experimental.pallas import tpu as pltpu

F32 = jnp.float32
BF16 = jnp.bfloat16
I32 = jnp.int32

D = 1024
CONV_W = 31
HIST = 32
N_MEM = 256
N_HEADS = 4
HEAD_DIM = D // N_HEADS
SSM_G = 64
SSM_C = 16
SSM_P = 64
SSM_N = SSM_G * SSM_P
SSM_CB = 256
SSM_NCB = D // SSM_CB
SSM_SB = SSM_CB // SSM_C * SSM_P
N_EXP = 8
EPS = 1e-6

V7X_VMEM_LIMIT = 56 * 1024 * 1024
SUBLANES = 8

CHUNK = 256
BLK = 512
WIN = 512


def _params(sem, vmem=None):
    return pltpu.CompilerParams(dimension_semantics=sem, vmem_limit_bytes=vmem)


def _rms(x, g):
    return x * lax.rsqrt(jnp.mean(x * x, axis=-1, keepdims=True) + EPS) * g


def _const_spec(shape):
    nd = len(shape)
    return pl.BlockSpec(shape, lambda *_: (0,) * nd)


def _pw1_kernel(x_ref, g_ref, w_ref, b_ref, u_ref, wbf_ref):
    @pl.when(pl.program_id(0) == 0)
    def _():
        wbf_ref[...] = w_ref[...].astype(BF16)

    h = _rms(x_ref[...], g_ref[...]).astype(BF16)
    z = jnp.dot(h, wbf_ref[...], preferred_element_type=F32) + b_ref[...]
    u_ref[...] = z[:, :D] * jax.nn.sigmoid(z[:, D:])


def _conv_pw1(x, g, w, b, tm):
    t = x.shape[0]
    return pl.pallas_call(
        _pw1_kernel,
        out_shape=jax.ShapeDtypeStruct((t, D), F32),
        grid=(t // tm,),
        in_specs=[pl.BlockSpec((tm, D), lambda i: (i, 0)),
                  _const_spec((1, D)), _const_spec((D, 2 * D)), _const_spec((1, 2 * D))],
        out_specs=pl.BlockSpec((tm, D), lambda i: (i, 0)),
        scratch_shapes=[pltpu.VMEM((D, 2 * D), BF16)],
        compiler_params=_params(("arbitrary",), V7X_VMEM_LIMIT),
        name="conv_pw1",
    )(x, g, w, b)


def _conv2_kernel(u_ref, x_ref, hist_ref, wdw_ref, bdw_ref, lng_ref, lnb_ref, w2_ref, b2_ref,
                  g_ref, o_ref, ext_ref, w2bf_ref, *, tl):
    bi = pl.program_id(0)
    li = pl.program_id(1)

    @pl.when((bi == 0) & (li == 0))
    def _():
        w2bf_ref[...] = w2_ref[...].astype(BF16)

    @pl.when(li == 0)
    def _():
        ext_ref[0:HIST, :] = hist_ref[0]

    @pl.when(li > 0)
    def _():
        ext_ref[0:HIST, :] = ext_ref[tl:tl + HIST, :]

    ext_ref[HIST:HIST + tl, :] = u_ref[0]
    acc = jnp.zeros((tl, D), F32) + bdw_ref[...]
    for k in range(CONV_W):
        acc = acc + wdw_ref[k:k + 1, :] * ext_ref[pl.ds(HIST - CONV_W + 1 + k, tl), :]
    mu = jnp.mean(acc, axis=-1, keepdims=True)
    xc = acc - mu
    var = jnp.mean(xc * xc, axis=-1, keepdims=True)
    y = xc * lax.rsqrt(var + EPS) * lng_ref[...] + lnb_ref[...]
    y = y * jax.nn.sigmoid(y)
    t = jnp.dot(y.astype(BF16), w2bf_ref[...], preferred_element_type=F32) + b2_ref[...]
    o_ref[0] = x_ref[0] + _rms(t, g_ref[...])


def _conv_dw_pw2(u, x, hist, wdw, bdw, lng, lnb, w2, b2, g, tl):
    nb, seq, _ = u.shape
    tok = pl.BlockSpec((1, tl, D), lambda b, l: (b, l, 0))
    return pl.pallas_call(
        functools.partial(_conv2_kernel, tl=tl),
        out_shape=jax.ShapeDtypeStruct((nb, seq, D), F32),
        grid=(nb, seq // tl),
        in_specs=[tok, tok, pl.BlockSpec((1, HIST, D), lambda b, l: (b, 0, 0)),
                  _const_spec((CONV_W, D)), _const_spec((1, D)), _const_spec((1, D)), _const_spec((1, D)),
                  _const_spec((D, D)), _const_spec((1, D)), _const_spec((1, D))],
        out_specs=tok,
        scratch_shapes=[pltpu.VMEM((tl + HIST, D), F32), pltpu.VMEM((D, D), BF16)],
        compiler_params=_params(("arbitrary", "arbitrary"), V7X_VMEM_LIMIT),
        name="conv_dw_pw2",
    )(u, x, hist, wdw, bdw, lng, lnb, w2, b2, g)


def _memkv_kernel(m_ref, g_ref, wk_ref, wv_ref, k_ref, v_ref, wkbf_ref, wvbf_ref):
    @pl.when(pl.program_id(0) == 0)
    def _():
        wkbf_ref[...] = wk_ref[...].astype(BF16)
        wvbf_ref[...] = wv_ref[...].astype(BF16)

    m = _rms(m_ref[...], g_ref[...]).astype(BF16)
    k_ref[...] = jnp.dot(m, wkbf_ref[...], preferred_element_type=F32)
    v_ref[...] = jnp.dot(m, wvbf_ref[...], preferred_element_type=F32)


def _mem_kv(mem, g, wk, wv):
    t = mem.shape[0]
    tm = N_MEM
    row = pl.BlockSpec((tm, D), lambda i: (i, 0))
    return pl.pallas_call(
        _memkv_kernel,
        out_shape=(jax.ShapeDtypeStruct((t, D), F32), jax.ShapeDtypeStruct((t, D), F32)),
        grid=(t // tm,),
        in_specs=[row, _const_spec((1, D)), _const_spec((D, D)), _const_spec((D, D))],
        out_specs=(row, row),
        scratch_shapes=[pltpu.VMEM((D, D), BF16), pltpu.VMEM((D, D), BF16)],
        compiler_params=_params(("arbitrary",), V7X_VMEM_LIMIT),
        name="mem_kv",
    )(mem, g, wk, wv)


def _attn_kernel(x_ref, k_ref, v_ref, wq_ref, wo_ref, g2_ref, g3_ref, o_ref, wqbf_ref, wobf_ref):
    @pl.when((pl.program_id(0) == 0) & (pl.program_id(1) == 0))
    def _():
        wqbf_ref[...] = wq_ref[...].astype(BF16)
        wobf_ref[...] = wo_ref[...].astype(BF16)

    x = x_ref[...]
    h = _rms(x, g2_ref[...]).astype(BF16)
    q = (jnp.dot(h, wqbf_ref[...], preferred_element_type=F32) * (HEAD_DIM ** -0.5)).astype(BF16)
    kb = k_ref[0].astype(BF16)
    vb = v_ref[0].astype(BF16)
    heads = []
    for hd in range(N_HEADS):
        sl = slice(hd * HEAD_DIM, (hd + 1) * HEAD_DIM)
        s = lax.dot_general(q[:, sl], kb[:, sl], (((1,), (1,)), ((), ())), preferred_element_type=F32)
        p = jnp.exp(s - jnp.max(s, axis=-1, keepdims=True))
        p = p / jnp.sum(p, axis=-1, keepdims=True)
        heads.append(jnp.dot(p.astype(BF16), vb[:, sl], preferred_element_type=F32))
    o = jnp.concatenate(heads, axis=1).astype(BF16)
    t = jnp.dot(o, wobf_ref[...], preferred_element_type=F32)
    o_ref[...] = x + _rms(t, g3_ref[...])


def _attn(x2d, x_index, k, v, wq, wo, g2, g3, nb, seq, tm):
    return pl.pallas_call(
        _attn_kernel,
        out_shape=jax.ShapeDtypeStruct((nb * seq, D), F32),
        grid=(nb, seq // tm),
        in_specs=[pl.BlockSpec((tm, D), x_index),
                  pl.BlockSpec((1, N_MEM, D), lambda b, l: (b, 0, 0)),
                  pl.BlockSpec((1, N_MEM, D), lambda b, l: (b, 0, 0)),
                  _const_spec((D, D)), _const_spec((D, D)), _const_spec((1, D)), _const_spec((1, D))],
        out_specs=pl.BlockSpec((tm, D), lambda b, l: (b * (seq // tm) + l, 0)),
        scratch_shapes=[pltpu.VMEM((D, D), BF16), pltpu.VMEM((D, D), BF16)],
        compiler_params=_params(("arbitrary", "arbitrary"), V7X_VMEM_LIMIT),
        name="mem_attn",
    )(x2d, k, v, wq, wo, g2, g3)


def _ffn_kernel(x_ref, g4_ref, g5_ref, wg_ref, wu_ref, wd_ref, o_ref, h_ref, acc_ref):
    f = pl.program_id(1)

    @pl.when(f == 0)
    def _():
        h_ref[...] = _rms(x_ref[...], g4_ref[...]).astype(BF16)
        acc_ref[...] = jnp.zeros_like(acc_ref)

    h = h_ref[...]
    a = jnp.dot(h, wg_ref[...].astype(BF16), preferred_element_type=F32)
    b = jnp.dot(h, wu_ref[...].astype(BF16), preferred_element_type=F32)
    hh = (a * jax.nn.sigmoid(a) * b).astype(BF16)
    acc_ref[...] += jnp.dot(hh, wd_ref[...].astype(BF16), preferred_element_type=F32)

    @pl.when(f == pl.num_programs(1) - 1)
    def _():
        o_ref[...] = x_ref[...] + _rms(acc_ref[...], g5_ref[...])


def _ffn(x, g4, g5, wg, wu, wd, tm, tf, out_shape, out_index):
    t = x.shape[0]
    ff = wg.shape[1]
    return pl.pallas_call(
        _ffn_kernel,
        out_shape=jax.ShapeDtypeStruct(out_shape, F32),
        grid=(t // tm, ff // tf),
        in_specs=[pl.BlockSpec((tm, D), lambda i, f: (i, 0)),
                  _const_spec((1, D)), _const_spec((1, D)),
                  pl.BlockSpec((D, tf), lambda i, f: (0, f)),
                  pl.BlockSpec((D, tf), lambda i, f: (0, f)),
                  pl.BlockSpec((tf, D), lambda i, f: (f, 0))],
        out_specs=pl.BlockSpec((tm, D), out_index),
        scratch_shapes=[pltpu.VMEM((tm, D), BF16), pltpu.VMEM((tm, D), F32)],
        compiler_params=_params(("arbitrary", "arbitrary"), V7X_VMEM_LIMIT),
        name="dense_ffn",
    )(x, g4, g5, wg, wu, wd)


def _ssm_prep_kernel(lr_ref, li_ref, ldt_ref, br_ref, bi_ref, abr_ref, abi_ref, bbr_ref, bbi_ref):
    dt = jnp.exp(ldt_ref[...])
    lr = lr_ref[...]
    li = li_ref[...]
    mag = jnp.exp(lr * dt)
    ab_r = mag * jnp.cos(li * dt)
    ab_i = mag * jnp.sin(li * dt)
    den = lr * lr + li * li
    nr = ab_r - 1.0
    k_r = (nr * lr + ab_i * li) / den
    k_i = (ab_i * lr - nr * li) / den
    br = br_ref[...]
    bi = bi_ref[...]
    abr_ref[...] = ab_r
    abi_ref[...] = ab_i
    bbr_ref[...] = k_r * br - k_i * bi
    bbi_ref[...] = k_r * bi + k_i * br


def _ssm_prep(a_re, a_im, log_dt, b_re, b_im):
    n = SSM_P * SSM_C
    rep = lambda a: jnp.repeat(a, SSM_C, axis=1)
    shp = jax.ShapeDtypeStruct((SSM_G, n), F32)
    abr, abi, bbr, bbi = pl.pallas_call(
        _ssm_prep_kernel,
        out_shape=(shp, shp, shp, shp),
        name="ssm_prep",
    )(rep(a_re), rep(a_im), log_dt.reshape(SSM_G, 1), b_re.reshape(SSM_G, n), b_im.reshape(SSM_G, n))
    pick = lambda a: a.reshape(SSM_G, SSM_P, SSM_C)[:, :, 0].reshape(1, SSM_N)
    return pick(abr), pick(abi), bbr.reshape(SSM_G, SSM_P, SSM_C), bbi.reshape(SSM_G, SSM_P, SSM_C)


def _block_diag(m):
    gpb = SSM_CB // SSM_C
    _, a, b = m.shape
    m = m.reshape(SSM_NCB, gpb, a, 1, b)
    eye = jnp.eye(gpb, dtype=m.dtype).reshape(1, gpb, 1, gpb, 1)
    return (m * eye).reshape(SSM_NCB, gpb * a, gpb * b)


def _ssm_kernel(x_ref, g0_ref, g1_ref, s0r_ref, s0i_ref, ar_ref, ai_ref, bmr_ref, bmi_ref, cmr_ref, cmi_ref,
                d_ref, wglu_ref, bglu_ref, o_ref, sr_ref, si_ref,
                wbf_ref, bur_ref, bui_ref, *, nb, rb, lc):
    @pl.when(pl.program_id(0) == 0)
    def _():
        wbf_ref[...] = wglu_ref[...].astype(BF16)
        sr_ref[...] = s0r_ref[...]
        si_ref[...] = s0i_ref[...]

    x = x_ref[...]
    h = _rms(x, g0_ref[...])
    hb = h.astype(BF16)
    for cb in range(SSM_NCB):
        hs = hb[:, cb * SSM_CB:(cb + 1) * SSM_CB]
        bur_ref[:, cb * SSM_SB:(cb + 1) * SSM_SB] = jnp.dot(hs, bmr_ref[cb], preferred_element_type=F32)
        bui_ref[:, cb * SSM_SB:(cb + 1) * SSM_SB] = jnp.dot(hs, bmi_ref[cb], preferred_element_type=F32)

    rows = max(nb, SUBLANES)
    for c in range(SSM_N // lc):
        cs = slice(c * lc, (c + 1) * lc)
        a_r = jnp.broadcast_to(ar_ref[:, cs], (rows, lc))
        a_i = jnp.broadcast_to(ai_ref[:, cs], (rows, lc))

        if nb == 4:
            low = lax.broadcasted_iota(I32, (SUBLANES, lc), 0) < nb

            def step(j, carry):
                s_r, s_i = carry
                r0 = pl.multiple_of(j * SUBLANES, SUBLANES)
                u_r = bur_ref[pl.ds(r0, SUBLANES), cs]
                u_i = bui_ref[pl.ds(r0, SUBLANES), cs]
                x1r = a_r * s_r - a_i * s_i + u_r
                x1i = a_r * s_i + a_i * s_r + u_i
                p_r = pltpu.roll(x1r, nb, 0)
                p_i = pltpu.roll(x1i, nb, 0)
                x2r = a_r * p_r - a_i * p_i + u_r
                x2i = a_r * p_i + a_i * p_r + u_i
                bur_ref[pl.ds(r0, SUBLANES), cs] = jnp.where(low, x1r, x2r)
                bui_ref[pl.ds(r0, SUBLANES), cs] = jnp.where(low, x1i, x2i)
                return (jnp.where(low, pltpu.roll(x2r, nb, 0), x2r),
                        jnp.where(low, pltpu.roll(x2i, nb, 0), x2i))
        else:
            def step(j, carry):
                s_r, s_i = carry
                r0 = pl.multiple_of(j * nb, SUBLANES)
                u_r = bur_ref[pl.ds(r0, nb), cs]
                u_i = bui_ref[pl.ds(r0, nb), cs]
                n_r = a_r * s_r - a_i * s_i + u_r
                n_i = a_r * s_i + a_i * s_r + u_i
                bur_ref[pl.ds(r0, nb), cs] = n_r
                bui_ref[pl.ds(r0, nb), cs] = n_i
                return n_r, n_i

        s_r, s_i = lax.fori_loop(0, rb // rows, step, (sr_ref[:, cs], si_ref[:, cs]))
        sr_ref[:, cs] = s_r
        si_ref[:, cs] = s_i

    ys = []
    for cb in range(SSM_NCB):
        ss = slice(cb * SSM_SB, (cb + 1) * SSM_SB)
        ys.append(jnp.dot(bur_ref[:, ss].astype(BF16), cmr_ref[cb], preferred_element_type=F32)
                  + jnp.dot(bui_ref[:, ss].astype(BF16), cmi_ref[cb], preferred_element_type=F32))
    y = jnp.concatenate(ys, axis=1) + d_ref[...] * h
    y = jax.nn.gelu(y).astype(BF16)
    z = jnp.dot(y, wbf_ref[...], preferred_element_type=F32) + bglu_ref[...]
    t = z[:, :D] * jax.nn.sigmoid(z[:, D:])
    o_ref[...] = x + _rms(t, g1_ref[...])


def _ssm(x_tm, g0, g1, s0r, s0i, ab_r, ab_i, bm_r, bm_i, cm_r, cm_i, d, wglu, bglu, nb, rb):
    t = x_tm.shape[0]
    rows = max(nb, SUBLANES)
    st = jax.ShapeDtypeStruct((rows, SSM_N), F32)
    row = pl.BlockSpec((rb, D), lambda i: (i, 0))
    return pl.pallas_call(
        functools.partial(_ssm_kernel, nb=nb, rb=rb, lc=512),
        out_shape=(jax.ShapeDtypeStruct((t, D), F32), st, st),
        grid=(t // rb,),
        in_specs=[row, _const_spec((1, D)), _const_spec((1, D)),
                  _const_spec((rows, SSM_N)), _const_spec((rows, SSM_N)),
                  _const_spec((1, SSM_N)), _const_spec((1, SSM_N)),
                  _const_spec((SSM_NCB, SSM_CB, SSM_SB)), _const_spec((SSM_NCB, SSM_CB, SSM_SB)),
                  _const_spec((SSM_NCB, SSM_SB, SSM_CB)), _const_spec((SSM_NCB, SSM_SB, SSM_CB)),
                  _const_spec((1, D)), _const_spec((D, 2 * D)), _const_spec((1, 2 * D))],
        out_specs=(row, _const_spec((rows, SSM_N)), _const_spec((rows, SSM_N))),
        scratch_shapes=[pltpu.VMEM((D, 2 * D), BF16), pltpu.VMEM((rb, SSM_N), F32), pltpu.VMEM((rb, SSM_N), F32)],
        compiler_params=_params(("arbitrary",), V7X_VMEM_LIMIT),
        name="ssm",
    )(x_tm, g0, g1, s0r, s0i, ab_r, ab_i, bm_r, bm_i, cm_r, cm_i, d, wglu, bglu)


def _route_kernel(xp_ref, xs_ref, g_ref, wr_ref, h_ref, rk_ref, cw_ref, cnt_ref, carry_ref, *, n_prompt_chunks):
    i = pl.program_id(0)

    @pl.when(i == 0)
    def _():
        carry_ref[...] = jnp.zeros_like(carry_ref)

    x = jnp.where(i < n_prompt_chunks, xp_ref[...], xs_ref[...])
    h = _rms(x, g_ref[...])
    h_ref[...] = h.astype(BF16)
    lg = jnp.dot(h, wr_ref[...], preferred_element_type=F32, precision=lax.Precision.HIGHEST)
    lane = lax.broadcasted_iota(I32, lg.shape, 1)
    m1 = jnp.max(lg, axis=-1, keepdims=True)
    i1 = jnp.min(jnp.where(lg == m1, lane, N_EXP), axis=-1, keepdims=True)
    first = lane == i1
    lg2 = jnp.where(first, -jnp.inf, lg)
    m2 = jnp.max(lg2, axis=-1, keepdims=True)
    i2 = jnp.min(jnp.where(lg2 == m2, lane, N_EXP), axis=-1, keepdims=True)
    second = lane == i2
    e = jnp.exp(m2 - m1)
    den = 1.0 + e
    cw_ref[...] = jnp.where(first, 1.0 / den, 0.0) + jnp.where(second, e / den, 0.0)
    assigned = first | second
    a = assigned.astype(BF16)
    r = lax.broadcasted_iota(I32, (CHUNK, CHUNK), 0)
    c = lax.broadcasted_iota(I32, (CHUNK, CHUNK), 1)
    before = (c < r).astype(BF16)
    rank = jnp.dot(before, a, preferred_element_type=F32) + carry_ref[...]
    rk_ref[...] = jnp.where(assigned, rank.astype(I32), -1)
    carry_ref[...] += jnp.sum(assigned.astype(F32), axis=0, keepdims=True)
    cnt_ref[0] = carry_ref[...]


def _route(xp, xs, g, wr):
    npc = xp.shape[0] // CHUNK
    nch = npc + xs.shape[0] // CHUNK
    t = nch * CHUNK
    tok8 = pl.BlockSpec((CHUNK, N_EXP), lambda i: (i, 0))
    return pl.pallas_call(
        functools.partial(_route_kernel, n_prompt_chunks=npc),
        out_shape=(jax.ShapeDtypeStruct((t, D), BF16), jax.ShapeDtypeStruct((t, N_EXP), I32),
                   jax.ShapeDtypeStruct((t, N_EXP), F32), jax.ShapeDtypeStruct((nch, 1, N_EXP), F32)),
        grid=(nch,),
        in_specs=[pl.BlockSpec((CHUNK, D), lambda i: (jnp.minimum(i, npc - 1), 0)),
                  pl.BlockSpec((CHUNK, D), lambda i: (jnp.maximum(i - npc, 0), 0)),
                  _const_spec((1, D)), _const_spec((D, N_EXP))],
        out_specs=(pl.BlockSpec((CHUNK, D), lambda i: (i, 0)), tok8, tok8,
                   pl.BlockSpec((1, 1, N_EXP), lambda i: (i, 0, 0))),
        scratch_shapes=[pltpu.VMEM((1, N_EXP), F32)],
        compiler_params=_params(("arbitrary",)),
        name="moe_route",
    )(xp, xs, g, wr)


def _gather_kernel(be_ref, lo_ref, hi_ref, h_hbm, pos_hbm, o_ref, hbuf, pbuf, acc_ref, sem):
    d = pl.program_id(0)
    e = be_ref[d]
    lo = lo_ref[d]
    n = hi_ref[d] - lo + 1

    def copies(cidx, slot):
        return (pltpu.make_async_copy(h_hbm.at[pl.ds(cidx * CHUNK, CHUNK)], hbuf.at[slot], sem.at[0, slot]),
                pltpu.make_async_copy(pos_hbm.at[e, cidx], pbuf.at[slot], sem.at[1, slot]))

    def start(cidx, slot):
        for cp in copies(cidx, slot):
            cp.start()

    acc_ref[...] = jnp.zeros_like(acc_ref)

    @pl.when(n > 0)
    def _():
        start(lo, 0)

    dest = d * BLK + lax.broadcasted_iota(I32, (BLK, CHUNK), 0)

    def body(j, carry):
        slot = j % 2
        for cp in copies(lo + j, slot):
            cp.wait()

        @pl.when(j + 1 < n)
        def _():
            start(lo + j + 1, 1 - slot)

        onehot = (pbuf[slot] == dest).astype(BF16)
        acc_ref[...] += jnp.dot(onehot, hbuf[slot], preferred_element_type=F32)
        return carry

    lax.fori_loop(0, n, body, 0)
    o_ref[...] = acc_ref[...].astype(BF16)


def _gather(h, pos_t, blk_e, c_lo, c_hi, nblk):
    return pl.pallas_call(
        _gather_kernel,
        out_shape=jax.ShapeDtypeStruct((nblk * BLK, D), BF16),
        grid_spec=pltpu.PrefetchScalarGridSpec(
            num_scalar_prefetch=3,
            grid=(nblk,),
            in_specs=[pl.BlockSpec(memory_space=pl.ANY), pl.BlockSpec(memory_space=pl.ANY)],
            out_specs=pl.BlockSpec((BLK, D), lambda d, *_: (d, 0)),
            scratch_shapes=[pltpu.VMEM((2, CHUNK, D), BF16), pltpu.VMEM((2, 1, CHUNK), I32),
                            pltpu.VMEM((BLK, D), F32), pltpu.SemaphoreType.DMA((2, 2))]),
        compiler_params=_params(("arbitrary",)),
        name="moe_gather",
    )(blk_e, c_lo, c_hi, h, pos_t)


def _expert_up_kernel(be_ref, x_ref, wg_ref, wu_ref, o_ref, wgbf_ref, wubf_ref):
    d = pl.program_id(1)
    changed = (d == 0) | (be_ref[d] != be_ref[jnp.maximum(d - 1, 0)])

    @pl.when(changed)
    def _():
        wgbf_ref[...] = wg_ref[0].astype(BF16)
        wubf_ref[...] = wu_ref[0].astype(BF16)

    x = x_ref[...]
    a = jnp.dot(x, wgbf_ref[...], preferred_element_type=F32)
    b = jnp.dot(x, wubf_ref[...], preferred_element_type=F32)
    o_ref[...] = (a * jax.nn.sigmoid(a) * b).astype(BF16)


def _expert_up(xs, blk_e, wg, wu, tf):
    p = xs.shape[0]
    ff = wg.shape[2]
    return pl.pallas_call(
        _expert_up_kernel,
        out_shape=jax.ShapeDtypeStruct((p, ff), BF16),
        grid_spec=pltpu.PrefetchScalarGridSpec(
            num_scalar_prefetch=1,
            grid=(ff // tf, p // BLK),
            in_specs=[pl.BlockSpec((BLK, D), lambda f, d, be: (d, 0)),
                      pl.BlockSpec((1, D, tf), lambda f, d, be: (be[d], 0, f)),
                      pl.BlockSpec((1, D, tf), lambda f, d, be: (be[d], 0, f))],
            out_specs=pl.BlockSpec((BLK, tf), lambda f, d, be: (d, f)),
            scratch_shapes=[pltpu.VMEM((D, tf), BF16), pltpu.VMEM((D, tf), BF16)]),
        compiler_params=_params(("arbitrary", "arbitrary"), V7X_VMEM_LIMIT),
        name="moe_up",
    )(blk_e, xs, wg, wu)


def _expert_down_kernel(be_ref, h_ref, wd_ref, o_ref, wdbf_ref):
    d = pl.program_id(1)
    changed = (d == 0) | (be_ref[d] != be_ref[jnp.maximum(d - 1, 0)])

    @pl.when(changed)
    def _():
        wdbf_ref[...] = wd_ref[0].astype(BF16)

    o_ref[...] = jnp.dot(h_ref[...], wdbf_ref[...], preferred_element_type=F32).astype(BF16)


def _expert_down(hh, blk_e, wd, tn):
    p, ff = hh.shape
    return pl.pallas_call(
        _expert_down_kernel,
        out_shape=jax.ShapeDtypeStruct((p, D), BF16),
        grid_spec=pltpu.PrefetchScalarGridSpec(
            num_scalar_prefetch=1,
            grid=(D // tn, p // BLK),
            in_specs=[pl.BlockSpec((BLK, ff), lambda n, d, be: (d, 0)),
                      pl.BlockSpec((1, ff, tn), lambda n, d, be: (be[d], 0, n))],
            out_specs=pl.BlockSpec((BLK, tn), lambda n, d, be: (d, n)),
            scratch_shapes=[pltpu.VMEM((ff, tn), BF16)]),
        compiler_params=_params(("arbitrary", "arbitrary"), V7X_VMEM_LIMIT),
        name="moe_down",
    )(blk_e, hh, wd)


def _combine_kernel(ws_ref, xp_ref, xs_ref, pos_ref, cw_ref, g_ref, y_hbm, op_ref, os_ref, wbuf, sem,
                    *, n_prompt_chunks):
    i = pl.program_id(0)
    nch = pl.num_programs(0)

    def copy(chunk, e, slot):
        start = pl.multiple_of(ws_ref[chunk * N_EXP + e], CHUNK)
        return pltpu.make_async_copy(y_hbm.at[pl.ds(start, WIN)], wbuf.at[slot, e], sem.at[slot, e])

    @pl.when(i == 0)
    def _():
        for e in range(N_EXP):
            copy(0, e, 0).start()

    slot = i % 2

    @pl.when(i + 1 < nch)
    def _():
        for e in range(N_EXP):
            copy(i + 1, e, 1 - slot).start()

    pos = pos_ref[...]
    cw = cw_ref[...]
    col = lax.broadcasted_iota(I32, (CHUNK, WIN), 1)
    acc = jnp.zeros((CHUNK, D), F32)
    for e in range(N_EXP):
        copy(i, e, slot).wait()
        rel = pos[:, e:e + 1] - ws_ref[i * N_EXP + e]
        onehot = (rel == col).astype(BF16)
        acc = acc + cw[:, e:e + 1] * jnp.dot(onehot, wbuf[slot, e], preferred_element_type=F32)

    x = jnp.where(i < n_prompt_chunks, xp_ref[...], xs_ref[...])
    out = x + _rms(acc, g_ref[...])

    @pl.when(i < n_prompt_chunks)
    def _():
        op_ref[...] = out

    @pl.when(i >= n_prompt_chunks)
    def _():
        os_ref[...] = out


def _combine(ws, xp, xs, pos, cw, g, ys):
    npc = xp.shape[0] // CHUNK
    nsc = xs.shape[0] // CHUNK
    nch = npc + nsc
    tok8 = pl.BlockSpec((CHUNK, N_EXP), lambda i, *_: (i, 0))
    pspec = pl.BlockSpec((CHUNK, D), lambda i, *_: (jnp.minimum(i, npc - 1), 0))
    sspec = pl.BlockSpec((CHUNK, D), lambda i, *_: (jnp.maximum(i - npc, 0), 0))
    return pl.pallas_call(
        functools.partial(_combine_kernel, n_prompt_chunks=npc),
        out_shape=(jax.ShapeDtypeStruct(xp.shape, F32), jax.ShapeDtypeStruct(xs.shape, F32)),
        grid_spec=pltpu.PrefetchScalarGridSpec(
            num_scalar_prefetch=1,
            grid=(nch,),
            in_specs=[pspec, sspec, tok8, tok8, pl.BlockSpec((1, D), lambda i, *_: (0, 0)),
                      pl.BlockSpec(memory_space=pl.ANY)],
            out_specs=(pspec, sspec),
            scratch_shapes=[pltpu.VMEM((2, N_EXP, WIN, D), BF16), pltpu.SemaphoreType.DMA((2, N_EXP))]),
        compiler_params=_params(("arbitrary",), V7X_VMEM_LIMIT),
        name="moe_combine",
    )(ws, xp, xs, pos, cw, g, ys)


def _moe(xp, xs, g4, g5, w_router, wg, wu, wd):
    npc = xp.shape[0] // CHUNK
    nch = npc + xs.shape[0] // CHUNK
    t = nch * CHUNK
    h, rk, cw, cnt = _route(xp, xs, g4, w_router)

    after = cnt.reshape(nch, N_EXP).astype(I32)
    before = jnp.concatenate([jnp.zeros((1, N_EXP), I32), after[:-1]], axis=0)
    counts = after[-1]
    gsz = (counts + BLK - 1) // BLK * BLK
    gend = jnp.cumsum(gsz)
    gstart = gend - gsz
    nblk = (2 * t + N_EXP * (BLK - 1)) // BLK + 1 + WIN // BLK
    bstart = jnp.arange(nblk, dtype=I32) * BLK
    blk_e = jnp.minimum(jnp.sum(gend[None, :] <= bstart[:, None], axis=1), N_EXP - 1).astype(I32)
    valid = bstart < gend[-1]
    r0 = bstart - gstart[blk_e]
    aft_e = after[:, blk_e]
    bef_e = before[:, blk_e]
    c_lo = jnp.where(valid, jnp.sum(aft_e <= r0[None, :], axis=0), 0).astype(I32)
    c_hi = jnp.where(valid, jnp.sum(bef_e < (r0 + BLK)[None, :], axis=0) - 1, -1).astype(I32)
    c_lo = jnp.minimum(c_lo, nch - 1)
    pos = jnp.where(rk >= 0, rk + gstart[None, :], -1).astype(I32)
    pos_t = pos.T.reshape(N_EXP, nch, 1, CHUNK)
    ws = ((gstart[None, :] + before) // CHUNK * CHUNK).astype(I32).reshape(-1)

    xsort = _gather(h, pos_t, blk_e, c_lo, c_hi, nblk)
    hh = _expert_up(xsort, blk_e, wg, wu, tf=896)
    ysort = _expert_down(hh, blk_e, wd, tn=512)
    return _combine(ws, xp, xs, pos, cw, g5, ysort)


def kernel(x_prompt, x_sample, cache_conv, cache_mem_k, cache_mem_v, state_ssm_re, state_ssm_im, mem_prompt, norm_g, mem_norm_g, w_xq, w_xk, w_xv, w_xo, conv_w_pw1, conv_b_pw1, conv_w_dw, conv_b_dw, conv_ln_g, conv_ln_b, conv_w_pw2, conv_b_pw2, ssm_a_re, ssm_a_im, ssm_log_dt, ssm_b_re, ssm_b_im, ssm_c_re, ssm_c_im, ssm_d, ssm_w_glu, ssm_b_glu, ffn_w_gate, ffn_w_up, ffn_w_down, moe_w_router, moe_w_gate, moe_w_up, moe_w_down):
    nbp, seqp, _ = x_prompt.shape
    nbs, seqs, _ = x_sample.shape
    tp = nbp * seqp
    ts = nbs * seqs
    row = lambda a: a.reshape(1, -1)
    g = lambda i, k: norm_g[i, k].reshape(1, D)

    mem2d = mem_prompt.reshape(nbp * N_MEM, D)
    kv = [_mem_kv(mem2d, row(mem_norm_g[i]), w_xk[i], w_xv[i]) for i in range(2)]
    p_mem_k = jnp.stack([k for k, _ in kv]).reshape(2, nbp, N_MEM, N_HEADS, HEAD_DIM)
    p_mem_v = jnp.stack([v for _, v in kv]).reshape(2, nbp, N_MEM, N_HEADS, HEAD_DIM)

    conv_args = (conv_w_dw[0], row(conv_b_dw[0]), row(conv_ln_g[0]), row(conv_ln_b[0]),
                 conv_w_pw2[0], row(conv_b_pw2[0]), g(0, 1))
    xp = x_prompt.reshape(tp, D)
    xs = x_sample.reshape(ts, D)
    up = _conv_pw1(xp, g(0, 0), conv_w_pw1[0], row(conv_b_pw1[0]), tm=512).reshape(nbp, seqp, D)
    us = _conv_pw1(xs, g(0, 0), conv_w_pw1[0], row(conv_b_pw1[0]), tm=ts).reshape(nbs, seqs, D)
    hist_p = jnp.zeros((nbp, HIST, D), F32)
    hist_s = jnp.pad(cache_conv[0], ((0, 0), (HIST - CONV_W + 1, 0), (0, 0)))
    xp = _conv_dw_pw2(up, x_prompt, hist_p, *conv_args, tl=256).reshape(tp, D)
    xs = _conv_dw_pw2(us, x_sample, hist_s, *conv_args, tl=seqs).reshape(ts, D)
    p_conv = up[:, seqp - (CONV_W - 1):][None]
    s_conv = jnp.concatenate([cache_conv[0], us], axis=1)[:, -(CONV_W - 1):][None]

    tma = 256
    xp = _attn(xp, lambda b, l: (b * (seqp // tma) + l, 0), kv[0][0].reshape(nbp, N_MEM, D),
               kv[0][1].reshape(nbp, N_MEM, D), w_xq[0], w_xo[0], g(0, 2), g(0, 3), nbp, seqp, tma)
    xs = _attn(xs, lambda b, l: (b, 0), cache_mem_k[0].reshape(nbs, N_MEM, D),
               cache_mem_v[0].reshape(nbs, N_MEM, D), w_xq[0], w_xo[0], g(0, 2), g(0, 3), nbs, seqs, seqs)

    tmf = 1024
    ffn_w = (ffn_w_gate[0], ffn_w_up[0], ffn_w_down[0])
    xp_tm = _ffn(xp, g(0, 4), g(0, 5), *ffn_w, tm=tmf, tf=256, out_shape=(seqp, nbp * D),
                 out_index=lambda i, f: (i % (seqp // tmf), i // (seqp // tmf))).reshape(tp, D)
    xs = _ffn(xs, g(0, 4), g(0, 5), *ffn_w, tm=ts, tf=256, out_shape=(ts, D), out_index=lambda i, f: (i, 0))
    xs_tm = xs.reshape(nbs, seqs, D).transpose(1, 0, 2).reshape(ts, D)

    ab_r, ab_i, bb_r, bb_i = _ssm_prep(ssm_a_re[0], ssm_a_im[0], ssm_log_dt[0], ssm_b_re[0], ssm_b_im[0])
    bm_r = _block_diag(bb_r.transpose(0, 2, 1)).astype(BF16)
    bm_i = _block_diag(bb_i.transpose(0, 2, 1)).astype(BF16)
    cm_r = _block_diag(ssm_c_re[0].transpose(0, 2, 1)).astype(BF16)
    cm_i = _block_diag(-ssm_c_im[0].transpose(0, 2, 1)).astype(BF16)
    ssm_args = (ab_r, ab_i, bm_r, bm_i, cm_r, cm_i, row(ssm_d[0]), ssm_w_glu[0], row(ssm_b_glu[0]))
    zero_state = jnp.zeros((SUBLANES, SSM_N), F32)
    xp_tm, p_sr, p_si = _ssm(xp_tm, g(1, 0), g(1, 1), zero_state, zero_state, *ssm_args, nb=nbp, rb=256)
    xs_tm, s_sr, s_si = _ssm(xs_tm, g(1, 0), g(1, 1), state_ssm_re[0].reshape(nbs, SSM_N),
                             state_ssm_im[0].reshape(nbs, SSM_N), *ssm_args, nb=nbs, rb=ts)
    st = lambda a, n: a[:n].reshape(1, n, SSM_G, SSM_P)
    p_ssm_re, p_ssm_im = st(p_sr, nbp), st(p_si, nbp)
    s_ssm_re, s_ssm_im = st(s_sr, nbs), st(s_si, nbs)

    xp = _attn(xp_tm.reshape(seqp, nbp * D), lambda b, l: (l, b), kv[1][0].reshape(nbp, N_MEM, D),
               kv[1][1].reshape(nbp, N_MEM, D), w_xq[1], w_xo[1], g(1, 2), g(1, 3), nbp, seqp, tma)
    xs = _attn(xs_tm.reshape(seqs, nbs * D), lambda b, l: (0, b), cache_mem_k[1].reshape(nbs, N_MEM, D),
               cache_mem_v[1].reshape(nbs, N_MEM, D), w_xq[1], w_xo[1], g(1, 2), g(1, 3), nbs, seqs, seqs)

    yp, ysm = _moe(xp, xs, g(1, 4), g(1, 5), moe_w_router[0], moe_w_gate[0], moe_w_up[0], moe_w_down[0])
    return (yp.reshape(nbp, seqp, D), ysm.reshape(nbs, seqs, D), p_conv, p_mem_k, p_mem_v,
            p_ssm_re, p_ssm_im, s_conv, s_ssm_re, s_ssm_im)
```

```python
import functools
import math

import jax
import jax.numpy as jnp
from jax import lax
from jax.experimental import pallas as pl
from jax.experimental.pallas import tpu as pltpu

F32 = jnp.float32
BF16 = jnp.bfloat16
I32 = jnp.int32

D = 1024
CONV_W = 31
HIST = 32
N_MEM = 256
N_HEADS = 4
HEAD_DIM = D // N_HEADS
SSM_G = 64
SSM_C = 16
SSM_P = 64
SSM_N = SSM_G * SSM_P
SSM_CB = 256
SSM_NCB = D // SSM_CB
SSM_SB = SSM_CB // SSM_C * SSM_P
N_EXP = 8
EPS = 1e-6

V7X_VMEM_LIMIT = 56 * 1024 * 1024
SUBLANES = 8

CHUNK = 256
BLK = 512
WIN = 512
GWIN = 128


def _params(sem, vmem=None):
    return pltpu.CompilerParams(dimension_semantics=sem, vmem_limit_bytes=vmem)


def _rms(x, g):
    return x * lax.rsqrt(jnp.mean(x * x, axis=-1, keepdims=True) + EPS) * g


def _const_spec(shape):
    nd = len(shape)
    return pl.BlockSpec(shape, lambda *_: (0,) * nd)


def _pw1_kernel(x_ref, g_ref, w_ref, b_ref, u_ref, wbf_ref):
    @pl.when(pl.program_id(0) == 0)
    def _():
        wbf_ref[...] = w_ref[...].astype(BF16)

    h = _rms(x_ref[...], g_ref[...]).astype(BF16)
    z = jnp.dot(h, wbf_ref[...], preferred_element_type=F32) + b_ref[...]
    u_ref[...] = z[:, :D] * jax.nn.sigmoid(z[:, D:])


def _conv_pw1(x, g, w, b, tm):
    t = x.shape[0]
    return pl.pallas_call(
        _pw1_kernel,
        out_shape=jax.ShapeDtypeStruct((t, D), F32),
        grid=(t // tm,),
        in_specs=[pl.BlockSpec((tm, D), lambda i: (i, 0)),
                  _const_spec((1, D)), _const_spec((D, 2 * D)), _const_spec((1, 2 * D))],
        out_specs=pl.BlockSpec((tm, D), lambda i: (i, 0)),
        scratch_shapes=[pltpu.VMEM((D, 2 * D), BF16)],
        compiler_params=_params(("arbitrary",), V7X_VMEM_LIMIT),
        name="conv_pw1",
    )(x, g, w, b)


def _conv2_kernel(u_ref, x_ref, hist_ref, wdw_ref, bdw_ref, lng_ref, lnb_ref, w2_ref, b2_ref,
                  g_ref, o_ref, ext_ref, sh_ref, conv_ref, w2bf_ref, *, tl, rt):
    bi = pl.program_id(0)
    li = pl.program_id(1)

    @pl.when((bi == 0) & (li == 0))
    def _():
        w2bf_ref[...] = w2_ref[...].astype(BF16)

    @pl.when(li == 0)
    def _():
        ext_ref[0:HIST, :] = hist_ref[0]

    @pl.when(li > 0)
    def _():
        ext_ref[0:HIST, :] = ext_ref[tl:tl + HIST, :]

    ext_ref[HIST:HIST + tl, :] = u_ref[0]
    for s in range(1, SUBLANES):
        sh_ref[s - 1] = ext_ref[pl.ds(s, tl + HIST - SUBLANES), :]

    def rows(i, carry):
        r0 = pl.multiple_of(i * rt, rt)
        acc = jnp.zeros((rt, D), F32) + bdw_ref[...]
        for k in range(CONV_W):
            off = HIST - CONV_W + 1 + k
            s = off % SUBLANES
            base = off - s
            if s == 0:
                src = ext_ref[pl.ds(r0 + base, rt), :]
            else:
                src = sh_ref[s - 1, pl.ds(r0 + base, rt), :]
            acc = acc + wdw_ref[k:k + 1, :] * src
        conv_ref[pl.ds(r0, rt), :] = acc
        return carry

    lax.fori_loop(0, tl // rt, rows, 0)
    acc = conv_ref[...]
    mu = jnp.mean(acc, axis=-1, keepdims=True)
    xc = acc - mu
    var = jnp.mean(xc * xc, axis=-1, keepdims=True)
    y = xc * lax.rsqrt(var + EPS) * lng_ref[...] + lnb_ref[...]
    y = y * jax.nn.sigmoid(y)
    t = jnp.dot(y.astype(BF16), w2bf_ref[...], preferred_element_type=F32) + b2_ref[...]
    o_ref[0] = x_ref[0] + _rms(t, g_ref[...])


def _conv_dw_pw2(u, x, hist, wdw, bdw, lng, lnb, w2, b2, g, tl):
    nb, seq, _ = u.shape
    tok = pl.BlockSpec((1, tl, D), lambda b, l: (b, l, 0))
    return pl.pallas_call(
        functools.partial(_conv2_kernel, tl=tl, rt=min(tl, 32)),
        out_shape=jax.ShapeDtypeStruct((nb, seq, D), F32),
        grid=(nb, seq // tl),
        in_specs=[tok, tok, pl.BlockSpec((1, HIST, D), lambda b, l: (b, 0, 0)),
                  _const_spec((CONV_W, D)), _const_spec((1, D)), _const_spec((1, D)), _const_spec((1, D)),
                  _const_spec((D, D)), _const_spec((1, D)), _const_spec((1, D))],
        out_specs=tok,
        scratch_shapes=[pltpu.VMEM((tl + HIST, D), F32),
                        pltpu.VMEM((SUBLANES - 1, tl + HIST - SUBLANES, D), F32),
                        pltpu.VMEM((tl, D), F32), pltpu.VMEM((D, D), BF16)],
        compiler_params=_params(("arbitrary", "arbitrary"), V7X_VMEM_LIMIT),
        name="conv_dw_pw2",
    )(u, x, hist, wdw, bdw, lng, lnb, w2, b2, g)


def _memkv_kernel(m_ref, g_ref, wk_ref, wv_ref, k_ref, v_ref, wkbf_ref, wvbf_ref):
    @pl.when(pl.program_id(0) == 0)
    def _():
        wkbf_ref[...] = wk_ref[...].astype(BF16)
        wvbf_ref[...] = wv_ref[...].astype(BF16)

    m = _rms(m_ref[...], g_ref[...]).astype(BF16)
    k_ref[...] = jnp.dot(m, wkbf_ref[...], preferred_element_type=F32)
    v_ref[...] = jnp.dot(m, wvbf_ref[...], preferred_element_type=F32)


def _mem_kv(mem, g, wk, wv):
    t = mem.shape[0]
    tm = N_MEM
    row = pl.BlockSpec((tm, D), lambda i: (i, 0))
    return pl.pallas_call(
        _memkv_kernel,
        out_shape=(jax.ShapeDtypeStruct((t, D), F32), jax.ShapeDtypeStruct((t, D), F32)),
        grid=(t // tm,),
        in_specs=[row, _const_spec((1, D)), _const_spec((D, D)), _const_spec((D, D))],
        out_specs=(row, row),
        scratch_shapes=[pltpu.VMEM((D, D), BF16), pltpu.VMEM((D, D), BF16)],
        compiler_params=_params(("arbitrary",), V7X_VMEM_LIMIT),
        name="mem_kv",
    )(mem, g, wk, wv)


def _attn_kernel(x_ref, k_ref, v_ref, wq_ref, wo_ref, g2_ref, g3_ref, o_ref, wqbf_ref, wobf_ref):
    @pl.when((pl.program_id(0) == 0) & (pl.program_id(1) == 0))
    def _():
        wqbf_ref[...] = wq_ref[...].astype(BF16)
        wobf_ref[...] = wo_ref[...].astype(BF16)

    x = x_ref[...]
    h = _rms(x, g2_ref[...]).astype(BF16)
    q = (jnp.dot(h, wqbf_ref[...], preferred_element_type=F32) * (HEAD_DIM ** -0.5)).astype(BF16)
    kb = k_ref[0].astype(BF16)
    vb = v_ref[0].astype(BF16)
    heads = []
    for hd in range(N_HEADS):
        sl = slice(hd * HEAD_DIM, (hd + 1) * HEAD_DIM)
        s = lax.dot_general(q[:, sl], kb[:, sl], (((1,), (1,)), ((), ())), preferred_element_type=F32)
        p = jnp.exp(s - jnp.max(s, axis=-1, keepdims=True))
        p = p / jnp.sum(p, axis=-1, keepdims=True)
        heads.append(jnp.dot(p.astype(BF16), vb[:, sl], preferred_element_type=F32))
    o = jnp.concatenate(heads, axis=1).astype(BF16)
    t = jnp.dot(o, wobf_ref[...], preferred_element_type=F32)
    o_ref[...] = x + _rms(t, g3_ref[...])


def _attn(x2d, x_index, k, v, wq, wo, g2, g3, nb, seq, tm):
    return pl.pallas_call(
        _attn_kernel,
        out_shape=jax.ShapeDtypeStruct((nb * seq, D), F32),
        grid=(nb, seq // tm),
        in_specs=[pl.BlockSpec((tm, D), x_index),
                  pl.BlockSpec((1, N_MEM, D), lambda b, l: (b, 0, 0)),
                  pl.BlockSpec((1, N_MEM, D), lambda b, l: (b, 0, 0)),
                  _const_spec((D, D)), _const_spec((D, D)), _const_spec((1, D)), _const_spec((1, D))],
        out_specs=pl.BlockSpec((tm, D), lambda b, l: (b * (seq // tm) + l, 0)),
        scratch_shapes=[pltpu.VMEM((D, D), BF16), pltpu.VMEM((D, D), BF16)],
        compiler_params=_params(("arbitrary", "arbitrary"), V7X_VMEM_LIMIT),
        name="mem_attn",
    )(x2d, k, v, wq, wo, g2, g3)


def _ffn_kernel(x_ref, g4_ref, g5_ref, wg_ref, wu_ref, wd_ref, o_ref, h_ref, acc_ref):
    f = pl.program_id(1)

    @pl.when(f == 0)
    def _():
        h_ref[...] = _rms(x_ref[...], g4_ref[...]).astype(BF16)
        acc_ref[...] = jnp.zeros_like(acc_ref)

    h = h_ref[...]
    a = jnp.dot(h, wg_ref[...].astype(BF16), preferred_element_type=F32)
    b = jnp.dot(h, wu_ref[...].astype(BF16), preferred_element_type=F32)
    hh = (a * jax.nn.sigmoid(a) * b).astype(BF16)
    acc_ref[...] += jnp.dot(hh, wd_ref[...].astype(BF16), preferred_element_type=F32)

    @pl.when(f == pl.num_programs(1) - 1)
    def _():
        o_ref[...] = x_ref[...] + _rms(acc_ref[...], g5_ref[...])


def _ffn(x, g4, g5, wg, wu, wd, tm, tf, out_shape, out_index):
    t = x.shape[0]
    ff = wg.shape[1]
    return pl.pallas_call(
        _ffn_kernel,
        out_shape=jax.ShapeDtypeStruct(out_shape, F32),
        grid=(t // tm, ff // tf),
        in_specs=[pl.BlockSpec((tm, D), lambda i, f: (i, 0)),
                  _const_spec((1, D)), _const_spec((1, D)),
                  pl.BlockSpec((D, tf), lambda i, f: (0, f)),
                  pl.BlockSpec((D, tf), lambda i, f: (0, f)),
                  pl.BlockSpec((tf, D), lambda i, f: (f, 0))],
        out_specs=pl.BlockSpec((tm, D), out_index),
        scratch_shapes=[pltpu.VMEM((tm, D), BF16), pltpu.VMEM((tm, D), F32)],
        compiler_params=_params(("arbitrary", "arbitrary"), V7X_VMEM_LIMIT),
        name="dense_ffn",
    )(x, g4, g5, wg, wu, wd)


def _ssm_prep_kernel(lr_ref, li_ref, ldt_ref, br_ref, bi_ref, abr_ref, abi_ref, bbr_ref, bbi_ref):
    dt = jnp.exp(ldt_ref[...])
    lr = lr_ref[...]
    li = li_ref[...]
    mag = jnp.exp(lr * dt)
    ab_r = mag * jnp.cos(li * dt)
    ab_i = mag * jnp.sin(li * dt)
    den = lr * lr + li * li
    nr = ab_r - 1.0
    k_r = (nr * lr + ab_i * li) / den
    k_i = (ab_i * lr - nr * li) / den
    br = br_ref[...]
    bi = bi_ref[...]
    abr_ref[...] = ab_r
    abi_ref[...] = ab_i
    bbr_ref[...] = k_r * br - k_i * bi
    bbi_ref[...] = k_r * bi + k_i * br


def _ssm_prep(a_re, a_im, log_dt, b_re, b_im):
    n = SSM_P * SSM_C
    rep = lambda a: jnp.repeat(a, SSM_C, axis=1)
    shp = jax.ShapeDtypeStruct((SSM_G, n), F32)
    abr, abi, bbr, bbi = pl.pallas_call(
        _ssm_prep_kernel,
        out_shape=(shp, shp, shp, shp),
        name="ssm_prep",
    )(rep(a_re), rep(a_im), log_dt.reshape(SSM_G, 1), b_re.reshape(SSM_G, n), b_im.reshape(SSM_G, n))
    pick = lambda a: a.reshape(SSM_G, SSM_P, SSM_C)[:, :, 0].reshape(1, SSM_N)
    return pick(abr), pick(abi), bbr.reshape(SSM_G, SSM_P, SSM_C), bbi.reshape(SSM_G, SSM_P, SSM_C)


def _block_diag(m):
    gpb = SSM_CB // SSM_C
    _, a, b = m.shape
    m = m.reshape(SSM_NCB, gpb, a, 1, b)
    eye = jnp.eye(gpb, dtype=m.dtype).reshape(1, gpb, 1, gpb, 1)
    return (m * eye).reshape(SSM_NCB, gpb * a, gpb * b)


def _ssm_kernel(x_ref, g0_ref, g1_ref, s0r_ref, s0i_ref, ar_ref, ai_ref, bmr_ref, bmi_ref, cmr_ref, cmi_ref,
                d_ref, wglu_ref, bglu_ref, o_ref, sr_ref, si_ref,
                wbf_ref, bur_ref, bui_ref, *, nb, rb, lc):
    @pl.when(pl.program_id(0) == 0)
    def _():
        wbf_ref[...] = wglu_ref[...].astype(BF16)
        sr_ref[...] = s0r_ref[...]
        si_ref[...] = s0i_ref[...]

    tl = rb // nb
    x = x_ref[...].reshape(rb, D)
    h = _rms(x, g0_ref[...])
    r = lax.broadcasted_iota(I32, (rb, rb), 0)
    c = lax.broadcasted_iota(I32, (rb, rb), 1)
    to_time_major = (c == (r & (nb - 1)) * tl + (r >> (nb.bit_length() - 1))).astype(BF16)
    to_batch_major = (c == (r & (tl - 1)) * nb + (r >> (tl.bit_length() - 1))).astype(BF16)
    hb = jnp.dot(to_time_major, h.astype(BF16), preferred_element_type=F32).astype(BF16)
    for cb in range(SSM_NCB):
        hs = hb[:, cb * SSM_CB:(cb + 1) * SSM_CB]
        bur_ref[:, cb * SSM_SB:(cb + 1) * SSM_SB] = jnp.dot(hs, bmr_ref[cb], preferred_element_type=F32)
        bui_ref[:, cb * SSM_SB:(cb + 1) * SSM_SB] = jnp.dot(hs, bmi_ref[cb], preferred_element_type=F32)

    rows = max(nb, SUBLANES)
    for c in range(SSM_N // lc):
        cs = slice(c * lc, (c + 1) * lc)
        a_r = jnp.broadcast_to(ar_ref[:, cs], (rows, lc))
        a_i = jnp.broadcast_to(ai_ref[:, cs], (rows, lc))

        if nb == 4:
            low = lax.broadcasted_iota(I32, (SUBLANES, lc), 0) < nb

            def step(j, carry):
                s_r, s_i = carry
                r0 = pl.multiple_of(j * SUBLANES, SUBLANES)
                u_r = bur_ref[pl.ds(r0, SUBLANES), cs]
                u_i = bui_ref[pl.ds(r0, SUBLANES), cs]
                x1r = a_r * s_r - a_i * s_i + u_r
                x1i = a_r * s_i + a_i * s_r + u_i
                p_r = pltpu.roll(x1r, nb, 0)
                p_i = pltpu.roll(x1i, nb, 0)
                x2r = a_r * p_r - a_i * p_i + u_r
                x2i = a_r * p_i + a_i * p_r + u_i
                bur_ref[pl.ds(r0, SUBLANES), cs] = jnp.where(low, x1r, x2r)
                bui_ref[pl.ds(r0, SUBLANES), cs] = jnp.where(low, x1i, x2i)
                return (jnp.where(low, pltpu.roll(x2r, nb, 0), x2r),
                        jnp.where(low, pltpu.roll(x2i, nb, 0), x2i))
        else:
            def step(j, carry):
                s_r, s_i = carry
                r0 = pl.multiple_of(j * nb, SUBLANES)
                u_r = bur_ref[pl.ds(r0, nb), cs]
                u_i = bui_ref[pl.ds(r0, nb), cs]
                n_r = a_r * s_r - a_i * s_i + u_r
                n_i = a_r * s_i + a_i * s_r + u_i
                bur_ref[pl.ds(r0, nb), cs] = n_r
                bui_ref[pl.ds(r0, nb), cs] = n_i
                return n_r, n_i

        s_r, s_i = lax.fori_loop(0, rb // rows, step, (sr_ref[:, cs], si_ref[:, cs]))
        sr_ref[:, cs] = s_r
        si_ref[:, cs] = s_i

    ys = []
    for cb in range(SSM_NCB):
        ss = slice(cb * SSM_SB, (cb + 1) * SSM_SB)
        ys.append(jnp.dot(bur_ref[:, ss].astype(BF16), cmr_ref[cb], preferred_element_type=F32)
                  + jnp.dot(bui_ref[:, ss].astype(BF16), cmi_ref[cb], preferred_element_type=F32))
    y_tm = jnp.concatenate(ys, axis=1)
    y1 = y_tm.astype(BF16)
    rem = y_tm - y1.astype(F32)
    y2 = rem.astype(BF16)
    y3 = (rem - y2.astype(F32)).astype(BF16)
    y = (jnp.dot(to_batch_major, y1, preferred_element_type=F32)
         + jnp.dot(to_batch_major, y2, preferred_element_type=F32)
         + jnp.dot(to_batch_major, y3, preferred_element_type=F32))
    y = y + d_ref[...] * h
    y = jax.nn.gelu(y).astype(BF16)
    z = jnp.dot(y, wbf_ref[...], preferred_element_type=F32) + bglu_ref[...]
    t = z[:, :D] * jax.nn.sigmoid(z[:, D:])
    o_ref[...] = (x + _rms(t, g1_ref[...])).reshape(nb, tl, D)


def _ssm(x, g0, g1, s0r, s0i, ab_r, ab_i, bm_r, bm_i, cm_r, cm_i, d, wglu, bglu, tl):
    nb, seq, _ = x.shape
    rb = nb * tl
    rows = max(nb, SUBLANES)
    st = jax.ShapeDtypeStruct((rows, SSM_N), F32)
    row = pl.BlockSpec((nb, tl, D), lambda i: (0, i, 0))
    return pl.pallas_call(
        functools.partial(_ssm_kernel, nb=nb, rb=rb, lc=512),
        out_shape=(jax.ShapeDtypeStruct((nb, seq, D), F32), st, st),
        grid=(seq // tl,),
        in_specs=[row, _const_spec((1, D)), _const_spec((1, D)),
                  _const_spec((rows, SSM_N)), _const_spec((rows, SSM_N)),
                  _const_spec((1, SSM_N)), _const_spec((1, SSM_N)),
                  _const_spec((SSM_NCB, SSM_CB, SSM_SB)), _const_spec((SSM_NCB, SSM_CB, SSM_SB)),
                  _const_spec((SSM_NCB, SSM_SB, SSM_CB)), _const_spec((SSM_NCB, SSM_SB, SSM_CB)),
                  _const_spec((1, D)), _const_spec((D, 2 * D)), _const_spec((1, 2 * D))],
        out_specs=(row, _const_spec((rows, SSM_N)), _const_spec((rows, SSM_N))),
        scratch_shapes=[pltpu.VMEM((D, 2 * D), BF16), pltpu.VMEM((rb, SSM_N), F32), pltpu.VMEM((rb, SSM_N), F32)],
        compiler_params=_params(("arbitrary",), V7X_VMEM_LIMIT),
        name="ssm",
    )(x, g0, g1, s0r, s0i, ab_r, ab_i, bm_r, bm_i, cm_r, cm_i, d, wglu, bglu)


def _route_kernel(xp_ref, xs_ref, g_ref, wr_ref, h_ref, rk_ref, cw_ref, cnt_ref, carry_ref, *, n_prompt_chunks):
    i = pl.program_id(0)

    @pl.when(i == 0)
    def _():
        carry_ref[...] = jnp.zeros_like(carry_ref)

    x = jnp.where(i < n_prompt_chunks, xp_ref[...], xs_ref[...])
    h = _rms(x, g_ref[...])
    h_ref[...] = h.astype(BF16)
    lg = jnp.dot(h, wr_ref[...], preferred_element_type=F32, precision=lax.Precision.HIGHEST)
    lane = lax.broadcasted_iota(I32, lg.shape, 1)
    m1 = jnp.max(lg, axis=-1, keepdims=True)
    i1 = jnp.min(jnp.where(lg == m1, lane, N_EXP), axis=-1, keepdims=True)
    first = lane == i1
    lg2 = jnp.where(first, -jnp.inf, lg)
    m2 = jnp.max(lg2, axis=-1, keepdims=True)
    i2 = jnp.min(jnp.where(lg2 == m2, lane, N_EXP), axis=-1, keepdims=True)
    second = lane == i2
    e = jnp.exp(m2 - m1)
    den = 1.0 + e
    cw_ref[...] = jnp.where(first, 1.0 / den, 0.0) + jnp.where(second, e / den, 0.0)
    assigned = first | second
    a = assigned.astype(BF16)
    r = lax.broadcasted_iota(I32, (CHUNK, CHUNK), 0)
    c = lax.broadcasted_iota(I32, (CHUNK, CHUNK), 1)
    before = (c < r).astype(BF16)
    rank = jnp.dot(before, a, preferred_element_type=F32) + carry_ref[...]
    rk_ref[...] = jnp.where(assigned, rank.astype(I32), -1)
    carry_ref[...] += jnp.sum(assigned.astype(F32), axis=0, keepdims=True)
    cnt_ref[0] = carry_ref[...]


def _route(xp, xs, g, wr):
    npc = xp.shape[0] // CHUNK
    nch = npc + xs.shape[0] // CHUNK
    t = nch * CHUNK
    tok8 = pl.BlockSpec((CHUNK, N_EXP), lambda i: (i, 0))
    return pl.pallas_call(
        functools.partial(_route_kernel, n_prompt_chunks=npc),
        out_shape=(jax.ShapeDtypeStruct((t, D), BF16), jax.ShapeDtypeStruct((t, N_EXP), I32),
                   jax.ShapeDtypeStruct((t, N_EXP), F32), jax.ShapeDtypeStruct((nch, 1, N_EXP), F32)),
        grid=(nch,),
        in_specs=[pl.BlockSpec((CHUNK, D), lambda i: (jnp.minimum(i, npc - 1), 0)),
                  pl.BlockSpec((CHUNK, D), lambda i: (jnp.maximum(i - npc, 0), 0)),
                  _const_spec((1, D)), _const_spec((D, N_EXP))],
        out_specs=(pl.BlockSpec((CHUNK, D), lambda i: (i, 0)), tok8, tok8,
                   pl.BlockSpec((1, 1, N_EXP), lambda i: (i, 0, 0))),
        scratch_shapes=[pltpu.VMEM((1, N_EXP), F32)],
        compiler_params=_params(("arbitrary",)),
        name="moe_route",
    )(xp, xs, g, wr)


def _gather_kernel(be_ref, lo_ref, hi_ref, r0_ref, bef_ref, aft_ref, h_hbm, pos_hbm, o_ref, hbuf, pbuf, acc_ref, sem):
    d = pl.program_id(0)
    e = be_ref[d]
    lo = lo_ref[d]
    n = hi_ref[d] - lo + 1
    r0 = r0_ref[d]

    def copies(cidx, slot):
        return (pltpu.make_async_copy(h_hbm.at[pl.ds(cidx * CHUNK, CHUNK)], hbuf.at[slot], sem.at[0, slot]),
                pltpu.make_async_copy(pos_hbm.at[e, cidx], pbuf.at[slot], sem.at[1, slot]))

    def start(cidx, slot):
        for cp in copies(cidx, slot):
            cp.start()

    acc_ref[...] = jnp.zeros_like(acc_ref)

    @pl.when(n > 0)
    def _():
        start(lo, 0)

    row = d * BLK + lax.broadcasted_iota(I32, (GWIN, CHUNK), 0)

    def body(j, carry):
        slot = j % 2
        for cp in copies(lo + j, slot):
            cp.wait()

        @pl.when(j + 1 < n)
        def _():
            start(lo + j + 1, 1 - slot)

        first = jnp.clip(bef_ref[(lo + j) * N_EXP + e] - r0, 0, BLK)
        last = jnp.clip(aft_ref[(lo + j) * N_EXP + e] - r0, 0, BLK)

        def window(w, c2):
            w0 = pl.multiple_of(w * GWIN, GWIN)
            onehot = (pbuf[slot] == row + w0).astype(BF16)
            acc_ref[pl.ds(w0, GWIN), :] += jnp.dot(onehot, hbuf[slot], preferred_element_type=F32)
            return c2

        lax.fori_loop(first // GWIN, (last + GWIN - 1) // GWIN, window, 0)
        return carry

    lax.fori_loop(0, n, body, 0)
    o_ref[...] = acc_ref[...].astype(BF16)


def _gather(h, pos_t, blk_e, c_lo, c_hi, r0, before, after, nblk):
    return pl.pallas_call(
        _gather_kernel,
        out_shape=jax.ShapeDtypeStruct((nblk * BLK, D), BF16),
        grid_spec=pltpu.PrefetchScalarGridSpec(
            num_scalar_prefetch=6,
            grid=(nblk,),
            in_specs=[pl.BlockSpec(memory_space=pl.ANY), pl.BlockSpec(memory_space=pl.ANY)],
            out_specs=pl.BlockSpec((BLK, D), lambda d, *_: (d, 0)),
            scratch_shapes=[pltpu.VMEM((2, CHUNK, D), BF16), pltpu.VMEM((2, 1, CHUNK), I32),
                            pltpu.VMEM((BLK, D), F32), pltpu.SemaphoreType.DMA((2, 2))]),
        compiler_params=_params(("arbitrary",)),
        name="moe_gather",
    )(blk_e, c_lo, c_hi, r0, before, after, h, pos_t)


def _expert_up_kernel(be_ref, nv_ref, x_ref, wg_ref, wu_ref, o_ref, wgbf_ref, wubf_ref):
    d = pl.program_id(1)
    changed = (d == 0) | (be_ref[d] != be_ref[jnp.maximum(d - 1, 0)])

    @pl.when(changed)
    def _():
        wgbf_ref[...] = wg_ref[0].astype(BF16)
        wubf_ref[...] = wu_ref[0].astype(BF16)

    @pl.when(d < nv_ref[0])
    def _():
        x = x_ref[...]
        a = jnp.dot(x, wgbf_ref[...], preferred_element_type=F32)
        b = jnp.dot(x, wubf_ref[...], preferred_element_type=F32)
        o_ref[...] = (a * jax.nn.sigmoid(a) * b).astype(BF16)

    @pl.when(d >= nv_ref[0])
    def _():
        o_ref[...] = jnp.zeros_like(o_ref)


def _expert_up(xs, blk_e, nvalid, wg, wu, tf):
    p = xs.shape[0]
    ff = wg.shape[2]
    return pl.pallas_call(
        _expert_up_kernel,
        out_shape=jax.ShapeDtypeStruct((p, ff), BF16),
        grid_spec=pltpu.PrefetchScalarGridSpec(
            num_scalar_prefetch=2,
            grid=(ff // tf, p // BLK),
            in_specs=[pl.BlockSpec((BLK, D), lambda f, d, be, nv: (d, 0)),
                      pl.BlockSpec((1, D, tf), lambda f, d, be, nv: (be[d], 0, f)),
                      pl.BlockSpec((1, D, tf), lambda f, d, be, nv: (be[d], 0, f))],
            out_specs=pl.BlockSpec((BLK, tf), lambda f, d, be, nv: (d, f)),
            scratch_shapes=[pltpu.VMEM((D, tf), BF16), pltpu.VMEM((D, tf), BF16)]),
        compiler_params=_params(("arbitrary", "arbitrary"), V7X_VMEM_LIMIT),
        name="moe_up",
    )(blk_e, nvalid, xs, wg, wu)


def _expert_down_kernel(be_ref, nv_ref, h_ref, wd_ref, o_ref, wdbf_ref):
    d = pl.program_id(1)
    changed = (d == 0) | (be_ref[d] != be_ref[jnp.maximum(d - 1, 0)])

    @pl.when(changed)
    def _():
        wdbf_ref[...] = wd_ref[0].astype(BF16)

    @pl.when(d < nv_ref[0])
    def _():
        o_ref[...] = jnp.dot(h_ref[...], wdbf_ref[...], preferred_element_type=F32).astype(BF16)

    @pl.when(d >= nv_ref[0])
    def _():
        o_ref[...] = jnp.zeros_like(o_ref)


def _expert_down(hh, blk_e, nvalid, wd, tn):
    p, ff = hh.shape
    return pl.pallas_call(
        _expert_down_kernel,
        out_shape=jax.ShapeDtypeStruct((p, D), BF16),
        grid_spec=pltpu.PrefetchScalarGridSpec(
            num_scalar_prefetch=2,
            grid=(D // tn, p // BLK),
            in_specs=[pl.BlockSpec((BLK, ff), lambda n, d, be, nv: (d, 0)),
                      pl.BlockSpec((1, ff, tn), lambda n, d, be, nv: (be[d], 0, n))],
            out_specs=pl.BlockSpec((BLK, tn), lambda n, d, be, nv: (d, n)),
            scratch_shapes=[pltpu.VMEM((ff, tn), BF16)]),
        compiler_params=_params(("arbitrary", "arbitrary"), V7X_VMEM_LIMIT),
        name="moe_down",
    )(blk_e, nvalid, hh, wd)


def _combine_kernel(rs_ref, rc_ref, xp_ref, xs_ref, pos_ref, cw_ref, g_ref, y_hbm, op_ref, os_ref, wbuf, acc_ref, sem,
                    *, n_prompt_chunks):
    i = pl.program_id(0)
    nch = pl.num_programs(0)
    half = WIN // 2

    def window_start(chunk, e):
        return pl.multiple_of(rs_ref[chunk * N_EXP + e] // half * half, half)

    def copy(chunk, e, slot):
        return pltpu.make_async_copy(y_hbm.at[pl.ds(window_start(chunk, e), WIN)], wbuf.at[slot, e],
                                     sem.at[slot, e])

    @pl.when(i == 0)
    def _():
        for e in range(N_EXP):
            copy(0, e, 0).start()

    slot = i % 2

    @pl.when(i + 1 < nch)
    def _():
        for e in range(N_EXP):
            copy(i + 1, e, 1 - slot).start()

    col = lax.broadcasted_iota(I32, (CHUNK, half), 1)
    acc_ref[...] = jnp.zeros_like(acc_ref)
    for e in range(N_EXP):
        copy(i, e, slot).wait()
        ws = window_start(i, e)
        run_end = rs_ref[i * N_EXP + e] + rc_ref[i * N_EXP + e]

        def add_half(lo, e=e, ws=ws):
            rel = pos_ref[:, e:e + 1] - (ws + lo)
            onehot = (rel == col).astype(BF16)
            rows = jnp.dot(onehot, wbuf[slot, e, pl.ds(lo, half), :], preferred_element_type=F32)
            acc_ref[...] += cw_ref[:, e:e + 1] * rows

        @pl.when(rc_ref[i * N_EXP + e] > 0)
        def _():
            add_half(0)

        @pl.when(run_end > ws + half)
        def _():
            add_half(half)

    x = jnp.where(i < n_prompt_chunks, xp_ref[...], xs_ref[...])
    out = x + _rms(acc_ref[...], g_ref[...])

    @pl.when(i < n_prompt_chunks)
    def _():
        op_ref[...] = out

    @pl.when(i >= n_prompt_chunks)
    def _():
        os_ref[...] = out


def _combine(run_start, run_count, xp, xs, pos, cw, g, ys):
    npc = xp.shape[0] // CHUNK
    nsc = xs.shape[0] // CHUNK
    nch = npc + nsc
    tok8 = pl.BlockSpec((CHUNK, N_EXP), lambda i, *_: (i, 0))
    pspec = pl.BlockSpec((CHUNK, D), lambda i, *_: (jnp.minimum(i, npc - 1), 0))
    sspec = pl.BlockSpec((CHUNK, D), lambda i, *_: (jnp.maximum(i - npc, 0), 0))
    return pl.pallas_call(
        functools.partial(_combine_kernel, n_prompt_chunks=npc),
        out_shape=(jax.ShapeDtypeStruct(xp.shape, F32), jax.ShapeDtypeStruct(xs.shape, F32)),
        grid_spec=pltpu.PrefetchScalarGridSpec(
            num_scalar_prefetch=2,
            grid=(nch,),
            in_specs=[pspec, sspec, tok8, tok8, pl.BlockSpec((1, D), lambda i, *_: (0, 0)),
                      pl.BlockSpec(memory_space=pl.ANY)],
            out_specs=(pspec, sspec),
            scratch_shapes=[pltpu.VMEM((2, N_EXP, WIN, D), BF16), pltpu.VMEM((CHUNK, D), F32),
                            pltpu.SemaphoreType.DMA((2, N_EXP))]),
        compiler_params=_params(("arbitrary",), V7X_VMEM_LIMIT),
        name="moe_combine",
    )(run_start, run_count, xp, xs, pos, cw, g, ys)


def _moe(xp, xs, g4, g5, w_router, wg, wu, wd):
    npc = xp.shape[0] // CHUNK
    nch = npc + xs.shape[0] // CHUNK
    t = nch * CHUNK
    h, rk, cw, cnt = _route(xp, xs, g4, w_router)

    after = cnt.reshape(nch, N_EXP).astype(I32)
    before = jnp.concatenate([jnp.zeros((1, N_EXP), I32), after[:-1]], axis=0)
    counts = after[-1]
    gsz = (counts + BLK - 1) // BLK * BLK
    gend = jnp.cumsum(gsz)
    gstart = gend - gsz
    nblk = (2 * t + N_EXP * (BLK - 1)) // BLK + 1 + WIN // BLK
    bstart = jnp.arange(nblk, dtype=I32) * BLK
    blk_e = jnp.minimum(jnp.sum(gend[None, :] <= bstart[:, None], axis=1), N_EXP - 1).astype(I32)
    valid = bstart < gend[-1]
    r0 = bstart - gstart[blk_e]
    aft_e = after[:, blk_e]
    bef_e = before[:, blk_e]
    c_lo = jnp.where(valid, jnp.sum(aft_e <= r0[None, :], axis=0), 0).astype(I32)
    c_hi = jnp.where(valid, jnp.sum(bef_e < (r0 + BLK)[None, :], axis=0) - 1, -1).astype(I32)
    c_lo = jnp.minimum(c_lo, nch - 1)
    pos = jnp.where(rk >= 0, rk + gstart[None, :], -1).astype(I32)
    pos_t = pos.T.reshape(N_EXP, nch, 1, CHUNK)
    run_start = (gstart[None, :] + before).astype(I32).reshape(-1)
    run_count = (after - before).reshape(-1)
    nvalid = (gend[-1:] // BLK).astype(I32)

    xsort = _gather(h, pos_t, blk_e, c_lo, c_hi, r0.astype(I32), before.reshape(-1), after.reshape(-1), nblk)
    hh = _expert_up(xsort, blk_e, nvalid, wg, wu, tf=896)
    ysort = _expert_down(hh, blk_e, nvalid, wd, tn=512)
    return _combine(run_start, run_count, xp, xs, pos, cw, g5, ysort)


def kernel(x_prompt, x_sample, cache_conv, cache_mem_k, cache_mem_v, state_ssm_re, state_ssm_im, mem_prompt, norm_g, mem_norm_g, w_xq, w_xk, w_xv, w_xo, conv_w_pw1, conv_b_pw1, conv_w_dw, conv_b_dw, conv_ln_g, conv_ln_b, conv_w_pw2, conv_b_pw2, ssm_a_re, ssm_a_im, ssm_log_dt, ssm_b_re, ssm_b_im, ssm_c_re, ssm_c_im, ssm_d, ssm_w_glu, ssm_b_glu, ffn_w_gate, ffn_w_up, ffn_w_down, moe_w_router, moe_w_gate, moe_w_up, moe_w_down):
    nbp, seqp, _ = x_prompt.shape
    nbs, seqs, _ = x_sample.shape
    tp = nbp * seqp
    ts = nbs * seqs
    row = lambda a: a.reshape(1, -1)
    g = lambda i, k: norm_g[i, k].reshape(1, D)

    mem2d = mem_prompt.reshape(nbp * N_MEM, D)
    kv = [_mem_kv(mem2d, row(mem_norm_g[i]), w_xk[i], w_xv[i]) for i in range(2)]
    p_mem_k = jnp.stack([k for k, _ in kv]).reshape(2, nbp, N_MEM, N_HEADS, HEAD_DIM)
    p_mem_v = jnp.stack([v for _, v in kv]).reshape(2, nbp, N_MEM, N_HEADS, HEAD_DIM)

    conv_args = (conv_w_dw[0], row(conv_b_dw[0]), row(conv_ln_g[0]), row(conv_ln_b[0]),
                 conv_w_pw2[0], row(conv_b_pw2[0]), g(0, 1))
    xp = x_prompt.reshape(tp, D)
    xs = x_sample.reshape(ts, D)
    up = _conv_pw1(xp, g(0, 0), conv_w_pw1[0], row(conv_b_pw1[0]), tm=512).reshape(nbp, seqp, D)
    us = _conv_pw1(xs, g(0, 0), conv_w_pw1[0], row(conv_b_pw1[0]), tm=ts).reshape(nbs, seqs, D)
    hist_p = jnp.zeros((nbp, HIST, D), F32)
    hist_s = jnp.pad(cache_conv[0], ((0, 0), (HIST - CONV_W + 1, 0), (0, 0)))
    xp = _conv_dw_pw2(up, x_prompt, hist_p, *conv_args, tl=256).reshape(tp, D)
    xs = _conv_dw_pw2(us, x_sample, hist_s, *conv_args, tl=seqs).reshape(ts, D)
    p_conv = up[:, seqp - (CONV_W - 1):][None]
    s_conv = jnp.concatenate([cache_conv[0], us], axis=1)[:, -(CONV_W - 1):][None]

    tma = 256
    xp = _attn(xp, lambda b, l: (b * (seqp // tma) + l, 0), kv[0][0].reshape(nbp, N_MEM, D),
               kv[0][1].reshape(nbp, N_MEM, D), w_xq[0], w_xo[0], g(0, 2), g(0, 3), nbp, seqp, tma)
    xs = _attn(xs, lambda b, l: (b, 0), cache_mem_k[0].reshape(nbs, N_MEM, D),
               cache_mem_v[0].reshape(nbs, N_MEM, D), w_xq[0], w_xo[0], g(0, 2), g(0, 3), nbs, seqs, seqs)

    ffn_w = (ffn_w_gate[0], ffn_w_up[0], ffn_w_down[0])
    xp = _ffn(xp, g(0, 4), g(0, 5), *ffn_w, tm=1024, tf=256, out_shape=(tp, D), out_index=lambda i, f: (i, 0))
    xs = _ffn(xs, g(0, 4), g(0, 5), *ffn_w, tm=ts, tf=256, out_shape=(ts, D), out_index=lambda i, f: (i, 0))

    ab_r, ab_i, bb_r, bb_i = _ssm_prep(ssm_a_re[0], ssm_a_im[0], ssm_log_dt[0], ssm_b_re[0], ssm_b_im[0])
    bm_r = _block_diag(bb_r.transpose(0, 2, 1)).astype(BF16)
    bm_i = _block_diag(bb_i.transpose(0, 2, 1)).astype(BF16)
    cm_r = _block_diag(ssm_c_re[0].transpose(0, 2, 1)).astype(BF16)
    cm_i = _block_diag(-ssm_c_im[0].transpose(0, 2, 1)).astype(BF16)
    ssm_args = (ab_r, ab_i, bm_r, bm_i, cm_r, cm_i, row(ssm_d[0]), ssm_w_glu[0], row(ssm_b_glu[0]))
    zero_state = jnp.zeros((SUBLANES, SSM_N), F32)
    xp, p_sr, p_si = _ssm(xp.reshape(nbp, seqp, D), g(1, 0), g(1, 1), zero_state, zero_state, *ssm_args, tl=64)
    xs, s_sr, s_si = _ssm(xs.reshape(nbs, seqs, D), g(1, 0), g(1, 1), state_ssm_re[0].reshape(nbs, SSM_N),
                          state_ssm_im[0].reshape(nbs, SSM_N), *ssm_args, tl=seqs)
    st = lambda a, n: a[:n].reshape(1, n, SSM_G, SSM_P)
    p_ssm_re, p_ssm_im = st(p_sr, nbp), st(p_si, nbp)
    s_ssm_re, s_ssm_im = st(s_sr, nbs), st(s_si, nbs)

    xp = _attn(xp.reshape(tp, D), lambda b, l: (b * (seqp // tma) + l, 0), kv[1][0].reshape(nbp, N_MEM, D),
               kv[1][1].reshape(nbp, N_MEM, D), w_xq[1], w_xo[1], g(1, 2), g(1, 3), nbp, seqp, tma)
    xs = _attn(xs.reshape(ts, D), lambda b, l: (b, 0), cache_mem_k[1].reshape(nbs, N_MEM, D),
               cache_mem_v[1].reshape(nbs, N_MEM, D), w_xq[1], w_xo[1], g(1, 2), g(1, 3), nbs, seqs, seqs)

    yp, ysm = _moe(xp, xs, g(1, 4), g(1, 5), moe_w_router[0], moe_w_gate[0], moe_w_up[0], moe_w_down[0])
    return (yp.reshape(nbp, seqp, D), ysm.reshape(nbs, seqs, D), p_conv, p_mem_k, p_mem_v,
            p_ssm_re, p_ssm_im, s_conv, s_ssm_re, s_ssm_im)
```

```python
import functools
import math

import jax
import jax.numpy as jnp
from jax import lax
from jax.experimental import pallas as pl
from jax.experimental.pallas import tpu as pltpu

F32 = jnp.float32
BF16 = jnp.bfloat16
I32 = jnp.int32

D = 1024
CONV_W = 31
HIST = 32
N_MEM = 256
N_HEADS = 4
HEAD_DIM = D // N_HEADS
SSM_G = 64
SSM_C = 16
SSM_P = 64
SSM_N = SSM_G * SSM_P
SSM_CB = 256
SSM_NCB = D // SSM_CB
SSM_SB = SSM_CB // SSM_C * SSM_P
N_EXP = 8
EPS = 1e-6

V7X_VMEM_LIMIT = 56 * 1024 * 1024
SUBLANES = 8

CHUNK = 256
BLK = 512
WIN = 512
GWIN = 128
GSLOTS = 4
CWIN = 256
BF16_ROWS = 16


def _params(sem, vmem=None):
    return pltpu.CompilerParams(dimension_semantics=sem, vmem_limit_bytes=vmem)


def _rms(x, g):
    return x * lax.rsqrt(jnp.mean(x * x, axis=-1, keepdims=True) + EPS) * g


def _const_spec(shape):
    nd = len(shape)
    return pl.BlockSpec(shape, lambda *_: (0,) * nd)


def _pw1_kernel(x_ref, g_ref, w_ref, b_ref, u_ref, wbf_ref):
    @pl.when(pl.program_id(0) == 0)
    def _():
        wbf_ref[...] = w_ref[...].astype(BF16)

    h = _rms(x_ref[...], g_ref[...]).astype(BF16)
    z = jnp.dot(h, wbf_ref[...], preferred_element_type=F32) + b_ref[...]
    u_ref[...] = z[:, :D] * jax.nn.sigmoid(z[:, D:])


def _conv_pw1(x, g, w, b, tm):
    t = x.shape[0]
    return pl.pallas_call(
        _pw1_kernel,
        out_shape=jax.ShapeDtypeStruct((t, D), F32),
        grid=(t // tm,),
        in_specs=[pl.BlockSpec((tm, D), lambda i: (i, 0)),
                  _const_spec((1, D)), _const_spec((D, 2 * D)), _const_spec((1, 2 * D))],
        out_specs=pl.BlockSpec((tm, D), lambda i: (i, 0)),
        scratch_shapes=[pltpu.VMEM((D, 2 * D), BF16)],
        compiler_params=_params(("arbitrary",), V7X_VMEM_LIMIT),
        name="conv_pw1",
    )(x, g, w, b)


def _conv2_kernel(u_ref, x_ref, hist_ref, wdw_ref, bdw_ref, lng_ref, lnb_ref, w2_ref, b2_ref,
                  g_ref, o_ref, ext_ref, sh_ref, conv_ref, w2bf_ref, *, tl, rt):
    bi = pl.program_id(0)
    li = pl.program_id(1)

    @pl.when((bi == 0) & (li == 0))
    def _():
        w2bf_ref[...] = w2_ref[...].astype(BF16)

    @pl.when(li == 0)
    def _():
        ext_ref[0:HIST, :] = hist_ref[0]

    @pl.when(li > 0)
    def _():
        ext_ref[0:HIST, :] = ext_ref[tl:tl + HIST, :]

    ext_ref[HIST:HIST + tl, :] = u_ref[0]
    for s in range(1, SUBLANES):
        sh_ref[s - 1] = ext_ref[pl.ds(s, tl + HIST - SUBLANES), :]

    def rows(i, carry):
        r0 = pl.multiple_of(i * rt, rt)
        acc = jnp.zeros((rt, D), F32) + bdw_ref[...]
        for k in range(CONV_W):
            off = HIST - CONV_W + 1 + k
            s = off % SUBLANES
            base = off - s
            if s == 0:
                src = ext_ref[pl.ds(r0 + base, rt), :]
            else:
                src = sh_ref[s - 1, pl.ds(r0 + base, rt), :]
            acc = acc + wdw_ref[k:k + 1, :] * src
        conv_ref[pl.ds(r0, rt), :] = acc
        return carry

    lax.fori_loop(0, tl // rt, rows, 0)
    acc = conv_ref[...]
    mu = jnp.mean(acc, axis=-1, keepdims=True)
    xc = acc - mu
    var = jnp.mean(xc * xc, axis=-1, keepdims=True)
    y = xc * lax.rsqrt(var + EPS) * lng_ref[...] + lnb_ref[...]
    y = y * jax.nn.sigmoid(y)
    t = jnp.dot(y.astype(BF16), w2bf_ref[...], preferred_element_type=F32) + b2_ref[...]
    o_ref[0] = x_ref[0] + _rms(t, g_ref[...])


def _conv_dw_pw2(u, x, hist, wdw, bdw, lng, lnb, w2, b2, g, tl):
    nb, seq, _ = u.shape
    tok = pl.BlockSpec((1, tl, D), lambda b, l: (b, l, 0))
    return pl.pallas_call(
        functools.partial(_conv2_kernel, tl=tl, rt=min(tl, 32)),
        out_shape=jax.ShapeDtypeStruct((nb, seq, D), F32),
        grid=(nb, seq // tl),
        in_specs=[tok, tok, pl.BlockSpec((1, HIST, D), lambda b, l: (b, 0, 0)),
                  _const_spec((CONV_W, D)), _const_spec((1, D)), _const_spec((1, D)), _const_spec((1, D)),
                  _const_spec((D, D)), _const_spec((1, D)), _const_spec((1, D))],
        out_specs=tok,
        scratch_shapes=[pltpu.VMEM((tl + HIST, D), F32),
                        pltpu.VMEM((SUBLANES - 1, tl + HIST - SUBLANES, D), F32),
                        pltpu.VMEM((tl, D), F32), pltpu.VMEM((D, D), BF16)],
        compiler_params=_params(("arbitrary", "arbitrary"), V7X_VMEM_LIMIT),
        name="conv_dw_pw2",
    )(u, x, hist, wdw, bdw, lng, lnb, w2, b2, g)


def _memkv_kernel(m_ref, g_ref, wk_ref, wv_ref, k_ref, v_ref, wkbf_ref, wvbf_ref):
    @pl.when(pl.program_id(1) == 0)
    def _():
        wkbf_ref[...] = wk_ref[0].astype(BF16)
        wvbf_ref[...] = wv_ref[0].astype(BF16)

    m = _rms(m_ref[0], g_ref[0]).astype(BF16)
    k = jnp.dot(m, wkbf_ref[...], preferred_element_type=F32)
    v = jnp.dot(m, wvbf_ref[...], preferred_element_type=F32)
    for hd in range(N_HEADS):
        sl = slice(hd * HEAD_DIM, (hd + 1) * HEAD_DIM)
        k_ref[0, :, hd, :] = k[:, sl]
        v_ref[0, :, hd, :] = v[:, sl]


def _mem_kv(mem, g, wk, wv):
    nb = mem.shape[0]
    nl = wk.shape[0]
    kv = jax.ShapeDtypeStruct((nl * nb, N_MEM, N_HEADS, HEAD_DIM), F32)
    out = pl.BlockSpec((1, N_MEM, N_HEADS, HEAD_DIM), lambda l, b: (l * nb + b, 0, 0, 0))
    w = pl.BlockSpec((1, D, D), lambda l, b: (l, 0, 0))
    return pl.pallas_call(
        _memkv_kernel,
        out_shape=(kv, kv),
        grid=(nl, nb),
        in_specs=[pl.BlockSpec((1, N_MEM, D), lambda l, b: (b, 0, 0)),
                  pl.BlockSpec((1, 1, D), lambda l, b: (l, 0, 0)), w, w],
        out_specs=(out, out),
        scratch_shapes=[pltpu.VMEM((D, D), BF16), pltpu.VMEM((D, D), BF16)],
        compiler_params=_params(("arbitrary", "arbitrary"), V7X_VMEM_LIMIT),
        name="mem_kv",
    )(mem, g, wk, wv)


def _attn_kernel(x_ref, k_ref, v_ref, wq_ref, wo_ref, g2_ref, g3_ref, o_ref, wqbf_ref, wobf_ref, kb_ref, vb_ref):
    @pl.when((pl.program_id(0) == 0) & (pl.program_id(1) == 0))
    def _():
        wqbf_ref[...] = wq_ref[...].astype(BF16)
        wobf_ref[...] = wo_ref[...].astype(BF16)

    @pl.when(pl.program_id(1) == 0)
    def _():
        for hd in range(N_HEADS):
            sl = slice(hd * HEAD_DIM, (hd + 1) * HEAD_DIM)
            kb_ref[:, sl] = k_ref[0, :, hd, :].astype(BF16)
            vb_ref[:, sl] = v_ref[0, :, hd, :].astype(BF16)

    x = x_ref[...]
    h = _rms(x, g2_ref[...]).astype(BF16)
    q = (jnp.dot(h, wqbf_ref[...], preferred_element_type=F32) * (HEAD_DIM ** -0.5)).astype(BF16)
    heads = []
    for hd in range(N_HEADS):
        sl = slice(hd * HEAD_DIM, (hd + 1) * HEAD_DIM)
        s = lax.dot_general(q[:, sl], kb_ref[:, sl], (((1,), (1,)), ((), ())), preferred_element_type=F32)
        p = jnp.exp(s - jnp.max(s, axis=-1, keepdims=True))
        p = p / jnp.sum(p, axis=-1, keepdims=True)
        heads.append(jnp.dot(p.astype(BF16), vb_ref[:, sl], preferred_element_type=F32))
    o = jnp.concatenate(heads, axis=1).astype(BF16)
    t = jnp.dot(o, wobf_ref[...], preferred_element_type=F32)
    o_ref[...] = x + _rms(t, g3_ref[...])


def _attn(x, k, v, kv_base, wq, wo, g2, g3, nb, seq, tm):
    tok = pl.BlockSpec((tm, D), lambda b, l: (b * (seq // tm) + l, 0))
    kv = pl.BlockSpec((1, N_MEM, N_HEADS, HEAD_DIM), lambda b, l: (kv_base + b, 0, 0, 0))
    return pl.pallas_call(
        _attn_kernel,
        out_shape=jax.ShapeDtypeStruct((nb * seq, D), F32),
        grid=(nb, seq // tm),
        in_specs=[tok, kv, kv, _const_spec((D, D)), _const_spec((D, D)), _const_spec((1, D)), _const_spec((1, D))],
        out_specs=tok,
        scratch_shapes=[pltpu.VMEM((D, D), BF16), pltpu.VMEM((D, D), BF16),
                        pltpu.VMEM((N_MEM, D), BF16), pltpu.VMEM((N_MEM, D), BF16)],
        compiler_params=_params(("arbitrary", "arbitrary"), V7X_VMEM_LIMIT),
        name="mem_attn",
    )(x, k, v, wq, wo, g2, g3)


def _ffn_kernel(x_ref, g4_ref, g5_ref, wg_ref, wu_ref, wd_ref, o_ref, h_ref, acc_ref):
    f = pl.program_id(1)

    @pl.when(f == 0)
    def _():
        h_ref[...] = _rms(x_ref[...], g4_ref[...]).astype(BF16)
        acc_ref[...] = jnp.zeros_like(acc_ref)

    h = h_ref[...]
    a = jnp.dot(h, wg_ref[...].astype(BF16), preferred_element_type=F32)
    b = jnp.dot(h, wu_ref[...].astype(BF16), preferred_element_type=F32)
    hh = (a * jax.nn.sigmoid(a) * b).astype(BF16)
    acc_ref[...] += jnp.dot(hh, wd_ref[...].astype(BF16), preferred_element_type=F32)

    @pl.when(f == pl.num_programs(1) - 1)
    def _():
        o_ref[...] = x_ref[...] + _rms(acc_ref[...], g5_ref[...])


def _ffn(x, g4, g5, wg, wu, wd, tm, tf, out_shape, out_index):
    t = x.shape[0]
    ff = wg.shape[1]
    return pl.pallas_call(
        _ffn_kernel,
        out_shape=jax.ShapeDtypeStruct(out_shape, F32),
        grid=(t // tm, ff // tf),
        in_specs=[pl.BlockSpec((tm, D), lambda i, f: (i, 0)),
                  _const_spec((1, D)), _const_spec((1, D)),
                  pl.BlockSpec((D, tf), lambda i, f: (0, f)),
                  pl.BlockSpec((D, tf), lambda i, f: (0, f)),
                  pl.BlockSpec((tf, D), lambda i, f: (f, 0))],
        out_specs=pl.BlockSpec((tm, D), out_index),
        scratch_shapes=[pltpu.VMEM((tm, D), BF16), pltpu.VMEM((tm, D), F32)],
        compiler_params=_params(("arbitrary", "arbitrary"), V7X_VMEM_LIMIT),
        name="dense_ffn",
    )(x, g4, g5, wg, wu, wd)


def _ssm_prep_kernel(lr_ref, li_ref, ldt_ref, br_ref, bi_ref, abr_ref, abi_ref, bbr_ref, bbi_ref):
    dt = jnp.exp(ldt_ref[...])
    lr = lr_ref[...]
    li = li_ref[...]
    mag = jnp.exp(lr * dt)
    ab_r = mag * jnp.cos(li * dt)
    ab_i = mag * jnp.sin(li * dt)
    den = lr * lr + li * li
    nr = ab_r - 1.0
    k_r = (nr * lr + ab_i * li) / den
    k_i = (ab_i * lr - nr * li) / den
    br = br_ref[...]
    bi = bi_ref[...]
    abr_ref[...] = ab_r
    abi_ref[...] = ab_i
    bbr_ref[...] = k_r * br - k_i * bi
    bbi_ref[...] = k_r * bi + k_i * br


def _ssm_prep(a_re, a_im, log_dt, b_re, b_im):
    n = SSM_P * SSM_C
    rep = lambda a: jnp.repeat(a, SSM_C, axis=1)
    shp = jax.ShapeDtypeStruct((SSM_G, n), F32)
    abr, abi, bbr, bbi = pl.pallas_call(
        _ssm_prep_kernel,
        out_shape=(shp, shp, shp, shp),
        name="ssm_prep",
    )(rep(a_re), rep(a_im), log_dt.reshape(SSM_G, 1), b_re.reshape(SSM_G, n), b_im.reshape(SSM_G, n))
    pick = lambda a: a.reshape(SSM_G, SSM_P, SSM_C)[:, :, 0].reshape(1, SSM_N)
    return pick(abr), pick(abi), bbr.reshape(SSM_G, SSM_P, SSM_C), bbi.reshape(SSM_G, SSM_P, SSM_C)


def _block_diag(m):
    gpb = SSM_CB // SSM_C
    _, a, b = m.shape
    m = m.reshape(SSM_NCB, gpb, a, 1, b)
    eye = jnp.eye(gpb, dtype=m.dtype).reshape(1, gpb, 1, gpb, 1)
    return (m * eye).reshape(SSM_NCB, gpb * a, gpb * b)


def _ssm_kernel(x_ref, g0_ref, g1_ref, s0r_ref, s0i_ref, ar_ref, ai_ref, bmr_ref, bmi_ref, cmr_ref, cmi_ref,
                d_ref, wglu_ref, bglu_ref, o_ref, sr_ref, si_ref,
                wbf_ref, bur_ref, bui_ref, *, nb, rb, lc):
    @pl.when(pl.program_id(0) == 0)
    def _():
        wbf_ref[...] = wglu_ref[...].astype(BF16)
        sr_ref[...] = s0r_ref[...]
        si_ref[...] = s0i_ref[...]

    tl = rb // nb
    x = x_ref[...].reshape(rb, D)
    h = _rms(x, g0_ref[...])
    r = lax.broadcasted_iota(I32, (rb, rb), 0)
    c = lax.broadcasted_iota(I32, (rb, rb), 1)
    to_time_major = (c == (r & (nb - 1)) * tl + (r >> (nb.bit_length() - 1))).astype(BF16)
    to_batch_major = (c == (r & (tl - 1)) * nb + (r >> (tl.bit_length() - 1))).astype(BF16)
    hb = jnp.dot(to_time_major, h.astype(BF16), preferred_element_type=F32).astype(BF16)
    for cb in range(SSM_NCB):
        hs = hb[:, cb * SSM_CB:(cb + 1) * SSM_CB]
        bur_ref[:, cb * SSM_SB:(cb + 1) * SSM_SB] = jnp.dot(hs, bmr_ref[cb], preferred_element_type=F32)
        bui_ref[:, cb * SSM_SB:(cb + 1) * SSM_SB] = jnp.dot(hs, bmi_ref[cb], preferred_element_type=F32)

    rows = max(nb, SUBLANES)
    for c in range(SSM_N // lc):
        cs = slice(c * lc, (c + 1) * lc)
        a_r = jnp.broadcast_to(ar_ref[:, cs], (rows, lc))
        a_i = jnp.broadcast_to(ai_ref[:, cs], (rows, lc))

        if nb == 4:
            low = lax.broadcasted_iota(I32, (SUBLANES, lc), 0) < nb

            def step(j, carry):
                s_r, s_i = carry
                r0 = pl.multiple_of(j * SUBLANES, SUBLANES)
                u_r = bur_ref[pl.ds(r0, SUBLANES), cs]
                u_i = bui_ref[pl.ds(r0, SUBLANES), cs]
                x1r = a_r * s_r - a_i * s_i + u_r
                x1i = a_r * s_i + a_i * s_r + u_i
                p_r = pltpu.roll(x1r, nb, 0)
                p_i = pltpu.roll(x1i, nb, 0)
                x2r = a_r * p_r - a_i * p_i + u_r
                x2i = a_r * p_i + a_i * p_r + u_i
                bur_ref[pl.ds(r0, SUBLANES), cs] = jnp.where(low, x1r, x2r)
                bui_ref[pl.ds(r0, SUBLANES), cs] = jnp.where(low, x1i, x2i)
                return (jnp.where(low, pltpu.roll(x2r, nb, 0), x2r),
                        jnp.where(low, pltpu.roll(x2i, nb, 0), x2i))
        else:
            def step(j, carry):
                s_r, s_i = carry
                r0 = pl.multiple_of(j * nb, SUBLANES)
                u_r = bur_ref[pl.ds(r0, nb), cs]
                u_i = bui_ref[pl.ds(r0, nb), cs]
                n_r = a_r * s_r - a_i * s_i + u_r
                n_i = a_r * s_i + a_i * s_r + u_i
                bur_ref[pl.ds(r0, nb), cs] = n_r
                bui_ref[pl.ds(r0, nb), cs] = n_i
                return n_r, n_i

        s_r, s_i = lax.fori_loop(0, rb // rows, step, (sr_ref[:, cs], si_ref[:, cs]))
        sr_ref[:, cs] = s_r
        si_ref[:, cs] = s_i

    ys = []
    for cb in range(SSM_NCB):
        ss = slice(cb * SSM_SB, (cb + 1) * SSM_SB)
        ys.append(jnp.dot(bur_ref[:, ss].astype(BF16), cmr_ref[cb], preferred_element_type=F32)
                  + jnp.dot(bui_ref[:, ss].astype(BF16), cmi_ref[cb], preferred_element_type=F32))
    y_tm = jnp.concatenate(ys, axis=1)
    y1 = y_tm.astype(BF16)
    rem = y_tm - y1.astype(F32)
    y2 = rem.astype(BF16)
    y3 = (rem - y2.astype(F32)).astype(BF16)
    y = (jnp.dot(to_batch_major, y1, preferred_element_type=F32)
         + jnp.dot(to_batch_major, y2, preferred_element_type=F32)
         + jnp.dot(to_batch_major, y3, preferred_element_type=F32))
    y = y + d_ref[...] * h
    y = jax.nn.gelu(y).astype(BF16)
    z = jnp.dot(y, wbf_ref[...], preferred_element_type=F32) + bglu_ref[...]
    t = z[:, :D] * jax.nn.sigmoid(z[:, D:])
    o_ref[...] = (x + _rms(t, g1_ref[...])).reshape(nb, tl, D)


def _ssm(x, g0, g1, s0r, s0i, ab_r, ab_i, bm_r, bm_i, cm_r, cm_i, d, wglu, bglu, tl):
    nb, seq, _ = x.shape
    rb = nb * tl
    rows = max(nb, SUBLANES)
    st = jax.ShapeDtypeStruct((rows, SSM_N), F32)
    row = pl.BlockSpec((nb, tl, D), lambda i: (0, i, 0))
    return pl.pallas_call(
        functools.partial(_ssm_kernel, nb=nb, rb=rb, lc=512),
        out_shape=(jax.ShapeDtypeStruct((nb, seq, D), F32), st, st),
        grid=(seq // tl,),
        in_specs=[row, _const_spec((1, D)), _const_spec((1, D)),
                  _const_spec((rows, SSM_N)), _const_spec((rows, SSM_N)),
                  _const_spec((1, SSM_N)), _const_spec((1, SSM_N)),
                  _const_spec((SSM_NCB, SSM_CB, SSM_SB)), _const_spec((SSM_NCB, SSM_CB, SSM_SB)),
                  _const_spec((SSM_NCB, SSM_SB, SSM_CB)), _const_spec((SSM_NCB, SSM_SB, SSM_CB)),
                  _const_spec((1, D)), _const_spec((D, 2 * D)), _const_spec((1, 2 * D))],
        out_specs=(row, _const_spec((rows, SSM_N)), _const_spec((rows, SSM_N))),
        scratch_shapes=[pltpu.VMEM((D, 2 * D), BF16), pltpu.VMEM((rb, SSM_N), F32), pltpu.VMEM((rb, SSM_N), F32)],
        compiler_params=_params(("arbitrary",), V7X_VMEM_LIMIT),
        name="ssm",
    )(x, g0, g1, s0r, s0i, ab_r, ab_i, bm_r, bm_i, cm_r, cm_i, d, wglu, bglu)


def _route_kernel(xp_ref, xs_ref, g_ref, wr_ref, h_ref, rk_ref, cw_ref, cnt_ref, carry_ref, *, n_prompt_chunks):
    i = pl.program_id(0)

    @pl.when(i == 0)
    def _():
        carry_ref[...] = jnp.zeros_like(carry_ref)

    x = jnp.where(i < n_prompt_chunks, xp_ref[...], xs_ref[...])
    h = _rms(x, g_ref[...])
    h_ref[...] = h.astype(BF16)
    lg = jnp.dot(h, wr_ref[...], preferred_element_type=F32, precision=lax.Precision.HIGHEST)
    lane = lax.broadcasted_iota(I32, lg.shape, 1)
    m1 = jnp.max(lg, axis=-1, keepdims=True)
    i1 = jnp.min(jnp.where(lg == m1, lane, N_EXP), axis=-1, keepdims=True)
    first = lane == i1
    lg2 = jnp.where(first, -jnp.inf, lg)
    m2 = jnp.max(lg2, axis=-1, keepdims=True)
    i2 = jnp.min(jnp.where(lg2 == m2, lane, N_EXP), axis=-1, keepdims=True)
    second = lane == i2
    e = jnp.exp(m2 - m1)
    den = 1.0 + e
    cw_ref[...] = jnp.where(first, 1.0 / den, 0.0) + jnp.where(second, e / den, 0.0)
    assigned = first | second
    a = assigned.astype(BF16)
    r = lax.broadcasted_iota(I32, (CHUNK, CHUNK), 0)
    c = lax.broadcasted_iota(I32, (CHUNK, CHUNK), 1)
    before = (c < r).astype(BF16)
    rank = jnp.dot(before, a, preferred_element_type=F32) + carry_ref[...]
    rk_ref[...] = jnp.where(assigned, rank.astype(I32), -1)
    carry_ref[...] += jnp.sum(assigned.astype(F32), axis=0, keepdims=True)
    cnt_ref[0] = carry_ref[...]


def _route(xp, xs, g, wr):
    npc = xp.shape[0] // CHUNK
    nch = npc + xs.shape[0] // CHUNK
    t = nch * CHUNK
    tok8 = pl.BlockSpec((CHUNK, N_EXP), lambda i: (i, 0))
    return pl.pallas_call(
        functools.partial(_route_kernel, n_prompt_chunks=npc),
        out_shape=(jax.ShapeDtypeStruct((t, D), BF16), jax.ShapeDtypeStruct((t, N_EXP), I32),
                   jax.ShapeDtypeStruct((t, N_EXP), F32), jax.ShapeDtypeStruct((nch, 1, N_EXP), F32)),
        grid=(nch,),
        in_specs=[pl.BlockSpec((CHUNK, D), lambda i: (jnp.minimum(i, npc - 1), 0)),
                  pl.BlockSpec((CHUNK, D), lambda i: (jnp.maximum(i - npc, 0), 0)),
                  _const_spec((1, D)), _const_spec((D, N_EXP))],
        out_specs=(pl.BlockSpec((CHUNK, D), lambda i: (i, 0)), tok8, tok8,
                   pl.BlockSpec((1, 1, N_EXP), lambda i: (i, 0, 0))),
        scratch_shapes=[pltpu.VMEM((1, N_EXP), F32)],
        compiler_params=_params(("arbitrary",)),
        name="moe_route",
    )(xp, xs, g, wr)


def _gather_kernel(be_ref, pb_ref, pn_ref, r0_ref, bef_ref, aft_ref, pc_ref, pe_ref, pt_ref,
                   h_hbm, pos_hbm, cw_hbm, o_ref, gs_ref, hbuf, pbuf, cbuf, acc_ref, gacc_ref, sem):
    d = pl.program_id(0)
    e = be_ref[d]
    base = pb_ref[d]
    r0 = r0_ref[d]
    total = pt_ref[0]

    def copies(idx):
        slot = idx % GSLOTS
        c = pc_ref[idx]
        ee = pe_ref[idx]
        return (pltpu.make_async_copy(h_hbm.at[pl.ds(c * CHUNK, CHUNK)], hbuf.at[slot], sem.at[0, slot]),
                pltpu.make_async_copy(pos_hbm.at[ee, c], pbuf.at[slot], sem.at[1, slot]),
                pltpu.make_async_copy(cw_hbm.at[ee, c], cbuf.at[slot], sem.at[2, slot]))

    def start(idx):
        @pl.when(idx < total)
        def _():
            for cp in copies(idx):
                cp.start()

    @pl.when(d == 0)
    def _():
        for k in range(GSLOTS - 1):
            start(k)

    acc_ref[...] = jnp.zeros_like(acc_ref)
    gacc_ref[...] = jnp.zeros_like(gacc_ref)
    row = d * BLK + lax.broadcasted_iota(I32, (GWIN, CHUNK), 0)

    def body(j, carry):
        idx = base + j
        slot = idx % GSLOTS
        for cp in copies(idx):
            cp.wait()
        start(idx + GSLOTS - 1)

        c = pc_ref[idx]
        first = jnp.clip(bef_ref[c * N_EXP + e] - r0, 0, BLK)
        last = jnp.clip(aft_ref[c * N_EXP + e] - r0, 0, BLK)

        def window(w, c2):
            w0 = pl.multiple_of(w * GWIN, GWIN)
            hit = pbuf[slot] == row + w0
            acc_ref[pl.ds(w0, GWIN), :] += jnp.dot(hit.astype(BF16), hbuf[slot], preferred_element_type=F32)
            gacc_ref[pl.ds(w0, GWIN), :] += jnp.sum(jnp.where(hit, cbuf[slot], 0.0), axis=1, keepdims=True)
            return c2

        lax.fori_loop(first // GWIN, (last + GWIN - 1) // GWIN, window, 0)
        return carry

    lax.fori_loop(0, pn_ref[d], body, 0)
    o_ref[...] = acc_ref[...].astype(BF16)
    gs_ref[...] = gacc_ref[...]


def _gather(h, pos_t, cw_t, blk_e, pbase, pcount, r0, before, after, pair_c, pair_e, ptotal, nblk):
    hbm = pl.BlockSpec(memory_space=pl.ANY)
    return pl.pallas_call(
        _gather_kernel,
        out_shape=(jax.ShapeDtypeStruct((nblk * BLK, D), BF16), jax.ShapeDtypeStruct((nblk * BLK, 1), F32)),
        grid_spec=pltpu.PrefetchScalarGridSpec(
            num_scalar_prefetch=9,
            grid=(nblk,),
            in_specs=[hbm, hbm, hbm],
            out_specs=(pl.BlockSpec((BLK, D), lambda d, *_: (d, 0)), pl.BlockSpec((BLK, 1), lambda d, *_: (d, 0))),
            scratch_shapes=[pltpu.VMEM((GSLOTS, CHUNK, D), BF16), pltpu.VMEM((GSLOTS, 1, CHUNK), I32),
                            pltpu.VMEM((GSLOTS, 1, CHUNK), F32), pltpu.VMEM((BLK, D), F32),
                            pltpu.VMEM((BLK, 1), F32), pltpu.SemaphoreType.DMA((3, GSLOTS))]),
        compiler_params=_params(("arbitrary",)),
        name="moe_gather",
    )(blk_e, pbase, pcount, r0, before, after, pair_c, pair_e, ptotal, h, pos_t, cw_t)


def _expert_up_kernel(be_ref, nv_ref, x_ref, wg_ref, wu_ref, o_ref, wgbf_ref, wubf_ref):
    d = pl.program_id(1)
    changed = (d == 0) | (be_ref[d] != be_ref[jnp.maximum(d - 1, 0)])

    @pl.when(changed)
    def _():
        wgbf_ref[...] = wg_ref[0].astype(BF16)
        wubf_ref[...] = wu_ref[0].astype(BF16)

    @pl.when(d < nv_ref[0])
    def _():
        x = x_ref[...]
        a = jnp.dot(x, wgbf_ref[...], preferred_element_type=F32)
        b = jnp.dot(x, wubf_ref[...], preferred_element_type=F32)
        o_ref[...] = (a * jax.nn.sigmoid(a) * b).astype(BF16)

    @pl.when(d >= nv_ref[0])
    def _():
        o_ref[...] = jnp.zeros_like(o_ref)


def _expert_up(xs, blk_e, nvalid, wg, wu, tf):
    p = xs.shape[0]
    ff = wg.shape[2]
    return pl.pallas_call(
        _expert_up_kernel,
        out_shape=jax.ShapeDtypeStruct((p, ff), BF16),
        grid_spec=pltpu.PrefetchScalarGridSpec(
            num_scalar_prefetch=2,
            grid=(ff // tf, p // BLK),
            in_specs=[pl.BlockSpec((BLK, D), lambda f, d, be, nv: (d, 0)),
                      pl.BlockSpec((1, D, tf), lambda f, d, be, nv: (be[d], 0, f)),
                      pl.BlockSpec((1, D, tf), lambda f, d, be, nv: (be[d], 0, f))],
            out_specs=pl.BlockSpec((BLK, tf), lambda f, d, be, nv: (d, f)),
            scratch_shapes=[pltpu.VMEM((D, tf), BF16), pltpu.VMEM((D, tf), BF16)]),
        compiler_params=_params(("arbitrary", "arbitrary"), V7X_VMEM_LIMIT),
        name="moe_up",
    )(blk_e, nvalid, xs, wg, wu)


def _expert_down_kernel(be_ref, nv_ref, h_ref, gs_ref, wd_ref, o_ref, wdbf_ref):
    d = pl.program_id(1)
    changed = (d == 0) | (be_ref[d] != be_ref[jnp.maximum(d - 1, 0)])

    @pl.when(changed)
    def _():
        wdbf_ref[...] = wd_ref[0].astype(BF16)

    @pl.when(d < nv_ref[0])
    def _():
        y = jnp.dot(h_ref[...], wdbf_ref[...], preferred_element_type=F32)
        o_ref[...] = (y * gs_ref[...]).astype(BF16)

    @pl.when(d >= nv_ref[0])
    def _():
        o_ref[...] = jnp.zeros_like(o_ref)


def _expert_down(hh, gsort, blk_e, nvalid, wd, tn):
    p, ff = hh.shape
    return pl.pallas_call(
        _expert_down_kernel,
        out_shape=jax.ShapeDtypeStruct((p, D), BF16),
        grid_spec=pltpu.PrefetchScalarGridSpec(
            num_scalar_prefetch=2,
            grid=(D // tn, p // BLK),
            in_specs=[pl.BlockSpec((BLK, ff), lambda n, d, be, nv: (d, 0)),
                      pl.BlockSpec((BLK, 1), lambda n, d, be, nv: (d, 0)),
                      pl.BlockSpec((1, ff, tn), lambda n, d, be, nv: (be[d], 0, n))],
            out_specs=pl.BlockSpec((BLK, tn), lambda n, d, be, nv: (d, n)),
            scratch_shapes=[pltpu.VMEM((ff, tn), BF16)]),
        compiler_params=_params(("arbitrary", "arbitrary"), V7X_VMEM_LIMIT),
        name="moe_down",
    )(blk_e, nvalid, hh, gsort, wd)


def _combine_kernel(rs_ref, rc_ref, xp_ref, xs_ref, pos_ref, g_ref, y_hbm, op_ref, os_ref, wbuf, obuf, acc_ref,
                    sem, osem, *, n_prompt_chunks):
    i = pl.program_id(0)
    nch = pl.num_programs(0)

    def window_start(chunk, e):
        return pl.multiple_of(rs_ref[chunk * N_EXP + e] // BF16_ROWS * BF16_ROWS, BF16_ROWS)

    def copy(chunk, e, slot):
        return pltpu.make_async_copy(y_hbm.at[pl.ds(window_start(chunk, e), CWIN)],
                                     wbuf.at[slot, pl.ds(e * CWIN, CWIN)], sem.at[slot, e])

    @pl.when(i == 0)
    def _():
        for e in range(N_EXP):
            copy(0, e, 0).start()

    slot = i % 2

    @pl.when(i + 1 < nch)
    def _():
        for e in range(N_EXP):
            copy(i + 1, e, 1 - slot).start()

    col = lax.broadcasted_iota(I32, (CHUNK, CWIN), 1)
    hits = []
    for e in range(N_EXP):
        copy(i, e, slot).wait()
        hits.append((pos_ref[:, e:e + 1] - window_start(i, e)) == col)
    onehot = jnp.concatenate(hits, axis=1).astype(BF16)
    acc_ref[...] = jnp.dot(onehot, wbuf[slot], preferred_element_type=F32)

    for e in range(N_EXP):
        tail = window_start(i, e) + CWIN

        @pl.when(rs_ref[i * N_EXP + e] + rc_ref[i * N_EXP + e] > tail)
        def _():
            cp = pltpu.make_async_copy(y_hbm.at[pl.ds(tail, CWIN)], obuf, osem)
            cp.start()
            cp.wait()
            hit = ((pos_ref[:, e:e + 1] - tail) == col).astype(BF16)
            acc_ref[...] += jnp.dot(hit, obuf[...], preferred_element_type=F32)

    x = jnp.where(i < n_prompt_chunks, xp_ref[...], xs_ref[...])
    out = x + _rms(acc_ref[...], g_ref[...])

    @pl.when(i < n_prompt_chunks)
    def _():
        op_ref[...] = out

    @pl.when(i >= n_prompt_chunks)
    def _():
        os_ref[...] = out


def _combine(run_start, run_count, xp, xs, pos, g, ys):
    npc = xp.shape[0] // CHUNK
    nsc = xs.shape[0] // CHUNK
    nch = npc + nsc
    tok8 = pl.BlockSpec((CHUNK, N_EXP), lambda i, *_: (i, 0))
    pspec = pl.BlockSpec((CHUNK, D), lambda i, *_: (jnp.minimum(i, npc - 1), 0))
    sspec = pl.BlockSpec((CHUNK, D), lambda i, *_: (jnp.maximum(i - npc, 0), 0))
    return pl.pallas_call(
        functools.partial(_combine_kernel, n_prompt_chunks=npc),
        out_shape=(jax.ShapeDtypeStruct(xp.shape, F32), jax.ShapeDtypeStruct(xs.shape, F32)),
        grid_spec=pltpu.PrefetchScalarGridSpec(
            num_scalar_prefetch=2,
            grid=(nch,),
            in_specs=[pspec, sspec, tok8, pl.BlockSpec((1, D), lambda i, *_: (0, 0)),
                      pl.BlockSpec(memory_space=pl.ANY)],
            out_specs=(pspec, sspec),
            scratch_shapes=[pltpu.VMEM((2, N_EXP * CWIN, D), BF16), pltpu.VMEM((CWIN, D), BF16),
                            pltpu.VMEM((CHUNK, D), F32),
                            pltpu.SemaphoreType.DMA((2, N_EXP)), pltpu.SemaphoreType.DMA]),
        compiler_params=_params(("arbitrary",), V7X_VMEM_LIMIT),
        name="moe_combine",
    )(run_start, run_count, xp, xs, pos, g, ys)


def _moe(xp, xs, g4, g5, w_router, wg, wu, wd):
    npc = xp.shape[0] // CHUNK
    nch = npc + xs.shape[0] // CHUNK
    t = nch * CHUNK
    h, rk, cw, cnt = _route(xp, xs, g4, w_router)

    after = cnt.reshape(nch, N_EXP).astype(I32)
    before = jnp.concatenate([jnp.zeros((1, N_EXP), I32), after[:-1]], axis=0)
    counts = after[-1]
    gsz = (counts + BLK - 1) // BLK * BLK
    gend = jnp.cumsum(gsz)
    gstart = gend - gsz
    nblk = (2 * t + N_EXP * (BLK - 1)) // BLK + 1 + (2 * CWIN + BLK - 1) // BLK
    bstart = jnp.arange(nblk, dtype=I32) * BLK
    blk_e = jnp.minimum(jnp.sum(gend[None, :] <= bstart[:, None], axis=1), N_EXP - 1).astype(I32)
    valid = bstart < gend[-1]
    r0 = bstart - gstart[blk_e]
    aft_e = after[:, blk_e]
    bef_e = before[:, blk_e]
    c_lo = jnp.where(valid, jnp.sum(aft_e <= r0[None, :], axis=0), 0).astype(I32)
    c_hi = jnp.where(valid, jnp.sum(bef_e < (r0 + BLK)[None, :], axis=0) - 1, -1).astype(I32)
    c_lo = jnp.minimum(c_lo, nch - 1)
    pos = jnp.where(rk >= 0, rk + gstart[None, :], -1).astype(I32)
    pos_t = pos.T.reshape(N_EXP, nch, 1, CHUNK)
    cw_t = cw.T.reshape(N_EXP, nch, 1, CHUNK)
    run_start = (gstart[None, :] + before).astype(I32).reshape(-1)
    run_count = (after - before).reshape(-1)
    nvalid = (gend[-1:] // BLK).astype(I32)
    pcount = jnp.maximum(c_hi - c_lo + 1, 0)
    pend = jnp.cumsum(pcount)
    pbase = pend - pcount
    k = jnp.arange(nblk + 2 * nch * N_EXP, dtype=I32)
    pair_blk = jnp.minimum(jnp.sum(pend[None, :] <= k[:, None], axis=1), nblk - 1)
    pair_c = jnp.clip(c_lo[pair_blk] + k - pbase[pair_blk], 0, nch - 1).astype(I32)
    pair_e = blk_e[pair_blk]

    xsort, gsort = _gather(h, pos_t, cw_t, blk_e, pbase.astype(I32), pcount.astype(I32), r0.astype(I32),
                           before.reshape(-1), after.reshape(-1), pair_c, pair_e, pend[-1:].astype(I32), nblk)
    hh = _expert_up(xsort, blk_e, nvalid, wg, wu, tf=896)
    ysort = _expert_down(hh, gsort, blk_e, nvalid, wd, tn=512)
    return _combine(run_start, run_count, xp, xs, pos, g5, ysort)


def kernel(x_prompt, x_sample, cache_conv, cache_mem_k, cache_mem_v, state_ssm_re, state_ssm_im, mem_prompt, norm_g, mem_norm_g, w_xq, w_xk, w_xv, w_xo, conv_w_pw1, conv_b_pw1, conv_w_dw, conv_b_dw, conv_ln_g, conv_ln_b, conv_w_pw2, conv_b_pw2, ssm_a_re, ssm_a_im, ssm_log_dt, ssm_b_re, ssm_b_im, ssm_c_re, ssm_c_im, ssm_d, ssm_w_glu, ssm_b_glu, ffn_w_gate, ffn_w_up, ffn_w_down, moe_w_router, moe_w_gate, moe_w_up, moe_w_down):
    nbp, seqp, _ = x_prompt.shape
    nbs, seqs, _ = x_sample.shape
    tp = nbp * seqp
    ts = nbs * seqs
    row = lambda a: a.reshape(1, -1)
    g = lambda i, k: norm_g[i, k].reshape(1, D)

    nl = w_xk.shape[0]
    pk, pv = _mem_kv(mem_prompt, mem_norm_g.reshape(nl, 1, D), w_xk, w_xv)
    p_mem_k = pk.reshape(nl, nbp, N_MEM, N_HEADS, HEAD_DIM)
    p_mem_v = pv.reshape(nl, nbp, N_MEM, N_HEADS, HEAD_DIM)
    ck = cache_mem_k.reshape(nl * nbs, N_MEM, N_HEADS, HEAD_DIM)
    cv = cache_mem_v.reshape(nl * nbs, N_MEM, N_HEADS, HEAD_DIM)

    conv_args = (conv_w_dw[0], row(conv_b_dw[0]), row(conv_ln_g[0]), row(conv_ln_b[0]),
                 conv_w_pw2[0], row(conv_b_pw2[0]), g(0, 1))
    xp = x_prompt.reshape(tp, D)
    xs = x_sample.reshape(ts, D)
    up = _conv_pw1(xp, g(0, 0), conv_w_pw1[0], row(conv_b_pw1[0]), tm=512).reshape(nbp, seqp, D)
    us = _conv_pw1(xs, g(0, 0), conv_w_pw1[0], row(conv_b_pw1[0]), tm=ts).reshape(nbs, seqs, D)
    hist_p = jnp.zeros((nbp, HIST, D), F32)
    hist_s = jnp.pad(cache_conv[0], ((0, 0), (HIST - CONV_W + 1, 0), (0, 0)))
    xp = _conv_dw_pw2(up, x_prompt, hist_p, *conv_args, tl=256).reshape(tp, D)
    xs = _conv_dw_pw2(us, x_sample, hist_s, *conv_args, tl=seqs).reshape(ts, D)
    p_conv = up[:, seqp - (CONV_W - 1):][None]
    s_conv = jnp.concatenate([cache_conv[0], us], axis=1)[:, -(CONV_W - 1):][None]

    tma = 512
    xp = _attn(xp, pk, pv, 0, w_xq[0], w_xo[0], g(0, 2), g(0, 3), nbp, seqp, tma)
    xs = _attn(xs, ck, cv, 0, w_xq[0], w_xo[0], g(0, 2), g(0, 3), nbs, seqs, seqs)

    ffn_w = (ffn_w_gate[0], ffn_w_up[0], ffn_w_down[0])
    xp = _ffn(xp, g(0, 4), g(0, 5), *ffn_w, tm=1024, tf=256, out_shape=(tp, D), out_index=lambda i, f: (i, 0))
    xs = _ffn(xs, g(0, 4), g(0, 5), *ffn_w, tm=ts, tf=256, out_shape=(ts, D), out_index=lambda i, f: (i, 0))

    ab_r, ab_i, bb_r, bb_i = _ssm_prep(ssm_a_re[0], ssm_a_im[0], ssm_log_dt[0], ssm_b_re[0], ssm_b_im[0])
    bm_r = _block_diag(bb_r.transpose(0, 2, 1)).astype(BF16)
    bm_i = _block_diag(bb_i.transpose(0, 2, 1)).astype(BF16)
    cm_r = _block_diag(ssm_c_re[0].transpose(0, 2, 1)).astype(BF16)
    cm_i = _block_diag(-ssm_c_im[0].transpose(0, 2, 1)).astype(BF16)
    ssm_args = (ab_r, ab_i, bm_r, bm_i, cm_r, cm_i, row(ssm_d[0]), ssm_w_glu[0], row(ssm_b_glu[0]))
    zero_state = jnp.zeros((SUBLANES, SSM_N), F32)
    xp, p_sr, p_si = _ssm(xp.reshape(nbp, seqp, D), g(1, 0), g(1, 1), zero_state, zero_state, *ssm_args, tl=64)
    xs, s_sr, s_si = _ssm(xs.reshape(nbs, seqs, D), g(1, 0), g(1, 1), state_ssm_re[0].reshape(nbs, SSM_N),
                          state_ssm_im[0].reshape(nbs, SSM_N), *ssm_args, tl=seqs)
    st = lambda a, n: a[:n].reshape(1, n, SSM_G, SSM_P)
    p_ssm_re, p_ssm_im = st(p_sr, nbp), st(p_si, nbp)
    s_ssm_re, s_ssm_im = st(s_sr, nbs), st(s_si, nbs)

    xp = _attn(xp.reshape(tp, D), pk, pv, nbp, w_xq[1], w_xo[1], g(1, 2), g(1, 3), nbp, seqp, tma)
    xs = _attn(xs.reshape(ts, D), ck, cv, nbs, w_xq[1], w_xo[1], g(1, 2), g(1, 3), nbs, seqs, seqs)

    yp, ysm = _moe(xp, xs, g(1, 4), g(1, 5), moe_w_router[0], moe_w_gate[0], moe_w_up[0], moe_w_down[0])
    return (yp.reshape(nbp, seqp, D), ysm.reshape(nbs, seqs, D), p_conv, p_mem_k, p_mem_v,
            p_ssm_re, p_ssm_im, s_conv, s_ssm_re, s_ssm_im)
```

```python
import functools
import math

import jax
import jax.numpy as jnp
from jax import lax
from jax.experimental import pallas as pl
from jax.experimental.pallas import tpu as pltpu

F32 = jnp.float32
BF16 = jnp.bfloat16
I32 = jnp.int32

D = 1024
CONV_W = 31
HIST = 32
N_MEM = 256
N_HEADS = 4
HEAD_DIM = D // N_HEADS
SSM_G = 64
SSM_C = 16
SSM_P = 64
SSM_N = SSM_G * SSM_P
SSM_CB = 256
SSM_NCB = D // SSM_CB
SSM_SB = SSM_CB // SSM_C * SSM_P
N_EXP = 8
EPS = 1e-6

V7X_VMEM_LIMIT = 56 * 1024 * 1024
SUBLANES = 8

CHUNK = 256
BLK = 512
WIN = 512
GWIN = 128
GSLOTS = 4
CWIN = 256
BF16_ROWS = 16


def _params(sem, vmem=None):
    return pltpu.CompilerParams(dimension_semantics=sem, vmem_limit_bytes=vmem)


def _rms(x, g):
    return x * lax.rsqrt(jnp.mean(x * x, axis=-1, keepdims=True) + EPS) * g


def _const_spec(shape):
    nd = len(shape)
    return pl.BlockSpec(shape, lambda *_: (0,) * nd)


def _pw1_kernel(x_ref, g_ref, w_ref, b_ref, u_ref, wbf_ref):
    @pl.when(pl.program_id(0) == 0)
    def _():
        wbf_ref[...] = w_ref[...].astype(BF16)

    h = _rms(x_ref[...], g_ref[...]).astype(BF16)
    z = jnp.dot(h, wbf_ref[...], preferred_element_type=F32) + b_ref[...]
    u_ref[...] = z[:, :D] * jax.nn.sigmoid(z[:, D:])


def _conv_pw1(x, g, w, b, tm):
    t = x.shape[0]
    return pl.pallas_call(
        _pw1_kernel,
        out_shape=jax.ShapeDtypeStruct((t, D), F32),
        grid=(t // tm,),
        in_specs=[pl.BlockSpec((tm, D), lambda i: (i, 0)),
                  _const_spec((1, D)), _const_spec((D, 2 * D)), _const_spec((1, 2 * D))],
        out_specs=pl.BlockSpec((tm, D), lambda i: (i, 0)),
        scratch_shapes=[pltpu.VMEM((D, 2 * D), BF16)],
        compiler_params=_params(("arbitrary",), V7X_VMEM_LIMIT),
        name="conv_pw1",
    )(x, g, w, b)


def _conv2_kernel(u_ref, x_ref, hist_ref, wdw_ref, bdw_ref, lng_ref, lnb_ref, w2_ref, b2_ref,
                  g_ref, o_ref, ext_ref, sh_ref, conv_ref, w2bf_ref, *, tl, rt):
    bi = pl.program_id(0)
    li = pl.program_id(1)

    @pl.when((bi == 0) & (li == 0))
    def _():
        w2bf_ref[...] = w2_ref[...].astype(BF16)

    @pl.when(li == 0)
    def _():
        ext_ref[0:HIST, :] = hist_ref[0]

    @pl.when(li > 0)
    def _():
        ext_ref[0:HIST, :] = ext_ref[tl:tl + HIST, :]

    ext_ref[HIST:HIST + tl, :] = u_ref[0]
    for s in range(1, SUBLANES):
        sh_ref[s - 1] = ext_ref[pl.ds(s, tl + HIST - SUBLANES), :]

    def rows(i, carry):
        r0 = pl.multiple_of(i * rt, rt)
        acc = jnp.zeros((rt, D), F32) + bdw_ref[...]
        for k in range(CONV_W):
            off = HIST - CONV_W + 1 + k
            s = off % SUBLANES
            base = off - s
            if s == 0:
                src = ext_ref[pl.ds(r0 + base, rt), :]
            else:
                src = sh_ref[s - 1, pl.ds(r0 + base, rt), :]
            acc = acc + wdw_ref[k:k + 1, :] * src
        conv_ref[pl.ds(r0, rt), :] = acc
        return carry

    lax.fori_loop(0, tl // rt, rows, 0)
    acc = conv_ref[...]
    mu = jnp.mean(acc, axis=-1, keepdims=True)
    xc = acc - mu
    var = jnp.mean(xc * xc, axis=-1, keepdims=True)
    y = xc * lax.rsqrt(var + EPS) * lng_ref[...] + lnb_ref[...]
    y = y * jax.nn.sigmoid(y)
    t = jnp.dot(y.astype(BF16), w2bf_ref[...], preferred_element_type=F32) + b2_ref[...]
    o_ref[0] = x_ref[0] + _rms(t, g_ref[...])


def _conv_dw_pw2(u, x, hist, wdw, bdw, lng, lnb, w2, b2, g, tl):
    nb, seq, _ = u.shape
    tok = pl.BlockSpec((1, tl, D), lambda b, l: (b, l, 0))
    return pl.pallas_call(
        functools.partial(_conv2_kernel, tl=tl, rt=min(tl, 32)),
        out_shape=jax.ShapeDtypeStruct((nb, seq, D), F32),
        grid=(nb, seq // tl),
        in_specs=[tok, tok, pl.BlockSpec((1, HIST, D), lambda b, l: (b, 0, 0)),
                  _const_spec((CONV_W, D)), _const_spec((1, D)), _const_spec((1, D)), _const_spec((1, D)),
                  _const_spec((D, D)), _const_spec((1, D)), _const_spec((1, D))],
        out_specs=tok,
        scratch_shapes=[pltpu.VMEM((tl + HIST, D), F32),
                        pltpu.VMEM((SUBLANES - 1, tl + HIST - SUBLANES, D), F32),
                        pltpu.VMEM((tl, D), F32), pltpu.VMEM((D, D), BF16)],
        compiler_params=_params(("arbitrary", "arbitrary"), V7X_VMEM_LIMIT),
        name="conv_dw_pw2",
    )(u, x, hist, wdw, bdw, lng, lnb, w2, b2, g)


def _memkv_kernel(m_ref, g_ref, wk_ref, wv_ref, k_ref, v_ref, wkbf_ref, wvbf_ref):
    @pl.when(pl.program_id(1) == 0)
    def _():
        wkbf_ref[...] = wk_ref[0].astype(BF16)
        wvbf_ref[...] = wv_ref[0].astype(BF16)

    m = _rms(m_ref[0], g_ref[0]).astype(BF16)
    k = jnp.dot(m, wkbf_ref[...], preferred_element_type=F32)
    v = jnp.dot(m, wvbf_ref[...], preferred_element_type=F32)
    for hd in range(N_HEADS):
        sl = slice(hd * HEAD_DIM, (hd + 1) * HEAD_DIM)
        k_ref[0, :, hd, :] = k[:, sl]
        v_ref[0, :, hd, :] = v[:, sl]


def _mem_kv(mem, g, wk, wv):
    nb = mem.shape[0]
    nl = wk.shape[0]
    kv = jax.ShapeDtypeStruct((nl * nb, N_MEM, N_HEADS, HEAD_DIM), F32)
    out = pl.BlockSpec((1, N_MEM, N_HEADS, HEAD_DIM), lambda l, b: (l * nb + b, 0, 0, 0))
    w = pl.BlockSpec((1, D, D), lambda l, b: (l, 0, 0))
    return pl.pallas_call(
        _memkv_kernel,
        out_shape=(kv, kv),
        grid=(nl, nb),
        in_specs=[pl.BlockSpec((1, N_MEM, D), lambda l, b: (b, 0, 0)),
                  pl.BlockSpec((1, 1, D), lambda l, b: (l, 0, 0)), w, w],
        out_specs=(out, out),
        scratch_shapes=[pltpu.VMEM((D, D), BF16), pltpu.VMEM((D, D), BF16)],
        compiler_params=_params(("arbitrary", "arbitrary"), V7X_VMEM_LIMIT),
        name="mem_kv",
    )(mem, g, wk, wv)


def _attn_kernel(x_ref, k_ref, v_ref, wq_ref, wo_ref, g2_ref, g3_ref, o_ref, wqbf_ref, wobf_ref, kb_ref, vb_ref):
    @pl.when((pl.program_id(0) == 0) & (pl.program_id(1) == 0))
    def _():
        wqbf_ref[...] = wq_ref[...].astype(BF16)
        wobf_ref[...] = wo_ref[...].astype(BF16)

    @pl.when(pl.program_id(1) == 0)
    def _():
        for hd in range(N_HEADS):
            sl = slice(hd * HEAD_DIM, (hd + 1) * HEAD_DIM)
            kb_ref[:, sl] = k_ref[0, :, hd, :].astype(BF16)
            vb_ref[:, sl] = v_ref[0, :, hd, :].astype(BF16)

    x = x_ref[...]
    h = _rms(x, g2_ref[...]).astype(BF16)
    q = (jnp.dot(h, wqbf_ref[...], preferred_element_type=F32) * (HEAD_DIM ** -0.5)).astype(BF16)
    heads = []
    for hd in range(N_HEADS):
        sl = slice(hd * HEAD_DIM, (hd + 1) * HEAD_DIM)
        s = lax.dot_general(q[:, sl], kb_ref[:, sl], (((1,), (1,)), ((), ())), preferred_element_type=F32)
        p = jnp.exp(s - jnp.max(s, axis=-1, keepdims=True))
        p = p / jnp.sum(p, axis=-1, keepdims=True)
        heads.append(jnp.dot(p.astype(BF16), vb_ref[:, sl], preferred_element_type=F32))
    o = jnp.concatenate(heads, axis=1).astype(BF16)
    t = jnp.dot(o, wobf_ref[...], preferred_element_type=F32)
    o_ref[...] = x + _rms(t, g3_ref[...])


def _attn(x, k, v, kv_base, wq, wo, g2, g3, nb, seq, tm):
    tok = pl.BlockSpec((tm, D), lambda b, l: (b * (seq // tm) + l, 0))
    kv = pl.BlockSpec((1, N_MEM, N_HEADS, HEAD_DIM), lambda b, l: (kv_base + b, 0, 0, 0))
    return pl.pallas_call(
        _attn_kernel,
        out_shape=jax.ShapeDtypeStruct((nb * seq, D), F32),
        grid=(nb, seq // tm),
        in_specs=[tok, kv, kv, _const_spec((D, D)), _const_spec((D, D)), _const_spec((1, D)), _const_spec((1, D))],
        out_specs=tok,
        scratch_shapes=[pltpu.VMEM((D, D), BF16), pltpu.VMEM((D, D), BF16),
                        pltpu.VMEM((N_MEM, D), BF16), pltpu.VMEM((N_MEM, D), BF16)],
        compiler_params=_params(("arbitrary", "arbitrary"), V7X_VMEM_LIMIT),
        name="mem_attn",
    )(x, k, v, wq, wo, g2, g3)


def _cast_kernel(w_ref, o_ref):
    o_ref[...] = w_ref[...].astype(BF16)


def _to_bf16(w, nsplit):
    r, c = w.shape
    blk = pl.BlockSpec((r // nsplit, c), lambda i: (i, 0))
    return pl.pallas_call(
        _cast_kernel,
        out_shape=jax.ShapeDtypeStruct((r, c), BF16),
        grid=(nsplit,),
        in_specs=[blk],
        out_specs=blk,
        compiler_params=_params(("arbitrary",)),
        name="to_bf16",
    )(w)


def _ffn_kernel(x_ref, g4_ref, g5_ref, wg_ref, wu_ref, wd_ref, o_ref, h_ref, acc_ref):
    f = pl.program_id(1)

    @pl.when(f == 0)
    def _():
        h_ref[...] = _rms(x_ref[...], g4_ref[...]).astype(BF16)
        acc_ref[...] = jnp.zeros_like(acc_ref)

    h = h_ref[...]
    a = jnp.dot(h, wg_ref[...], preferred_element_type=F32)
    b = jnp.dot(h, wu_ref[...], preferred_element_type=F32)
    hh = (a * jax.nn.sigmoid(a) * b).astype(BF16)
    acc_ref[...] += jnp.dot(hh, wd_ref[...], preferred_element_type=F32)

    @pl.when(f == pl.num_programs(1) - 1)
    def _():
        o_ref[...] = x_ref[...] + _rms(acc_ref[...], g5_ref[...])


def _ffn(x, g4, g5, wg, wu, wd, tm, tf, out_shape, out_index):
    t = x.shape[0]
    ff = wg.shape[1]
    return pl.pallas_call(
        _ffn_kernel,
        out_shape=jax.ShapeDtypeStruct(out_shape, F32),
        grid=(t // tm, ff // tf),
        in_specs=[pl.BlockSpec((tm, D), lambda i, f: (i, 0)),
                  _const_spec((1, D)), _const_spec((1, D)),
                  pl.BlockSpec((D, tf), lambda i, f: (0, f)),
                  pl.BlockSpec((D, tf), lambda i, f: (0, f)),
                  pl.BlockSpec((tf, D), lambda i, f: (f, 0))],
        out_specs=pl.BlockSpec((tm, D), out_index),
        scratch_shapes=[pltpu.VMEM((tm, D), BF16), pltpu.VMEM((tm, D), F32)],
        compiler_params=_params(("arbitrary", "arbitrary"), V7X_VMEM_LIMIT),
        name="dense_ffn",
    )(x, g4, g5, wg, wu, wd)


def _ssm_prep_kernel(lr_ref, li_ref, ldt_ref, br_ref, bi_ref, abr_ref, abi_ref, bbr_ref, bbi_ref):
    dt = jnp.exp(ldt_ref[...])
    lr = lr_ref[...]
    li = li_ref[...]
    mag = jnp.exp(lr * dt)
    ab_r = mag * jnp.cos(li * dt)
    ab_i = mag * jnp.sin(li * dt)
    den = lr * lr + li * li
    nr = ab_r - 1.0
    k_r = (nr * lr + ab_i * li) / den
    k_i = (ab_i * lr - nr * li) / den
    br = br_ref[...]
    bi = bi_ref[...]
    abr_ref[...] = ab_r
    abi_ref[...] = ab_i
    bbr_ref[...] = k_r * br - k_i * bi
    bbi_ref[...] = k_r * bi + k_i * br


def _ssm_prep(a_re, a_im, log_dt, b_re, b_im):
    n = SSM_P * SSM_C
    rep = lambda a: jnp.repeat(a, SSM_C, axis=1)
    shp = jax.ShapeDtypeStruct((SSM_G, n), F32)
    abr, abi, bbr, bbi = pl.pallas_call(
        _ssm_prep_kernel,
        out_shape=(shp, shp, shp, shp),
        name="ssm_prep",
    )(rep(a_re), rep(a_im), log_dt.reshape(SSM_G, 1), b_re.reshape(SSM_G, n), b_im.reshape(SSM_G, n))
    pick = lambda a: a.reshape(SSM_G, SSM_P, SSM_C)[:, :, 0].reshape(1, SSM_N)
    return pick(abr), pick(abi), bbr.reshape(SSM_G, SSM_P, SSM_C), bbi.reshape(SSM_G, SSM_P, SSM_C)


def _block_diag(m):
    gpb = SSM_CB // SSM_C
    _, a, b = m.shape
    m = m.reshape(SSM_NCB, gpb, a, 1, b)
    eye = jnp.eye(gpb, dtype=m.dtype).reshape(1, gpb, 1, gpb, 1)
    return (m * eye).reshape(SSM_NCB, gpb * a, gpb * b)


def _ssm_kernel(x_ref, g0_ref, g1_ref, s0r_ref, s0i_ref, ar_ref, ai_ref, bmr_ref, bmi_ref, cmr_ref, cmi_ref,
                d_ref, wglu_ref, bglu_ref, o_ref, sr_ref, si_ref,
                wbf_ref, bur_ref, bui_ref, *, nb, rb, lc):
    @pl.when(pl.program_id(0) == 0)
    def _():
        wbf_ref[...] = wglu_ref[...].astype(BF16)
        sr_ref[...] = s0r_ref[...]
        si_ref[...] = s0i_ref[...]

    tl = rb // nb
    x = x_ref[...].reshape(rb, D)
    h = _rms(x, g0_ref[...])
    r = lax.broadcasted_iota(I32, (rb, rb), 0)
    c = lax.broadcasted_iota(I32, (rb, rb), 1)
    to_time_major = (c == (r & (nb - 1)) * tl + (r >> (nb.bit_length() - 1))).astype(BF16)
    to_batch_major = (c == (r & (tl - 1)) * nb + (r >> (tl.bit_length() - 1))).astype(BF16)
    hb = jnp.dot(to_time_major, h.astype(BF16), preferred_element_type=F32).astype(BF16)
    for cb in range(SSM_NCB):
        hs = hb[:, cb * SSM_CB:(cb + 1) * SSM_CB]
        bur_ref[:, cb * SSM_SB:(cb + 1) * SSM_SB] = jnp.dot(hs, bmr_ref[cb], preferred_element_type=F32)
        bui_ref[:, cb * SSM_SB:(cb + 1) * SSM_SB] = jnp.dot(hs, bmi_ref[cb], preferred_element_type=F32)

    rows = max(nb, SUBLANES)
    for c in range(SSM_N // lc):
        cs = slice(c * lc, (c + 1) * lc)
        a_r = jnp.broadcast_to(ar_ref[:, cs], (rows, lc))
        a_i = jnp.broadcast_to(ai_ref[:, cs], (rows, lc))

        if nb == 4:
            low = lax.broadcasted_iota(I32, (SUBLANES, lc), 0) < nb

            def step(j, carry):
                s_r, s_i = carry
                r0 = pl.multiple_of(j * SUBLANES, SUBLANES)
                u_r = bur_ref[pl.ds(r0, SUBLANES), cs]
                u_i = bui_ref[pl.ds(r0, SUBLANES), cs]
                x1r = a_r * s_r - a_i * s_i + u_r
                x1i = a_r * s_i + a_i * s_r + u_i
                p_r = pltpu.roll(x1r, nb, 0)
                p_i = pltpu.roll(x1i, nb, 0)
                x2r = a_r * p_r - a_i * p_i + u_r
                x2i = a_r * p_i + a_i * p_r + u_i
                bur_ref[pl.ds(r0, SUBLANES), cs] = jnp.where(low, x1r, x2r)
                bui_ref[pl.ds(r0, SUBLANES), cs] = jnp.where(low, x1i, x2i)
                return (jnp.where(low, pltpu.roll(x2r, nb, 0), x2r),
                        jnp.where(low, pltpu.roll(x2i, nb, 0), x2i))
        else:
            def step(j, carry):
                s_r, s_i = carry
                r0 = pl.multiple_of(j * nb, SUBLANES)
                u_r = bur_ref[pl.ds(r0, nb), cs]
                u_i = bui_ref[pl.ds(r0, nb), cs]
                n_r = a_r * s_r - a_i * s_i + u_r
                n_i = a_r * s_i + a_i * s_r + u_i
                bur_ref[pl.ds(r0, nb), cs] = n_r
                bui_ref[pl.ds(r0, nb), cs] = n_i
                return n_r, n_i

        s_r, s_i = lax.fori_loop(0, rb // rows, step, (sr_ref[:, cs], si_ref[:, cs]))
        sr_ref[:, cs] = s_r
        si_ref[:, cs] = s_i

    ys = []
    for cb in range(SSM_NCB):
        ss = slice(cb * SSM_SB, (cb + 1) * SSM_SB)
        ys.append(jnp.dot(bur_ref[:, ss].astype(BF16), cmr_ref[cb], preferred_element_type=F32)
                  + jnp.dot(bui_ref[:, ss].astype(BF16), cmi_ref[cb], preferred_element_type=F32))
    y_tm = jnp.concatenate(ys, axis=1)
    y1 = y_tm.astype(BF16)
    rem = y_tm - y1.astype(F32)
    y2 = rem.astype(BF16)
    y3 = (rem - y2.astype(F32)).astype(BF16)
    y = (jnp.dot(to_batch_major, y1, preferred_element_type=F32)
         + jnp.dot(to_batch_major, y2, preferred_element_type=F32)
         + jnp.dot(to_batch_major, y3, preferred_element_type=F32))
    y = y + d_ref[...] * h
    y = jax.nn.gelu(y).astype(BF16)
    z = jnp.dot(y, wbf_ref[...], preferred_element_type=F32) + bglu_ref[...]
    t = z[:, :D] * jax.nn.sigmoid(z[:, D:])
    o_ref[...] = (x + _rms(t, g1_ref[...])).reshape(nb, tl, D)


def _ssm(x, g0, g1, s0r, s0i, ab_r, ab_i, bm_r, bm_i, cm_r, cm_i, d, wglu, bglu, tl):
    nb, seq, _ = x.shape
    rb = nb * tl
    rows = max(nb, SUBLANES)
    st = jax.ShapeDtypeStruct((rows, SSM_N), F32)
    row = pl.BlockSpec((nb, tl, D), lambda i: (0, i, 0))
    return pl.pallas_call(
        functools.partial(_ssm_kernel, nb=nb, rb=rb, lc=512),
        out_shape=(jax.ShapeDtypeStruct((nb, seq, D), F32), st, st),
        grid=(seq // tl,),
        in_specs=[row, _const_spec((1, D)), _const_spec((1, D)),
                  _const_spec((rows, SSM_N)), _const_spec((rows, SSM_N)),
                  _const_spec((1, SSM_N)), _const_spec((1, SSM_N)),
                  _const_spec((SSM_NCB, SSM_CB, SSM_SB)), _const_spec((SSM_NCB, SSM_CB, SSM_SB)),
                  _const_spec((SSM_NCB, SSM_SB, SSM_CB)), _const_spec((SSM_NCB, SSM_SB, SSM_CB)),
                  _const_spec((1, D)), _const_spec((D, 2 * D)), _const_spec((1, 2 * D))],
        out_specs=(row, _const_spec((rows, SSM_N)), _const_spec((rows, SSM_N))),
        scratch_shapes=[pltpu.VMEM((D, 2 * D), BF16), pltpu.VMEM((rb, SSM_N), F32), pltpu.VMEM((rb, SSM_N), F32)],
        compiler_params=_params(("arbitrary",), V7X_VMEM_LIMIT),
        name="ssm",
    )(x, g0, g1, s0r, s0i, ab_r, ab_i, bm_r, bm_i, cm_r, cm_i, d, wglu, bglu)


def _route_kernel(xp_ref, xs_ref, g_ref, wr_ref, h_ref, rk_ref, cw_ref, cnt_ref, carry_ref, *, n_prompt_chunks):
    i = pl.program_id(0)

    @pl.when(i == 0)
    def _():
        carry_ref[...] = jnp.zeros_like(carry_ref)

    x = jnp.where(i < n_prompt_chunks, xp_ref[...], xs_ref[...])
    h = _rms(x, g_ref[...])
    h_ref[...] = h.astype(BF16)
    lg = jnp.dot(h, wr_ref[...], preferred_element_type=F32, precision=lax.Precision.HIGHEST)
    lane = lax.broadcasted_iota(I32, lg.shape, 1)
    m1 = jnp.max(lg, axis=-1, keepdims=True)
    i1 = jnp.min(jnp.where(lg == m1, lane, N_EXP), axis=-1, keepdims=True)
    first = lane == i1
    lg2 = jnp.where(first, -jnp.inf, lg)
    m2 = jnp.max(lg2, axis=-1, keepdims=True)
    i2 = jnp.min(jnp.where(lg2 == m2, lane, N_EXP), axis=-1, keepdims=True)
    second = lane == i2
    e = jnp.exp(m2 - m1)
    den = 1.0 + e
    cw_ref[...] = jnp.where(first, 1.0 / den, 0.0) + jnp.where(second, e / den, 0.0)
    assigned = first | second
    a = assigned.astype(BF16)
    r = lax.broadcasted_iota(I32, (CHUNK, CHUNK), 0)
    c = lax.broadcasted_iota(I32, (CHUNK, CHUNK), 1)
    before = (c < r).astype(BF16)
    rank = jnp.dot(before, a, preferred_element_type=F32) + carry_ref[...]
    rk_ref[...] = jnp.where(assigned, rank.astype(I32), -1)
    carry_ref[...] += jnp.sum(assigned.astype(F32), axis=0, keepdims=True)
    cnt_ref[0] = carry_ref[...]


def _route(xp, xs, g, wr):
    npc = xp.shape[0] // CHUNK
    nch = npc + xs.shape[0] // CHUNK
    t = nch * CHUNK
    tok8 = pl.BlockSpec((CHUNK, N_EXP), lambda i: (i, 0))
    return pl.pallas_call(
        functools.partial(_route_kernel, n_prompt_chunks=npc),
        out_shape=(jax.ShapeDtypeStruct((t, D), BF16), jax.ShapeDtypeStruct((t, N_EXP), I32),
                   jax.ShapeDtypeStruct((t, N_EXP), F32), jax.ShapeDtypeStruct((nch, 1, N_EXP), F32)),
        grid=(nch,),
        in_specs=[pl.BlockSpec((CHUNK, D), lambda i: (jnp.minimum(i, npc - 1), 0)),
                  pl.BlockSpec((CHUNK, D), lambda i: (jnp.maximum(i - npc, 0), 0)),
                  _const_spec((1, D)), _const_spec((D, N_EXP))],
        out_specs=(pl.BlockSpec((CHUNK, D), lambda i: (i, 0)), tok8, tok8,
                   pl.BlockSpec((1, 1, N_EXP), lambda i: (i, 0, 0))),
        scratch_shapes=[pltpu.VMEM((1, N_EXP), F32)],
        compiler_params=_params(("arbitrary",)),
        name="moe_route",
    )(xp, xs, g, wr)


def _gather_kernel(be_ref, pb_ref, pn_ref, r0_ref, bef_ref, aft_ref, pc_ref, pe_ref, pt_ref,
                   h_hbm, pos_hbm, cw_hbm, o_ref, gs_ref, hbuf, pbuf, cbuf, acc_ref, gacc_ref, sem):
    d = pl.program_id(0)
    e = be_ref[d]
    base = pb_ref[d]
    r0 = r0_ref[d]
    total = pt_ref[0]

    def copies(idx):
        slot = idx % GSLOTS
        c = pc_ref[idx]
        ee = pe_ref[idx]
        return (pltpu.make_async_copy(h_hbm.at[pl.ds(c * CHUNK, CHUNK)], hbuf.at[slot], sem.at[0, slot]),
                pltpu.make_async_copy(pos_hbm.at[ee, c], pbuf.at[slot], sem.at[1, slot]),
                pltpu.make_async_copy(cw_hbm.at[ee, c], cbuf.at[slot], sem.at[2, slot]))

    def start(idx):
        @pl.when(idx < total)
        def _():
            for cp in copies(idx):
                cp.start()

    @pl.when(d == 0)
    def _():
        for k in range(GSLOTS - 1):
            start(k)

    acc_ref[...] = jnp.zeros_like(acc_ref)
    gacc_ref[...] = jnp.zeros_like(gacc_ref)
    row = d * BLK + lax.broadcasted_iota(I32, (GWIN, CHUNK), 0)

    def body(j, carry):
        idx = base + j
        slot = idx % GSLOTS
        for cp in copies(idx):
            cp.wait()
        start(idx + GSLOTS - 1)

        c = pc_ref[idx]
        first = jnp.clip(bef_ref[c * N_EXP + e] - r0, 0, BLK)
        last = jnp.clip(aft_ref[c * N_EXP + e] - r0, 0, BLK)

        def window(w, c2):
            w0 = pl.multiple_of(w * GWIN, GWIN)
            hit = pbuf[slot] == row + w0
            acc_ref[pl.ds(w0, GWIN), :] += jnp.dot(hit.astype(BF16), hbuf[slot], preferred_element_type=F32)
            gacc_ref[pl.ds(w0, GWIN), :] += jnp.sum(jnp.where(hit, cbuf[slot], 0.0), axis=1, keepdims=True)
            return c2

        lax.fori_loop(first // GWIN, (last + GWIN - 1) // GWIN, window, 0)
        return carry

    lax.fori_loop(0, pn_ref[d], body, 0)
    o_ref[...] = acc_ref[...].astype(BF16)
    gs_ref[...] = gacc_ref[...]


def _gather(h, pos_t, cw_t, blk_e, pbase, pcount, r0, before, after, pair_c, pair_e, ptotal, nblk):
    hbm = pl.BlockSpec(memory_space=pl.ANY)
    return pl.pallas_call(
        _gather_kernel,
        out_shape=(jax.ShapeDtypeStruct((nblk * BLK, D), BF16), jax.ShapeDtypeStruct((nblk * BLK, 1), F32)),
        grid_spec=pltpu.PrefetchScalarGridSpec(
            num_scalar_prefetch=9,
            grid=(nblk,),
            in_specs=[hbm, hbm, hbm],
            out_specs=(pl.BlockSpec((BLK, D), lambda d, *_: (d, 0)), pl.BlockSpec((BLK, 1), lambda d, *_: (d, 0))),
            scratch_shapes=[pltpu.VMEM((GSLOTS, CHUNK, D), BF16), pltpu.VMEM((GSLOTS, 1, CHUNK), I32),
                            pltpu.VMEM((GSLOTS, 1, CHUNK), F32), pltpu.VMEM((BLK, D), F32),
                            pltpu.VMEM((BLK, 1), F32), pltpu.SemaphoreType.DMA((3, GSLOTS))]),
        compiler_params=_params(("arbitrary",)),
        name="moe_gather",
    )(blk_e, pbase, pcount, r0, before, after, pair_c, pair_e, ptotal, h, pos_t, cw_t)


def _expert_up_kernel(be_ref, nv_ref, x_ref, wg_ref, wu_ref, o_ref, wgbf_ref, wubf_ref):
    d = pl.program_id(1)
    changed = (d == 0) | (be_ref[d] != be_ref[jnp.maximum(d - 1, 0)])

    @pl.when(changed)
    def _():
        wgbf_ref[...] = wg_ref[0].astype(BF16)
        wubf_ref[...] = wu_ref[0].astype(BF16)

    def up(rows):
        x = x_ref[rows, :]
        a = jnp.dot(x, wgbf_ref[...], preferred_element_type=F32)
        b = jnp.dot(x, wubf_ref[...], preferred_element_type=F32)
        o_ref[rows, :] = (a * jax.nn.sigmoid(a) * b).astype(BF16)

    half = BLK // 2
    nrows = nv_ref[d]

    @pl.when(nrows > half)
    def _():
        up(slice(0, BLK))

    @pl.when((nrows > 0) & (nrows <= half))
    def _():
        up(slice(0, half))
        o_ref[half:, :] = jnp.zeros((BLK - half, o_ref.shape[1]), BF16)

    @pl.when(nrows == 0)
    def _():
        o_ref[...] = jnp.zeros_like(o_ref)


def _expert_up(xs, blk_e, nvalid, wg, wu, tf):
    p = xs.shape[0]
    ff = wg.shape[2]
    return pl.pallas_call(
        _expert_up_kernel,
        out_shape=jax.ShapeDtypeStruct((p, ff), BF16),
        grid_spec=pltpu.PrefetchScalarGridSpec(
            num_scalar_prefetch=2,
            grid=(ff // tf, p // BLK),
            in_specs=[pl.BlockSpec((BLK, D), lambda f, d, be, nv: (d, 0)),
                      pl.BlockSpec((1, D, tf), lambda f, d, be, nv: (be[d], 0, f)),
                      pl.BlockSpec((1, D, tf), lambda f, d, be, nv: (be[d], 0, f))],
            out_specs=pl.BlockSpec((BLK, tf), lambda f, d, be, nv: (d, f)),
            scratch_shapes=[pltpu.VMEM((D, tf), BF16), pltpu.VMEM((D, tf), BF16)]),
        compiler_params=_params(("arbitrary", "arbitrary"), V7X_VMEM_LIMIT),
        name="moe_up",
    )(blk_e, nvalid, xs, wg, wu)


def _expert_down_kernel(be_ref, nv_ref, h_ref, gs_ref, wd_ref, o_ref, wdbf_ref):
    d = pl.program_id(1)
    changed = (d == 0) | (be_ref[d] != be_ref[jnp.maximum(d - 1, 0)])

    @pl.when(changed)
    def _():
        wdbf_ref[...] = wd_ref[0].astype(BF16)

    def down(rows):
        y = jnp.dot(h_ref[rows, :], wdbf_ref[...], preferred_element_type=F32)
        o_ref[rows, :] = (y * gs_ref[rows, :]).astype(BF16)

    half = BLK // 2
    nrows = nv_ref[d]

    @pl.when(nrows > half)
    def _():
        down(slice(0, BLK))

    @pl.when((nrows > 0) & (nrows <= half))
    def _():
        down(slice(0, half))
        o_ref[half:, :] = jnp.zeros((BLK - half, o_ref.shape[1]), BF16)

    @pl.when(nrows == 0)
    def _():
        o_ref[...] = jnp.zeros_like(o_ref)


def _expert_down(hh, gsort, blk_e, nvalid, wd, tn):
    p, ff = hh.shape
    return pl.pallas_call(
        _expert_down_kernel,
        out_shape=jax.ShapeDtypeStruct((p, D), BF16),
        grid_spec=pltpu.PrefetchScalarGridSpec(
            num_scalar_prefetch=2,
            grid=(D // tn, p // BLK),
            in_specs=[pl.BlockSpec((BLK, ff), lambda n, d, be, nv: (d, 0)),
                      pl.BlockSpec((BLK, 1), lambda n, d, be, nv: (d, 0)),
                      pl.BlockSpec((1, ff, tn), lambda n, d, be, nv: (be[d], 0, n))],
            out_specs=pl.BlockSpec((BLK, tn), lambda n, d, be, nv: (d, n)),
            scratch_shapes=[pltpu.VMEM((ff, tn), BF16)]),
        compiler_params=_params(("arbitrary", "arbitrary"), V7X_VMEM_LIMIT),
        name="moe_down",
    )(blk_e, nvalid, hh, gsort, wd)


def _combine_kernel(rs_ref, rc_ref, xp_ref, xs_ref, pos_ref, g_ref, y_hbm, op_ref, os_ref, wbuf, obuf, acc_ref,
                    sem, osem, *, n_prompt_chunks):
    i = pl.program_id(0)
    nch = pl.num_programs(0)

    def window_start(chunk, e):
        return pl.multiple_of(rs_ref[chunk * N_EXP + e] // BF16_ROWS * BF16_ROWS, BF16_ROWS)

    def copy(chunk, e, slot):
        return pltpu.make_async_copy(y_hbm.at[pl.ds(window_start(chunk, e), CWIN)],
                                     wbuf.at[slot, pl.ds(e * CWIN, CWIN)], sem.at[slot, e])

    @pl.when(i == 0)
    def _():
        for e in range(N_EXP):
            copy(0, e, 0).start()

    slot = i % 2

    @pl.when(i + 1 < nch)
    def _():
        for e in range(N_EXP):
            copy(i + 1, e, 1 - slot).start()

    col = lax.broadcasted_iota(I32, (CHUNK, CWIN), 1)
    hits = []
    for e in range(N_EXP):
        copy(i, e, slot).wait()
        hits.append((pos_ref[:, e:e + 1] - window_start(i, e)) == col)
    onehot = jnp.concatenate(hits, axis=1).astype(BF16)
    acc_ref[...] = jnp.dot(onehot, wbuf[slot], preferred_element_type=F32)

    for e in range(N_EXP):
        tail = window_start(i, e) + CWIN

        @pl.when(rs_ref[i * N_EXP + e] + rc_ref[i * N_EXP + e] > tail)
        def _():
            cp = pltpu.make_async_copy(y_hbm.at[pl.ds(tail, CWIN)], obuf, osem)
            cp.start()
            cp.wait()
            hit = ((pos_ref[:, e:e + 1] - tail) == col).astype(BF16)
            acc_ref[...] += jnp.dot(hit, obuf[...], preferred_element_type=F32)

    x = jnp.where(i < n_prompt_chunks, xp_ref[...], xs_ref[...])
    out = x + _rms(acc_ref[...], g_ref[...])

    @pl.when(i < n_prompt_chunks)
    def _():
        op_ref[...] = out

    @pl.when(i >= n_prompt_chunks)
    def _():
        os_ref[...] = out


def _combine(run_start, run_count, xp, xs, pos, g, ys):
    npc = xp.shape[0] // CHUNK
    nsc = xs.shape[0] // CHUNK
    nch = npc + nsc
    tok8 = pl.BlockSpec((CHUNK, N_EXP), lambda i, *_: (i, 0))
    pspec = pl.BlockSpec((CHUNK, D), lambda i, *_: (jnp.minimum(i, npc - 1), 0))
    sspec = pl.BlockSpec((CHUNK, D), lambda i, *_: (jnp.maximum(i - npc, 0), 0))
    return pl.pallas_call(
        functools.partial(_combine_kernel, n_prompt_chunks=npc),
        out_shape=(jax.ShapeDtypeStruct(xp.shape, F32), jax.ShapeDtypeStruct(xs.shape, F32)),
        grid_spec=pltpu.PrefetchScalarGridSpec(
            num_scalar_prefetch=2,
            grid=(nch,),
            in_specs=[pspec, sspec, tok8, pl.BlockSpec((1, D), lambda i, *_: (0, 0)),
                      pl.BlockSpec(memory_space=pl.ANY)],
            out_specs=(pspec, sspec),
            scratch_shapes=[pltpu.VMEM((2, N_EXP * CWIN, D), BF16), pltpu.VMEM((CWIN, D), BF16),
                            pltpu.VMEM((CHUNK, D), F32),
                            pltpu.SemaphoreType.DMA((2, N_EXP)), pltpu.SemaphoreType.DMA]),
        compiler_params=_params(("arbitrary",), V7X_VMEM_LIMIT),
        name="moe_combine",
    )(run_start, run_count, xp, xs, pos, g, ys)


def _moe(xp, xs, g4, g5, w_router, wg, wu, wd):
    npc = xp.shape[0] // CHUNK
    nch = npc + xs.shape[0] // CHUNK
    t = nch * CHUNK
    h, rk, cw, cnt = _route(xp, xs, g4, w_router)

    after = cnt.reshape(nch, N_EXP).astype(I32)
    before = jnp.concatenate([jnp.zeros((1, N_EXP), I32), after[:-1]], axis=0)
    counts = after[-1]
    gsz = (counts + BLK - 1) // BLK * BLK
    gend = jnp.cumsum(gsz)
    gstart = gend - gsz
    nblk = (2 * t + N_EXP * (BLK - 1)) // BLK + 1 + (2 * CWIN + BLK - 1) // BLK
    bstart = jnp.arange(nblk, dtype=I32) * BLK
    blk_e = jnp.minimum(jnp.sum(gend[None, :] <= bstart[:, None], axis=1), N_EXP - 1).astype(I32)
    valid = bstart < gend[-1]
    r0 = bstart - gstart[blk_e]
    aft_e = after[:, blk_e]
    bef_e = before[:, blk_e]
    c_lo = jnp.where(valid, jnp.sum(aft_e <= r0[None, :], axis=0), 0).astype(I32)
    c_hi = jnp.where(valid, jnp.sum(bef_e < (r0 + BLK)[None, :], axis=0) - 1, -1).astype(I32)
    c_lo = jnp.minimum(c_lo, nch - 1)
    pos = jnp.where(rk >= 0, rk + gstart[None, :], -1).astype(I32)
    pos_t = pos.T.reshape(N_EXP, nch, 1, CHUNK)
    cw_t = cw.T.reshape(N_EXP, nch, 1, CHUNK)
    run_start = (gstart[None, :] + before).astype(I32).reshape(-1)
    run_count = (after - before).reshape(-1)
    nvalid = jnp.where(valid, jnp.clip(counts[blk_e] - r0, 0, BLK), 0).astype(I32)
    pcount = jnp.maximum(c_hi - c_lo + 1, 0)
    pend = jnp.cumsum(pcount)
    pbase = pend - pcount
    k = jnp.arange(nblk + 2 * nch * N_EXP, dtype=I32)
    pair_blk = jnp.minimum(jnp.sum(pend[None, :] <= k[:, None], axis=1), nblk - 1)
    pair_c = jnp.clip(c_lo[pair_blk] + k - pbase[pair_blk], 0, nch - 1).astype(I32)
    pair_e = blk_e[pair_blk]

    xsort, gsort = _gather(h, pos_t, cw_t, blk_e, pbase.astype(I32), pcount.astype(I32), r0.astype(I32),
                           before.reshape(-1), after.reshape(-1), pair_c, pair_e, pend[-1:].astype(I32), nblk)
    hh = _expert_up(xsort, blk_e, nvalid, wg, wu, tf=1792)
    ysort = _expert_down(hh, gsort, blk_e, nvalid, wd, tn=D)
    return _combine(run_start, run_count, xp, xs, pos, g5, ysort)


def kernel(x_prompt, x_sample, cache_conv, cache_mem_k, cache_mem_v, state_ssm_re, state_ssm_im, mem_prompt, norm_g, mem_norm_g, w_xq, w_xk, w_xv, w_xo, conv_w_pw1, conv_b_pw1, conv_w_dw, conv_b_dw, conv_ln_g, conv_ln_b, conv_w_pw2, conv_b_pw2, ssm_a_re, ssm_a_im, ssm_log_dt, ssm_b_re, ssm_b_im, ssm_c_re, ssm_c_im, ssm_d, ssm_w_glu, ssm_b_glu, ffn_w_gate, ffn_w_up, ffn_w_down, moe_w_router, moe_w_gate, moe_w_up, moe_w_down):
    nbp, seqp, _ = x_prompt.shape
    nbs, seqs, _ = x_sample.shape
    tp = nbp * seqp
    ts = nbs * seqs
    row = lambda a: a.reshape(1, -1)
    g = lambda i, k: norm_g[i, k].reshape(1, D)

    nl = w_xk.shape[0]
    pk, pv = _mem_kv(mem_prompt, mem_norm_g.reshape(nl, 1, D), w_xk, w_xv)
    p_mem_k = pk.reshape(nl, nbp, N_MEM, N_HEADS, HEAD_DIM)
    p_mem_v = pv.reshape(nl, nbp, N_MEM, N_HEADS, HEAD_DIM)
    ck = cache_mem_k.reshape(nl * nbs, N_MEM, N_HEADS, HEAD_DIM)
    cv = cache_mem_v.reshape(nl * nbs, N_MEM, N_HEADS, HEAD_DIM)

    conv_args = (conv_w_dw[0], row(conv_b_dw[0]), row(conv_ln_g[0]), row(conv_ln_b[0]),
                 conv_w_pw2[0], row(conv_b_pw2[0]), g(0, 1))
    xp = x_prompt.reshape(tp, D)
    xs = x_sample.reshape(ts, D)
    up = _conv_pw1(xp, g(0, 0), conv_w_pw1[0], row(conv_b_pw1[0]), tm=512).reshape(nbp, seqp, D)
    us = _conv_pw1(xs, g(0, 0), conv_w_pw1[0], row(conv_b_pw1[0]), tm=ts).reshape(nbs, seqs, D)
    hist_p = jnp.zeros((nbp, HIST, D), F32)
    hist_s = jnp.pad(cache_conv[0], ((0, 0), (HIST - CONV_W + 1, 0), (0, 0)))
    xp = _conv_dw_pw2(up, x_prompt, hist_p, *conv_args, tl=256).reshape(tp, D)
    xs = _conv_dw_pw2(us, x_sample, hist_s, *conv_args, tl=seqs).reshape(ts, D)
    p_conv = up[:, seqp - (CONV_W - 1):][None]
    s_conv = jnp.concatenate([cache_conv[0], us], axis=1)[:, -(CONV_W - 1):][None]

    tma = 512
    xp = _attn(xp, pk, pv, 0, w_xq[0], w_xo[0], g(0, 2), g(0, 3), nbp, seqp, tma)
    xs = _attn(xs, ck, cv, 0, w_xq[0], w_xo[0], g(0, 2), g(0, 3), nbs, seqs, seqs)

    ffn_w = (_to_bf16(ffn_w_gate[0], 4), _to_bf16(ffn_w_up[0], 4), _to_bf16(ffn_w_down[0], 4))
    tff = ffn_w_gate.shape[2] // 2
    xp = _ffn(xp, g(0, 4), g(0, 5), *ffn_w, tm=512, tf=tff, out_shape=(tp, D), out_index=lambda i, f: (i, 0))
    xs = _ffn(xs, g(0, 4), g(0, 5), *ffn_w, tm=ts, tf=tff, out_shape=(ts, D), out_index=lambda i, f: (i, 0))

    ab_r, ab_i, bb_r, bb_i = _ssm_prep(ssm_a_re[0], ssm_a_im[0], ssm_log_dt[0], ssm_b_re[0], ssm_b_im[0])
    bm_r = _block_diag(bb_r.transpose(0, 2, 1)).astype(BF16)
    bm_i = _block_diag(bb_i.transpose(0, 2, 1)).astype(BF16)
    cm_r = _block_diag(ssm_c_re[0].transpose(0, 2, 1)).astype(BF16)
    cm_i = _block_diag(-ssm_c_im[0].transpose(0, 2, 1)).astype(BF16)
    ssm_args = (ab_r, ab_i, bm_r, bm_i, cm_r, cm_i, row(ssm_d[0]), ssm_w_glu[0], row(ssm_b_glu[0]))
    zero_state = jnp.zeros((SUBLANES, SSM_N), F32)
    xp, p_sr, p_si = _ssm(xp.reshape(nbp, seqp, D), g(1, 0), g(1, 1), zero_state, zero_state, *ssm_args, tl=64)
    xs, s_sr, s_si = _ssm(xs.reshape(nbs, seqs, D), g(1, 0), g(1, 1), state_ssm_re[0].reshape(nbs, SSM_N),
                          state_ssm_im[0].reshape(nbs, SSM_N), *ssm_args, tl=seqs)
    st = lambda a, n: a[:n].reshape(1, n, SSM_G, SSM_P)
    p_ssm_re, p_ssm_im = st(p_sr, nbp), st(p_si, nbp)
    s_ssm_re, s_ssm_im = st(s_sr, nbs), st(s_si, nbs)

    xp = _attn(xp.reshape(tp, D), pk, pv, nbp, w_xq[1], w_xo[1], g(1, 2), g(1, 3), nbp, seqp, tma)
    xs = _attn(xs.reshape(ts, D), ck, cv, nbs, w_xq[1], w_xo[1], g(1, 2), g(1, 3), nbs, seqs, seqs)

    yp, ysm = _moe(xp, xs, g(1, 4), g(1, 5), moe_w_router[0], moe_w_gate[0], moe_w_up[0], moe_w_down[0])
    return (yp.reshape(nbp, seqp, D), ysm.reshape(nbs, seqs, D), p_conv, p_mem_k, p_mem_v,
            p_ssm_re, p_ssm_im, s_conv, s_ssm_re, s_ssm_im)
```

```python
import functools
import math

import jax
import jax.numpy as jnp
from jax import lax
from jax.experimental import pallas as pl
from jax.experimental.pallas import tpu as pltpu

F32 = jnp.float32
BF16 = jnp.bfloat16
I32 = jnp.int32

D = 1024
CONV_W = 31
HIST = 32
N_MEM = 256
N_HEADS = 4
HEAD_DIM = D // N_HEADS
SSM_G = 64
SSM_C = 16
SSM_P = 64
SSM_N = SSM_G * SSM_P
SSM_CB = 256
SSM_NCB = D // SSM_CB
SSM_SB = SSM_CB // SSM_C * SSM_P
N_EXP = 8
EPS = 1e-6

V7X_VMEM_LIMIT = 56 * 1024 * 1024
SUBLANES = 8

CHUNK = 256
BLK = 512
WIN = 512
GWIN = 128
GSLOTS = 4
CWIN = 256
BF16_ROWS = 16


def _params(sem, vmem=None):
    return pltpu.CompilerParams(dimension_semantics=sem, vmem_limit_bytes=vmem)


def _rms(x, g):
    return x * lax.rsqrt(jnp.mean(x * x, axis=-1, keepdims=True) + EPS) * g


def _const_spec(shape):
    nd = len(shape)
    return pl.BlockSpec(shape, lambda *_: (0,) * nd)


def _pw1_kernel(x_ref, g_ref, w_ref, b_ref, u_ref, wbf_ref):
    @pl.when(pl.program_id(0) == 0)
    def _():
        wbf_ref[...] = w_ref[...].astype(BF16)

    h = _rms(x_ref[...], g_ref[...]).astype(BF16)
    z = jnp.dot(h, wbf_ref[...], preferred_element_type=F32) + b_ref[...]
    u_ref[...] = z[:, :D] * jax.nn.sigmoid(z[:, D:])


def _conv_pw1(x, g, w, b, tm):
    t = x.shape[0]
    return pl.pallas_call(
        _pw1_kernel,
        out_shape=jax.ShapeDtypeStruct((t, D), F32),
        grid=(t // tm,),
        in_specs=[pl.BlockSpec((tm, D), lambda i: (i, 0)),
                  _const_spec((1, D)), _const_spec((D, 2 * D)), _const_spec((1, 2 * D))],
        out_specs=pl.BlockSpec((tm, D), lambda i: (i, 0)),
        scratch_shapes=[pltpu.VMEM((D, 2 * D), BF16)],
        compiler_params=_params(("arbitrary",), V7X_VMEM_LIMIT),
        name="conv_pw1",
    )(x, g, w, b)


def _conv2_kernel(u_ref, x_ref, hist_ref, wdw_ref, bdw_ref, lng_ref, lnb_ref, w2_ref, b2_ref,
                  g_ref, o_ref, ext_ref, sh_ref, conv_ref, w2bf_ref, wtap_ref, *, tl, rt):
    bi = pl.program_id(0)
    li = pl.program_id(1)

    @pl.when((bi == 0) & (li == 0))
    def _():
        w2bf_ref[...] = w2_ref[...].astype(BF16)
        for k in range(CONV_W):
            wtap_ref[k] = jnp.broadcast_to(wdw_ref[k:k + 1, :], (SUBLANES, D))

    @pl.when(li == 0)
    def _():
        ext_ref[0:HIST, :] = hist_ref[0]

    @pl.when(li > 0)
    def _():
        ext_ref[0:HIST, :] = ext_ref[tl:tl + HIST, :]

    ext_ref[HIST:HIST + tl, :] = u_ref[0]
    for s in range(1, SUBLANES):
        sh_ref[s - 1] = ext_ref[pl.ds(s, tl + HIST - SUBLANES), :]

    def rows(i, carry):
        r0 = pl.multiple_of(i * rt, rt)
        ntile = rt // SUBLANES
        accs = [jnp.zeros((SUBLANES, D), F32) + bdw_ref[...] for _ in range(ntile)]
        for k in range(CONV_W):
            off = HIST - CONV_W + 1 + k
            s = off % SUBLANES
            base = off - s
            w = wtap_ref[k]
            for ti in range(ntile):
                rows_ti = pl.ds(r0 + base + ti * SUBLANES, SUBLANES)
                src = ext_ref[rows_ti, :] if s == 0 else sh_ref[s - 1, rows_ti, :]
                accs[ti] = accs[ti] + w * src
        for ti in range(ntile):
            conv_ref[pl.ds(r0 + ti * SUBLANES, SUBLANES), :] = accs[ti]
        return carry

    lax.fori_loop(0, tl // rt, rows, 0)
    acc = conv_ref[...]
    mu = jnp.mean(acc, axis=-1, keepdims=True)
    xc = acc - mu
    var = jnp.mean(xc * xc, axis=-1, keepdims=True)
    y = xc * lax.rsqrt(var + EPS) * lng_ref[...] + lnb_ref[...]
    y = y * jax.nn.sigmoid(y)
    t = jnp.dot(y.astype(BF16), w2bf_ref[...], preferred_element_type=F32) + b2_ref[...]
    o_ref[0] = x_ref[0] + _rms(t, g_ref[...])


def _conv_dw_pw2(u, x, hist, wdw, bdw, lng, lnb, w2, b2, g, tl):
    nb, seq, _ = u.shape
    tok = pl.BlockSpec((1, tl, D), lambda b, l: (b, l, 0))
    return pl.pallas_call(
        functools.partial(_conv2_kernel, tl=tl, rt=min(tl, 32)),
        out_shape=jax.ShapeDtypeStruct((nb, seq, D), F32),
        grid=(nb, seq // tl),
        in_specs=[tok, tok, pl.BlockSpec((1, HIST, D), lambda b, l: (b, 0, 0)),
                  _const_spec((CONV_W, D)), _const_spec((1, D)), _const_spec((1, D)), _const_spec((1, D)),
                  _const_spec((D, D)), _const_spec((1, D)), _const_spec((1, D))],
        out_specs=tok,
        scratch_shapes=[pltpu.VMEM((tl + HIST, D), F32),
                        pltpu.VMEM((SUBLANES - 1, tl + HIST - SUBLANES, D), F32),
                        pltpu.VMEM((tl, D), F32), pltpu.VMEM((D, D), BF16),
                        pltpu.VMEM((CONV_W, SUBLANES, D), F32)],
        compiler_params=_params(("arbitrary", "arbitrary"), V7X_VMEM_LIMIT),
        name="conv_dw_pw2",
    )(u, x, hist, wdw, bdw, lng, lnb, w2, b2, g)


def _memkv_kernel(m_ref, g_ref, wk_ref, wv_ref, k_ref, v_ref, wkbf_ref, wvbf_ref):
    @pl.when(pl.program_id(1) == 0)
    def _():
        wkbf_ref[...] = wk_ref[0].astype(BF16)
        wvbf_ref[...] = wv_ref[0].astype(BF16)

    m = _rms(m_ref[0], g_ref[0]).astype(BF16)
    k = jnp.dot(m, wkbf_ref[...], preferred_element_type=F32)
    v = jnp.dot(m, wvbf_ref[...], preferred_element_type=F32)
    for hd in range(N_HEADS):
        sl = slice(hd * HEAD_DIM, (hd + 1) * HEAD_DIM)
        k_ref[0, :, hd, :] = k[:, sl]
        v_ref[0, :, hd, :] = v[:, sl]


def _mem_kv(mem, g, wk, wv):
    nb = mem.shape[0]
    nl = wk.shape[0]
    kv = jax.ShapeDtypeStruct((nl * nb, N_MEM, N_HEADS, HEAD_DIM), F32)
    out = pl.BlockSpec((1, N_MEM, N_HEADS, HEAD_DIM), lambda l, b: (l * nb + b, 0, 0, 0))
    w = pl.BlockSpec((1, D, D), lambda l, b: (l, 0, 0))
    return pl.pallas_call(
        _memkv_kernel,
        out_shape=(kv, kv),
        grid=(nl, nb),
        in_specs=[pl.BlockSpec((1, N_MEM, D), lambda l, b: (b, 0, 0)),
                  pl.BlockSpec((1, 1, D), lambda l, b: (l, 0, 0)), w, w],
        out_specs=(out, out),
        scratch_shapes=[pltpu.VMEM((D, D), BF16), pltpu.VMEM((D, D), BF16)],
        compiler_params=_params(("arbitrary", "arbitrary"), V7X_VMEM_LIMIT),
        name="mem_kv",
    )(mem, g, wk, wv)


def _attn_kernel(x_ref, k_ref, v_ref, wq_ref, wo_ref, g2_ref, g3_ref, o_ref, wqbf_ref, wobf_ref, kb_ref, vb_ref):
    @pl.when((pl.program_id(0) == 0) & (pl.program_id(1) == 0))
    def _():
        wqbf_ref[...] = wq_ref[...].astype(BF16)
        wobf_ref[...] = wo_ref[...].astype(BF16)

    @pl.when(pl.program_id(1) == 0)
    def _():
        for hd in range(N_HEADS):
            sl = slice(hd * HEAD_DIM, (hd + 1) * HEAD_DIM)
            kb_ref[:, sl] = k_ref[0, :, hd, :].astype(BF16)
            vb_ref[:, sl] = v_ref[0, :, hd, :].astype(BF16)

    x = x_ref[...]
    h = _rms(x, g2_ref[...]).astype(BF16)
    q = (jnp.dot(h, wqbf_ref[...], preferred_element_type=F32) * (HEAD_DIM ** -0.5)).astype(BF16)
    heads = []
    for hd in range(N_HEADS):
        sl = slice(hd * HEAD_DIM, (hd + 1) * HEAD_DIM)
        s = lax.dot_general(q[:, sl], kb_ref[:, sl], (((1,), (1,)), ((), ())), preferred_element_type=F32)
        p = jnp.exp(s - jnp.max(s, axis=-1, keepdims=True))
        p = p / jnp.sum(p, axis=-1, keepdims=True)
        heads.append(jnp.dot(p.astype(BF16), vb_ref[:, sl], preferred_element_type=F32))
    o = jnp.concatenate(heads, axis=1).astype(BF16)
    t = jnp.dot(o, wobf_ref[...], preferred_element_type=F32)
    o_ref[...] = x + _rms(t, g3_ref[...])


def _attn(x, k, v, kv_base, wq, wo, g2, g3, nb, seq, tm):
    tok = pl.BlockSpec((tm, D), lambda b, l: (b * (seq // tm) + l, 0))
    kv = pl.BlockSpec((1, N_MEM, N_HEADS, HEAD_DIM), lambda b, l: (kv_base + b, 0, 0, 0))
    return pl.pallas_call(
        _attn_kernel,
        out_shape=jax.ShapeDtypeStruct((nb * seq, D), F32),
        grid=(nb, seq // tm),
        in_specs=[tok, kv, kv, _const_spec((D, D)), _const_spec((D, D)), _const_spec((1, D)), _const_spec((1, D))],
        out_specs=tok,
        scratch_shapes=[pltpu.VMEM((D, D), BF16), pltpu.VMEM((D, D), BF16),
                        pltpu.VMEM((N_MEM, D), BF16), pltpu.VMEM((N_MEM, D), BF16)],
        compiler_params=_params(("arbitrary", "arbitrary"), V7X_VMEM_LIMIT),
        name="mem_attn",
    )(x, k, v, wq, wo, g2, g3)


def _cast_kernel(w_ref, o_ref):
    o_ref[...] = w_ref[...].astype(BF16)


def _to_bf16(w, nsplit):
    r, c = w.shape
    blk = pl.BlockSpec((r // nsplit, c), lambda i: (i, 0))
    return pl.pallas_call(
        _cast_kernel,
        out_shape=jax.ShapeDtypeStruct((r, c), BF16),
        grid=(nsplit,),
        in_specs=[blk],
        out_specs=blk,
        compiler_params=_params(("arbitrary",)),
        name="to_bf16",
    )(w)


def _ffn_kernel(x_ref, g4_ref, g5_ref, wg_ref, wu_ref, wd_ref, o_ref, h_ref, acc_ref):
    f = pl.program_id(1)

    @pl.when(f == 0)
    def _():
        h_ref[...] = _rms(x_ref[...], g4_ref[...]).astype(BF16)
        acc_ref[...] = jnp.zeros_like(acc_ref)

    h = h_ref[...]
    a = jnp.dot(h, wg_ref[...], preferred_element_type=F32)
    b = jnp.dot(h, wu_ref[...], preferred_element_type=F32)
    hh = (a * jax.nn.sigmoid(a) * b).astype(BF16)
    acc_ref[...] += jnp.dot(hh, wd_ref[...], preferred_element_type=F32)

    @pl.when(f == pl.num_programs(1) - 1)
    def _():
        o_ref[...] = x_ref[...] + _rms(acc_ref[...], g5_ref[...])


def _ffn(x, g4, g5, wg, wu, wd, tm, tf, out_shape, out_index):
    t = x.shape[0]
    ff = wg.shape[1]
    return pl.pallas_call(
        _ffn_kernel,
        out_shape=jax.ShapeDtypeStruct(out_shape, F32),
        grid=(t // tm, ff // tf),
        in_specs=[pl.BlockSpec((tm, D), lambda i, f: (i, 0)),
                  _const_spec((1, D)), _const_spec((1, D)),
                  pl.BlockSpec((D, tf), lambda i, f: (0, f)),
                  pl.BlockSpec((D, tf), lambda i, f: (0, f)),
                  pl.BlockSpec((tf, D), lambda i, f: (f, 0))],
        out_specs=pl.BlockSpec((tm, D), out_index),
        scratch_shapes=[pltpu.VMEM((tm, D), BF16), pltpu.VMEM((tm, D), F32)],
        compiler_params=_params(("arbitrary", "arbitrary"), V7X_VMEM_LIMIT),
        name="dense_ffn",
    )(x, g4, g5, wg, wu, wd)


def _ssm_prep_kernel(lr_ref, li_ref, ldt_ref, br_ref, bi_ref, abr_ref, abi_ref, bbr_ref, bbi_ref,
                     a2r_ref, a2i_ref, abbr_ref, abbi_ref):
    dt = jnp.exp(ldt_ref[...])
    lr = lr_ref[...]
    li = li_ref[...]
    mag = jnp.exp(lr * dt)
    ab_r = mag * jnp.cos(li * dt)
    ab_i = mag * jnp.sin(li * dt)
    den = lr * lr + li * li
    nr = ab_r - 1.0
    k_r = (nr * lr + ab_i * li) / den
    k_i = (ab_i * lr - nr * li) / den
    br = br_ref[...]
    bi = bi_ref[...]
    bb_r = k_r * br - k_i * bi
    bb_i = k_r * bi + k_i * br
    abr_ref[...] = ab_r
    abi_ref[...] = ab_i
    bbr_ref[...] = bb_r
    bbi_ref[...] = bb_i
    a2r_ref[...] = ab_r * ab_r - ab_i * ab_i
    a2i_ref[...] = 2.0 * (ab_r * ab_i)
    abbr_ref[...] = ab_r * bb_r - ab_i * bb_i
    abbi_ref[...] = ab_r * bb_i + ab_i * bb_r


def _ssm_prep(a_re, a_im, log_dt, b_re, b_im):
    n = SSM_P * SSM_C
    rep = lambda a: jnp.repeat(a, SSM_C, axis=1)
    shp = jax.ShapeDtypeStruct((SSM_G, n), F32)
    abr, abi, bbr, bbi, a2r, a2i, abbr, abbi = pl.pallas_call(
        _ssm_prep_kernel,
        out_shape=(shp,) * 8,
        name="ssm_prep",
    )(rep(a_re), rep(a_im), log_dt.reshape(SSM_G, 1), b_re.reshape(SSM_G, n), b_im.reshape(SSM_G, n))
    pick = lambda a: a.reshape(SSM_G, SSM_P, SSM_C)[:, :, 0].reshape(1, SSM_N)
    gpc = lambda a: a.reshape(SSM_G, SSM_P, SSM_C)
    return (pick(abr), pick(abi), pick(a2r), pick(a2i)), (gpc(bbr), gpc(bbi), gpc(abbr), gpc(abbi))


def _block_diag(m):
    gpb = SSM_CB // SSM_C
    _, a, b = m.shape
    m = m.reshape(SSM_NCB, gpb, a, 1, b)
    eye = jnp.eye(gpb, dtype=m.dtype).reshape(1, gpb, 1, gpb, 1)
    return (m * eye).reshape(SSM_NCB, gpb * a, gpb * b)


def _ssm_kernel(x_ref, g0_ref, g1_ref, s0r_ref, s0i_ref, ar_ref, ai_ref, bmr_ref, bmi_ref, cmr_ref, cmi_ref,
                d_ref, wglu_ref, bglu_ref, o_ref, sr_ref, si_ref,
                bur_ref, bui_ref, *, nb, rb, lc):
    @pl.when(pl.program_id(0) == 0)
    def _():
        sr_ref[...] = s0r_ref[...]
        si_ref[...] = s0i_ref[...]

    tl = rb // nb
    x = x_ref[...].reshape(rb, D)
    h = _rms(x, g0_ref[...])
    r = lax.broadcasted_iota(I32, (rb, rb), 0)
    c = lax.broadcasted_iota(I32, (rb, rb), 1)
    to_time_major = (c == (r & (nb - 1)) * tl + (r >> (nb.bit_length() - 1))).astype(BF16)
    to_batch_major = (c == (r & (tl - 1)) * nb + (r >> (tl.bit_length() - 1))).astype(BF16)
    hb = jnp.dot(to_time_major, h.astype(BF16), preferred_element_type=F32).astype(BF16)
    rows = max(nb, SUBLANES)
    pair = nb < SUBLANES
    if pair:
        odd = ((r >> (nb.bit_length() - 1)) & 1) == 1
        prev_time_major = ((c == (r & (nb - 1)) * tl + (r >> (nb.bit_length() - 1)) - 1) & odd).astype(BF16)
        hprev = jnp.dot(prev_time_major, h.astype(BF16), preferred_element_type=F32).astype(BF16)
        low = lax.broadcasted_iota(I32, (SUBLANES, lc), 0) < nb

    def scan_chunk(cs):
        a_r = ar_ref[:, cs]
        a_i = ai_ref[:, cs]
        s_r = sr_ref[:, cs]
        s_i = si_ref[:, cs]
        for j in range(rb // rows):
            rs = slice(j * rows, (j + 1) * rows)
            n_r = a_r * s_r - a_i * s_i + bur_ref[rs, cs]
            n_i = a_r * s_i + a_i * s_r + bui_ref[rs, cs]
            bur_ref[rs, cs] = n_r
            bui_ref[rs, cs] = n_i
            if pair:
                s_r = jnp.where(low, pltpu.roll(n_r, nb, 0), n_r)
                s_i = jnp.where(low, pltpu.roll(n_i, nb, 0), n_i)
            else:
                s_r, s_i = n_r, n_i
        sr_ref[:, cs] = s_r
        si_ref[:, cs] = s_i

    ys = []
    for cb in range(SSM_NCB):
        ss = slice(cb * SSM_SB, (cb + 1) * SSM_SB)
        hs = hb[:, cb * SSM_CB:(cb + 1) * SSM_CB]
        if pair:
            hs = jnp.concatenate([hs, hprev[:, cb * SSM_CB:(cb + 1) * SSM_CB]], axis=1)
        bur_ref[:, ss] = jnp.dot(hs, bmr_ref[cb], preferred_element_type=F32)
        bui_ref[:, ss] = jnp.dot(hs, bmi_ref[cb], preferred_element_type=F32)
        for c in range(SSM_SB // lc):
            scan_chunk(slice(cb * SSM_SB + c * lc, cb * SSM_SB + (c + 1) * lc))
        ys.append(jnp.dot(bur_ref[:, ss].astype(BF16), cmr_ref[cb], preferred_element_type=F32)
                  + jnp.dot(bui_ref[:, ss].astype(BF16), cmi_ref[cb], preferred_element_type=F32))
    y_tm = jnp.concatenate(ys, axis=1)
    y1 = y_tm.astype(BF16)
    rem = y_tm - y1.astype(F32)
    y2 = rem.astype(BF16)
    y3 = (rem - y2.astype(F32)).astype(BF16)
    y = (jnp.dot(to_batch_major, y1, preferred_element_type=F32)
         + jnp.dot(to_batch_major, y2, preferred_element_type=F32)
         + jnp.dot(to_batch_major, y3, preferred_element_type=F32))
    y = y + d_ref[...] * h
    y = jax.nn.gelu(y).astype(BF16)
    z = jnp.dot(y, wglu_ref[...], preferred_element_type=F32) + bglu_ref[...]
    t = z[:, :D] * jax.nn.sigmoid(z[:, D:])
    o_ref[...] = (x + _rms(t, g1_ref[...])).reshape(nb, tl, D)


def _ssm(x, g0, g1, s0r, s0i, ab_r, ab_i, bm_r, bm_i, cm_r, cm_i, d, wglu, bglu, tl):
    nb, seq, _ = x.shape
    rb = nb * tl
    rows = max(nb, SUBLANES)
    kb = bm_r.shape[1]
    st = jax.ShapeDtypeStruct((rows, SSM_N), F32)
    row = pl.BlockSpec((nb, tl, D), lambda i: (0, i, 0))
    return pl.pallas_call(
        functools.partial(_ssm_kernel, nb=nb, rb=rb, lc=512),
        out_shape=(jax.ShapeDtypeStruct((nb, seq, D), F32), st, st),
        grid=(seq // tl,),
        in_specs=[row, _const_spec((1, D)), _const_spec((1, D)),
                  _const_spec((rows, SSM_N)), _const_spec((rows, SSM_N)),
                  _const_spec((rows, SSM_N)), _const_spec((rows, SSM_N)),
                  _const_spec((SSM_NCB, kb, SSM_SB)), _const_spec((SSM_NCB, kb, SSM_SB)),
                  _const_spec((SSM_NCB, SSM_SB, SSM_CB)), _const_spec((SSM_NCB, SSM_SB, SSM_CB)),
                  _const_spec((1, D)), _const_spec((D, 2 * D)), _const_spec((1, 2 * D))],
        out_specs=(row, _const_spec((rows, SSM_N)), _const_spec((rows, SSM_N))),
        scratch_shapes=[pltpu.VMEM((rb, SSM_N), F32), pltpu.VMEM((rb, SSM_N), F32)],
        compiler_params=_params(("arbitrary",), V7X_VMEM_LIMIT),
        name="ssm",
    )(x, g0, g1, s0r, s0i, ab_r, ab_i, bm_r, bm_i, cm_r, cm_i, d, wglu, bglu)


def _route_kernel(xp_ref, xs_ref, g_ref, wr_ref, h_ref, rk_ref, cw_ref, cnt_ref, carry_ref, *, n_prompt_chunks):
    i = pl.program_id(0)

    @pl.when(i == 0)
    def _():
        carry_ref[...] = jnp.zeros_like(carry_ref)

    x = jnp.where(i < n_prompt_chunks, xp_ref[...], xs_ref[...])
    h = _rms(x, g_ref[...])
    h_ref[...] = h.astype(BF16)
    lg = jnp.dot(h, wr_ref[...], preferred_element_type=F32, precision=lax.Precision.HIGHEST)
    lane = lax.broadcasted_iota(I32, lg.shape, 1)
    m1 = jnp.max(lg, axis=-1, keepdims=True)
    i1 = jnp.min(jnp.where(lg == m1, lane, N_EXP), axis=-1, keepdims=True)
    first = lane == i1
    lg2 = jnp.where(first, -jnp.inf, lg)
    m2 = jnp.max(lg2, axis=-1, keepdims=True)
    i2 = jnp.min(jnp.where(lg2 == m2, lane, N_EXP), axis=-1, keepdims=True)
    second = lane == i2
    e = jnp.exp(m2 - m1)
    den = 1.0 + e
    cw_ref[...] = jnp.where(first, 1.0 / den, 0.0) + jnp.where(second, e / den, 0.0)
    assigned = first | second
    a = assigned.astype(BF16)
    r = lax.broadcasted_iota(I32, (CHUNK, CHUNK), 0)
    c = lax.broadcasted_iota(I32, (CHUNK, CHUNK), 1)
    before = (c < r).astype(BF16)
    rank = jnp.dot(before, a, preferred_element_type=F32) + carry_ref[...]
    rk_ref[...] = jnp.where(assigned, rank.astype(I32), -1)
    carry_ref[...] += jnp.sum(assigned.astype(F32), axis=0, keepdims=True)
    cnt_ref[0] = carry_ref[...]


def _route(xp, xs, g, wr):
    npc = xp.shape[0] // CHUNK
    nch = npc + xs.shape[0] // CHUNK
    t = nch * CHUNK
    tok8 = pl.BlockSpec((CHUNK, N_EXP), lambda i: (i, 0))
    return pl.pallas_call(
        functools.partial(_route_kernel, n_prompt_chunks=npc),
        out_shape=(jax.ShapeDtypeStruct((t, D), BF16), jax.ShapeDtypeStruct((t, N_EXP), I32),
                   jax.ShapeDtypeStruct((t, N_EXP), F32), jax.ShapeDtypeStruct((nch, 1, N_EXP), F32)),
        grid=(nch,),
        in_specs=[pl.BlockSpec((CHUNK, D), lambda i: (jnp.minimum(i, npc - 1), 0)),
                  pl.BlockSpec((CHUNK, D), lambda i: (jnp.maximum(i - npc, 0), 0)),
                  _const_spec((1, D)), _const_spec((D, N_EXP))],
        out_specs=(pl.BlockSpec((CHUNK, D), lambda i: (i, 0)), tok8, tok8,
                   pl.BlockSpec((1, 1, N_EXP), lambda i: (i, 0, 0))),
        scratch_shapes=[pltpu.VMEM((1, N_EXP), F32)],
        compiler_params=_params(("arbitrary",)),
        name="moe_route",
    )(xp, xs, g, wr)


def _gather_kernel(be_ref, pb_ref, pn_ref, r0_ref, bef_ref, aft_ref, pc_ref, pe_ref, pt_ref,
                   h_hbm, pos_hbm, cw_hbm, o_ref, gs_ref, hbuf, pbuf, cbuf, acc_ref, gacc_ref, sem):
    d = pl.program_id(0)
    e = be_ref[d]
    base = pb_ref[d]
    r0 = r0_ref[d]
    total = pt_ref[0]

    def copies(idx):
        slot = idx % GSLOTS
        c = pc_ref[idx]
        ee = pe_ref[idx]
        return (pltpu.make_async_copy(h_hbm.at[pl.ds(c * CHUNK, CHUNK)], hbuf.at[slot], sem.at[0, slot]),
                pltpu.make_async_copy(pos_hbm.at[ee, c], pbuf.at[slot], sem.at[1, slot]),
                pltpu.make_async_copy(cw_hbm.at[ee, c], cbuf.at[slot], sem.at[2, slot]))

    def start(idx):
        @pl.when(idx < total)
        def _():
            for cp in copies(idx):
                cp.start()

    @pl.when(d == 0)
    def _():
        for k in range(GSLOTS - 1):
            start(k)

    acc_ref[...] = jnp.zeros_like(acc_ref)
    gacc_ref[...] = jnp.zeros_like(gacc_ref)
    row = d * BLK + lax.broadcasted_iota(I32, (GWIN, CHUNK), 0)

    def body(j, carry):
        idx = base + j
        slot = idx % GSLOTS
        for cp in copies(idx):
            cp.wait()
        start(idx + GSLOTS - 1)

        c = pc_ref[idx]
        first = jnp.clip(bef_ref[c * N_EXP + e] - r0, 0, BLK)
        last = jnp.clip(aft_ref[c * N_EXP + e] - r0, 0, BLK)

        def window(w, c2):
            w0 = pl.multiple_of(w * GWIN, GWIN)
            hit = pbuf[slot] == row + w0
            acc_ref[pl.ds(w0, GWIN), :] += jnp.dot(hit.astype(BF16), hbuf[slot], preferred_element_type=F32)
            gacc_ref[pl.ds(w0, GWIN), :] += jnp.sum(jnp.where(hit, cbuf[slot], 0.0), axis=1, keepdims=True)
            return c2

        lax.fori_loop(first // GWIN, (last + GWIN - 1) // GWIN, window, 0)
        return carry

    lax.fori_loop(0, pn_ref[d], body, 0)
    o_ref[...] = acc_ref[...].astype(BF16)
    gs_ref[...] = gacc_ref[...]


def _gather(h, pos_t, cw_t, blk_e, pbase, pcount, r0, before, after, pair_c, pair_e, ptotal, nblk):
    hbm = pl.BlockSpec(memory_space=pl.ANY)
    return pl.pallas_call(
        _gather_kernel,
        out_shape=(jax.ShapeDtypeStruct((nblk * BLK, D), BF16), jax.ShapeDtypeStruct((nblk * BLK, 1), F32)),
        grid_spec=pltpu.PrefetchScalarGridSpec(
            num_scalar_prefetch=9,
            grid=(nblk,),
            in_specs=[hbm, hbm, hbm],
            out_specs=(pl.BlockSpec((BLK, D), lambda d, *_: (d, 0)), pl.BlockSpec((BLK, 1), lambda d, *_: (d, 0))),
            scratch_shapes=[pltpu.VMEM((GSLOTS, CHUNK, D), BF16), pltpu.VMEM((GSLOTS, 1, CHUNK), I32),
                            pltpu.VMEM((GSLOTS, 1, CHUNK), F32), pltpu.VMEM((BLK, D), F32),
                            pltpu.VMEM((BLK, 1), F32), pltpu.SemaphoreType.DMA((3, GSLOTS))]),
        compiler_params=_params(("arbitrary",)),
        name="moe_gather",
    )(blk_e, pbase, pcount, r0, before, after, pair_c, pair_e, ptotal, h, pos_t, cw_t)


def _expert_up_kernel(be_ref, nv_ref, x_ref, wg_ref, wu_ref, o_ref, wgbf_ref, wubf_ref):
    d = pl.program_id(1)
    changed = (d == 0) | (be_ref[d] != be_ref[jnp.maximum(d - 1, 0)])

    @pl.when(changed)
    def _():
        wgbf_ref[...] = wg_ref[0].astype(BF16)
        wubf_ref[...] = wu_ref[0].astype(BF16)

    def up(rows):
        x = x_ref[rows, :]
        a = jnp.dot(x, wgbf_ref[...], preferred_element_type=F32)
        b = jnp.dot(x, wubf_ref[...], preferred_element_type=F32)
        o_ref[rows, :] = (a * jax.nn.sigmoid(a) * b).astype(BF16)

    half = BLK // 2
    nrows = nv_ref[d]

    @pl.when(nrows > half)
    def _():
        up(slice(0, BLK))

    @pl.when((nrows > 0) & (nrows <= half))
    def _():
        up(slice(0, half))
        o_ref[half:, :] = jnp.zeros((BLK - half, o_ref.shape[1]), BF16)

    @pl.when(nrows == 0)
    def _():
        o_ref[...] = jnp.zeros_like(o_ref)


def _expert_up(xs, blk_e, nvalid, wg, wu, tf):
    p = xs.shape[0]
    ff = wg.shape[2]
    return pl.pallas_call(
        _expert_up_kernel,
        out_shape=jax.ShapeDtypeStruct((p, ff), BF16),
        grid_spec=pltpu.PrefetchScalarGridSpec(
            num_scalar_prefetch=2,
            grid=(ff // tf, p // BLK),
            in_specs=[pl.BlockSpec((BLK, D), lambda f, d, be, nv: (d, 0)),
                      pl.BlockSpec((1, D, tf), lambda f, d, be, nv: (be[d], 0, f)),
                      pl.BlockSpec((1, D, tf), lambda f, d, be, nv: (be[d], 0, f))],
            out_specs=pl.BlockSpec((BLK, tf), lambda f, d, be, nv: (d, f)),
            scratch_shapes=[pltpu.VMEM((D, tf), BF16), pltpu.VMEM((D, tf), BF16)]),
        compiler_params=_params(("arbitrary", "arbitrary"), V7X_VMEM_LIMIT),
        name="moe_up",
    )(blk_e, nvalid, xs, wg, wu)


def _expert_down_kernel(be_ref, nv_ref, h_ref, gs_ref, wd_ref, o_ref, wdbf_ref):
    d = pl.program_id(1)
    changed = (d == 0) | (be_ref[d] != be_ref[jnp.maximum(d - 1, 0)])

    @pl.when(changed)
    def _():
        wdbf_ref[...] = wd_ref[0].astype(BF16)

    def down(rows):
        y = jnp.dot(h_ref[rows, :], wdbf_ref[...], preferred_element_type=F32)
        o_ref[rows, :] = (y * gs_ref[rows, :]).astype(BF16)

    half = BLK // 2
    nrows = nv_ref[d]

    @pl.when(nrows > half)
    def _():
        down(slice(0, BLK))

    @pl.when((nrows > 0) & (nrows <= half))
    def _():
        down(slice(0, half))
        o_ref[half:, :] = jnp.zeros((BLK - half, o_ref.shape[1]), BF16)

    @pl.when(nrows == 0)
    def _():
        o_ref[...] = jnp.zeros_like(o_ref)


def _expert_down(hh, gsort, blk_e, nvalid, wd, tn):
    p, ff = hh.shape
    return pl.pallas_call(
        _expert_down_kernel,
        out_shape=jax.ShapeDtypeStruct((p, D), BF16),
        grid_spec=pltpu.PrefetchScalarGridSpec(
            num_scalar_prefetch=2,
            grid=(D // tn, p // BLK),
            in_specs=[pl.BlockSpec((BLK, ff), lambda n, d, be, nv: (d, 0)),
                      pl.BlockSpec((BLK, 1), lambda n, d, be, nv: (d, 0)),
                      pl.BlockSpec((1, ff, tn), lambda n, d, be, nv: (be[d], 0, n))],
            out_specs=pl.BlockSpec((BLK, tn), lambda n, d, be, nv: (d, n)),
            scratch_shapes=[pltpu.VMEM((ff, tn), BF16)]),
        compiler_params=_params(("arbitrary", "arbitrary"), V7X_VMEM_LIMIT),
        name="moe_down",
    )(blk_e, nvalid, hh, gsort, wd)


def _combine_kernel(rs_ref, rc_ref, xp_ref, xs_ref, pos_ref, g_ref, y_hbm, op_ref, os_ref, wbuf, obuf, acc_ref,
                    sem, osem, *, n_prompt_chunks):
    i = pl.program_id(0)
    nch = pl.num_programs(0)

    def window_start(chunk, e):
        return pl.multiple_of(rs_ref[chunk * N_EXP + e] // BF16_ROWS * BF16_ROWS, BF16_ROWS)

    def copy(chunk, e, slot):
        return pltpu.make_async_copy(y_hbm.at[pl.ds(window_start(chunk, e), CWIN)],
                                     wbuf.at[slot, pl.ds(e * CWIN, CWIN)], sem.at[slot, e])

    @pl.when(i == 0)
    def _():
        for e in range(N_EXP):
            copy(0, e, 0).start()

    slot = i % 2

    @pl.when(i + 1 < nch)
    def _():
        for e in range(N_EXP):
            copy(i + 1, e, 1 - slot).start()

    col = lax.broadcasted_iota(I32, (CHUNK, CWIN), 1)
    hits = []
    for e in range(N_EXP):
        copy(i, e, slot).wait()
        hits.append((pos_ref[:, e:e + 1] - window_start(i, e)) == col)
    onehot = jnp.concatenate(hits, axis=1).astype(BF16)
    acc_ref[...] = jnp.dot(onehot, wbuf[slot], preferred_element_type=F32)

    for e in range(N_EXP):
        tail = window_start(i, e) + CWIN

        @pl.when(rs_ref[i * N_EXP + e] + rc_ref[i * N_EXP + e] > tail)
        def _():
            cp = pltpu.make_async_copy(y_hbm.at[pl.ds(tail, CWIN)], obuf, osem)
            cp.start()
            cp.wait()
            hit = ((pos_ref[:, e:e + 1] - tail) == col).astype(BF16)
            acc_ref[...] += jnp.dot(hit, obuf[...], preferred_element_type=F32)

    x = jnp.where(i < n_prompt_chunks, xp_ref[...], xs_ref[...])
    out = x + _rms(acc_ref[...], g_ref[...])

    @pl.when(i < n_prompt_chunks)
    def _():
        op_ref[...] = out

    @pl.when(i >= n_prompt_chunks)
    def _():
        os_ref[...] = out


def _combine(run_start, run_count, xp, xs, pos, g, ys):
    npc = xp.shape[0] // CHUNK
    nsc = xs.shape[0] // CHUNK
    nch = npc + nsc
    tok8 = pl.BlockSpec((CHUNK, N_EXP), lambda i, *_: (i, 0))
    pspec = pl.BlockSpec((CHUNK, D), lambda i, *_: (jnp.minimum(i, npc - 1), 0))
    sspec = pl.BlockSpec((CHUNK, D), lambda i, *_: (jnp.maximum(i - npc, 0), 0))
    return pl.pallas_call(
        functools.partial(_combine_kernel, n_prompt_chunks=npc),
        out_shape=(jax.ShapeDtypeStruct(xp.shape, F32), jax.ShapeDtypeStruct(xs.shape, F32)),
        grid_spec=pltpu.PrefetchScalarGridSpec(
            num_scalar_prefetch=2,
            grid=(nch,),
            in_specs=[pspec, sspec, tok8, pl.BlockSpec((1, D), lambda i, *_: (0, 0)),
                      pl.BlockSpec(memory_space=pl.ANY)],
            out_specs=(pspec, sspec),
            scratch_shapes=[pltpu.VMEM((2, N_EXP * CWIN, D), BF16), pltpu.VMEM((CWIN, D), BF16),
                            pltpu.VMEM((CHUNK, D), F32),
                            pltpu.SemaphoreType.DMA((2, N_EXP)), pltpu.SemaphoreType.DMA]),
        compiler_params=_params(("arbitrary",), V7X_VMEM_LIMIT),
        name="moe_combine",
    )(run_start, run_count, xp, xs, pos, g, ys)


def _moe(xp, xs, g4, g5, w_router, wg, wu, wd):
    npc = xp.shape[0] // CHUNK
    nch = npc + xs.shape[0] // CHUNK
    t = nch * CHUNK
    h, rk, cw, cnt = _route(xp, xs, g4, w_router)

    after = cnt.reshape(nch, N_EXP).astype(I32)
    before = jnp.concatenate([jnp.zeros((1, N_EXP), I32), after[:-1]], axis=0)
    counts = after[-1]
    gsz = (counts + BLK - 1) // BLK * BLK
    gend = jnp.cumsum(gsz)
    gstart = gend - gsz
    nblk = (2 * t + N_EXP * (BLK - 1)) // BLK + 1 + (2 * CWIN + BLK - 1) // BLK
    bstart = jnp.arange(nblk, dtype=I32) * BLK
    blk_e = jnp.minimum(jnp.sum(gend[None, :] <= bstart[:, None], axis=1), N_EXP - 1).astype(I32)
    valid = bstart < gend[-1]
    r0 = bstart - gstart[blk_e]
    aft_e = after[:, blk_e]
    bef_e = before[:, blk_e]
    c_lo = jnp.where(valid, jnp.sum(aft_e <= r0[None, :], axis=0), 0).astype(I32)
    c_hi = jnp.where(valid, jnp.sum(bef_e < (r0 + BLK)[None, :], axis=0) - 1, -1).astype(I32)
    c_lo = jnp.minimum(c_lo, nch - 1)
    pos = jnp.where(rk >= 0, rk + gstart[None, :], -1).astype(I32)
    pos_t = pos.T.reshape(N_EXP, nch, 1, CHUNK)
    cw_t = cw.T.reshape(N_EXP, nch, 1, CHUNK)
    run_start = (gstart[None, :] + before).astype(I32).reshape(-1)
    run_count = (after - before).reshape(-1)
    nvalid = jnp.where(valid, jnp.clip(counts[blk_e] - r0, 0, BLK), 0).astype(I32)
    pcount = jnp.maximum(c_hi - c_lo + 1, 0)
    pend = jnp.cumsum(pcount)
    pbase = pend - pcount
    k = jnp.arange(nblk + 2 * nch * N_EXP, dtype=I32)
    pair_blk = jnp.minimum(jnp.sum(pend[None, :] <= k[:, None], axis=1), nblk - 1)
    pair_c = jnp.clip(c_lo[pair_blk] + k - pbase[pair_blk], 0, nch - 1).astype(I32)
    pair_e = blk_e[pair_blk]

    xsort, gsort = _gather(h, pos_t, cw_t, blk_e, pbase.astype(I32), pcount.astype(I32), r0.astype(I32),
                           before.reshape(-1), after.reshape(-1), pair_c, pair_e, pend[-1:].astype(I32), nblk)
    hh = _expert_up(xsort, blk_e, nvalid, wg, wu, tf=1792)
    ysort = _expert_down(hh, gsort, blk_e, nvalid, wd, tn=D)
    return _combine(run_start, run_count, xp, xs, pos, g5, ysort)


def kernel(x_prompt, x_sample, cache_conv, cache_mem_k, cache_mem_v, state_ssm_re, state_ssm_im, mem_prompt, norm_g, mem_norm_g, w_xq, w_xk, w_xv, w_xo, conv_w_pw1, conv_b_pw1, conv_w_dw, conv_b_dw, conv_ln_g, conv_ln_b, conv_w_pw2, conv_b_pw2, ssm_a_re, ssm_a_im, ssm_log_dt, ssm_b_re, ssm_b_im, ssm_c_re, ssm_c_im, ssm_d, ssm_w_glu, ssm_b_glu, ffn_w_gate, ffn_w_up, ffn_w_down, moe_w_router, moe_w_gate, moe_w_up, moe_w_down):
    nbp, seqp, _ = x_prompt.shape
    nbs, seqs, _ = x_sample.shape
    tp = nbp * seqp
    ts = nbs * seqs
    row = lambda a: a.reshape(1, -1)
    g = lambda i, k: norm_g[i, k].reshape(1, D)

    nl = w_xk.shape[0]
    pk, pv = _mem_kv(mem_prompt, mem_norm_g.reshape(nl, 1, D), w_xk, w_xv)
    p_mem_k = pk.reshape(nl, nbp, N_MEM, N_HEADS, HEAD_DIM)
    p_mem_v = pv.reshape(nl, nbp, N_MEM, N_HEADS, HEAD_DIM)
    ck = cache_mem_k.reshape(nl * nbs, N_MEM, N_HEADS, HEAD_DIM)
    cv = cache_mem_v.reshape(nl * nbs, N_MEM, N_HEADS, HEAD_DIM)

    conv_args = (conv_w_dw[0], row(conv_b_dw[0]), row(conv_ln_g[0]), row(conv_ln_b[0]),
                 conv_w_pw2[0], row(conv_b_pw2[0]), g(0, 1))
    xp = x_prompt.reshape(tp, D)
    xs = x_sample.reshape(ts, D)
    up = _conv_pw1(xp, g(0, 0), conv_w_pw1[0], row(conv_b_pw1[0]), tm=512).reshape(nbp, seqp, D)
    us = _conv_pw1(xs, g(0, 0), conv_w_pw1[0], row(conv_b_pw1[0]), tm=ts).reshape(nbs, seqs, D)
    hist_p = jnp.zeros((nbp, HIST, D), F32)
    hist_s = jnp.pad(cache_conv[0], ((0, 0), (HIST - CONV_W + 1, 0), (0, 0)))
    xp = _conv_dw_pw2(up, x_prompt, hist_p, *conv_args, tl=256).reshape(tp, D)
    xs = _conv_dw_pw2(us, x_sample, hist_s, *conv_args, tl=seqs).reshape(ts, D)
    p_conv = up[:, seqp - (CONV_W - 1):][None]
    s_conv = jnp.concatenate([cache_conv[0], us], axis=1)[:, -(CONV_W - 1):][None]

    tma = 512
    xp = _attn(xp, pk, pv, 0, w_xq[0], w_xo[0], g(0, 2), g(0, 3), nbp, seqp, tma)
    xs = _attn(xs, ck, cv, 0, w_xq[0], w_xo[0], g(0, 2), g(0, 3), nbs, seqs, seqs)

    ffn_w = (_to_bf16(ffn_w_gate[0], 4), _to_bf16(ffn_w_up[0], 4), _to_bf16(ffn_w_down[0], 4))
    tff = ffn_w_gate.shape[2] // 2
    xp = _ffn(xp, g(0, 4), g(0, 5), *ffn_w, tm=512, tf=tff, out_shape=(tp, D), out_index=lambda i, f: (i, 0))
    xs = _ffn(xs, g(0, 4), g(0, 5), *ffn_w, tm=ts, tf=tff, out_shape=(ts, D), out_index=lambda i, f: (i, 0))

    (ab_r, ab_i, a2_r, a2_i), (bb_r, bb_i, abb_r, abb_i) = _ssm_prep(
        ssm_a_re[0], ssm_a_im[0], ssm_log_dt[0], ssm_b_re[0], ssm_b_im[0])
    bdiag = lambda m: _block_diag(m.transpose(0, 2, 1)).astype(BF16)
    bm_r, bm_i = bdiag(bb_r), bdiag(bb_i)
    cm_r, cm_i = bdiag(ssm_c_re[0]), bdiag(-ssm_c_im[0])
    tail_args = (cm_r, cm_i, row(ssm_d[0]), _to_bf16(ssm_w_glu[0], 4), row(ssm_b_glu[0]))
    half = SUBLANES // 2
    pair_args = (jnp.concatenate([jnp.tile(ab_r, (half, 1)), jnp.tile(a2_r, (half, 1))]),
                 jnp.concatenate([jnp.tile(ab_i, (half, 1)), jnp.tile(a2_i, (half, 1))]),
                 jnp.concatenate([bm_r, bdiag(abb_r)], axis=1), jnp.concatenate([bm_i, bdiag(abb_i)], axis=1))
    zero_state = jnp.zeros((SUBLANES, SSM_N), F32)
    xp, p_sr, p_si = _ssm(xp.reshape(nbp, seqp, D), g(1, 0), g(1, 1), zero_state, zero_state,
                          *pair_args, *tail_args, tl=64)
    xs, s_sr, s_si = _ssm(xs.reshape(nbs, seqs, D), g(1, 0), g(1, 1), state_ssm_re[0].reshape(nbs, SSM_N),
                          state_ssm_im[0].reshape(nbs, SSM_N), jnp.tile(ab_r, (nbs, 1)), jnp.tile(ab_i, (nbs, 1)),
                          bm_r, bm_i, *tail_args, tl=seqs)
    st = lambda a, n: a[:n].reshape(1, n, SSM_G, SSM_P)
    p_ssm_re, p_ssm_im = st(p_sr, nbp), st(p_si, nbp)
    s_ssm_re, s_ssm_im = st(s_sr, nbs), st(s_si, nbs)

    xp = _attn(xp.reshape(tp, D), pk, pv, nbp, w_xq[1], w_xo[1], g(1, 2), g(1, 3), nbp, seqp, tma)
    xs = _attn(xs.reshape(ts, D), ck, cv, nbs, w_xq[1], w_xo[1], g(1, 2), g(1, 3), nbs, seqs, seqs)

    yp, ysm = _moe(xp, xs, g(1, 4), g(1, 5), moe_w_router[0], moe_w_gate[0], moe_w_up[0], moe_w_down[0])
    return (yp.reshape(nbp, seqp, D), ysm.reshape(nbs, seqs, D), p_conv, p_mem_k, p_mem_v,
            p_ssm_re, p_ssm_im, s_conv, s_ssm_re, s_ssm_im)
```

```python
import functools
import math

import jax
import jax.numpy as jnp
from jax import lax
from jax.experimental import pallas as pl
from jax.experimental.pallas import tpu as pltpu

F32 = jnp.float32
BF16 = jnp.bfloat16
I32 = jnp.int32

D = 1024
CONV_W = 31
HIST = 32
N_MEM = 256
N_HEADS = 4
HEAD_DIM = D // N_HEADS
SSM_G = 64
SSM_C = 16
SSM_P = 64
SSM_N = SSM_G * SSM_P
SSM_CB = 256
SSM_NCB = D // SSM_CB
SSM_SB = SSM_CB // SSM_C * SSM_P
N_EXP = 8
EPS = 1e-6

V7X_VMEM_LIMIT = 56 * 1024 * 1024
SUBLANES = 8

CHUNK = 256
BLK = 512
WIN = 512
GWIN = 128
GSLOTS = 4
CWIN = 256
BF16_ROWS = 16


def _params(sem, vmem=None):
    return pltpu.CompilerParams(dimension_semantics=sem, vmem_limit_bytes=vmem)


def _rms(x, g):
    return x * lax.rsqrt(jnp.mean(x * x, axis=-1, keepdims=True) + EPS) * g


def _const_spec(shape):
    nd = len(shape)
    return pl.BlockSpec(shape, lambda *_: (0,) * nd)


def _pw1_kernel(x_ref, g_ref, w_ref, b_ref, u_ref, wbf_ref):
    @pl.when(pl.program_id(0) == 0)
    def _():
        wbf_ref[...] = w_ref[...].astype(BF16)

    h = _rms(x_ref[...], g_ref[...]).astype(BF16)
    z = jnp.dot(h, wbf_ref[...], preferred_element_type=F32) + b_ref[...]
    u_ref[...] = z[:, :D] * jax.nn.sigmoid(z[:, D:])


def _conv_pw1(x, g, w, b, tm):
    t = x.shape[0]
    return pl.pallas_call(
        _pw1_kernel,
        out_shape=jax.ShapeDtypeStruct((t, D), F32),
        grid=(t // tm,),
        in_specs=[pl.BlockSpec((tm, D), lambda i: (i, 0)),
                  _const_spec((1, D)), _const_spec((D, 2 * D)), _const_spec((1, 2 * D))],
        out_specs=pl.BlockSpec((tm, D), lambda i: (i, 0)),
        scratch_shapes=[pltpu.VMEM((D, 2 * D), BF16)],
        compiler_params=_params(("arbitrary",), V7X_VMEM_LIMIT),
        name="conv_pw1",
    )(x, g, w, b)


def _conv2_kernel(u_ref, x_ref, hist_ref, wdw_ref, bdw_ref, lng_ref, lnb_ref, w2_ref, b2_ref,
                  g_ref, o_ref, ext_ref, sh_ref, conv_ref, w2bf_ref, wtap_ref, *, tl, rt):
    bi = pl.program_id(0)
    li = pl.program_id(1)

    @pl.when((bi == 0) & (li == 0))
    def _():
        w2bf_ref[...] = w2_ref[...].astype(BF16)
        for k in range(CONV_W):
            wtap_ref[k] = jnp.broadcast_to(wdw_ref[k:k + 1, :], (SUBLANES, D))

    @pl.when(li == 0)
    def _():
        ext_ref[0:HIST, :] = hist_ref[0]

    @pl.when(li > 0)
    def _():
        ext_ref[0:HIST, :] = ext_ref[tl:tl + HIST, :]

    ext_ref[HIST:HIST + tl, :] = u_ref[0]
    for s in range(1, SUBLANES):
        sh_ref[s - 1] = ext_ref[pl.ds(s, tl + HIST - SUBLANES), :]

    def rows(i, carry):
        r0 = pl.multiple_of(i * rt, rt)
        ntile = rt // SUBLANES
        accs = [jnp.zeros((SUBLANES, D), F32) + bdw_ref[...] for _ in range(ntile)]
        for k in range(CONV_W):
            off = HIST - CONV_W + 1 + k
            s = off % SUBLANES
            base = off - s
            w = wtap_ref[k]
            for ti in range(ntile):
                rows_ti = pl.ds(r0 + base + ti * SUBLANES, SUBLANES)
                src = ext_ref[rows_ti, :] if s == 0 else sh_ref[s - 1, rows_ti, :]
                accs[ti] = accs[ti] + w * src
        for ti in range(ntile):
            conv_ref[pl.ds(r0 + ti * SUBLANES, SUBLANES), :] = accs[ti]
        return carry

    lax.fori_loop(0, tl // rt, rows, 0)
    acc = conv_ref[...]
    mu = jnp.mean(acc, axis=-1, keepdims=True)
    xc = acc - mu
    var = jnp.mean(xc * xc, axis=-1, keepdims=True)
    y = xc * lax.rsqrt(var + EPS) * lng_ref[...] + lnb_ref[...]
    y = y * jax.nn.sigmoid(y)
    t = jnp.dot(y.astype(BF16), w2bf_ref[...], preferred_element_type=F32) + b2_ref[...]
    o_ref[0] = x_ref[0] + _rms(t, g_ref[...])


def _conv_dw_pw2(u, x, hist, wdw, bdw, lng, lnb, w2, b2, g, tl):
    nb, seq, _ = u.shape
    tok = pl.BlockSpec((1, tl, D), lambda b, l: (b, l, 0))
    return pl.pallas_call(
        functools.partial(_conv2_kernel, tl=tl, rt=min(tl, 32)),
        out_shape=jax.ShapeDtypeStruct((nb, seq, D), F32),
        grid=(nb, seq // tl),
        in_specs=[tok, tok, pl.BlockSpec((1, HIST, D), lambda b, l: (b, 0, 0)),
                  _const_spec((CONV_W, D)), _const_spec((1, D)), _const_spec((1, D)), _const_spec((1, D)),
                  _const_spec((D, D)), _const_spec((1, D)), _const_spec((1, D))],
        out_specs=tok,
        scratch_shapes=[pltpu.VMEM((tl + HIST, D), F32),
                        pltpu.VMEM((SUBLANES - 1, tl + HIST - SUBLANES, D), F32),
                        pltpu.VMEM((tl, D), F32), pltpu.VMEM((D, D), BF16),
                        pltpu.VMEM((CONV_W, SUBLANES, D), F32)],
        compiler_params=_params(("arbitrary", "arbitrary"), V7X_VMEM_LIMIT),
        name="conv_dw_pw2",
    )(u, x, hist, wdw, bdw, lng, lnb, w2, b2, g)


def _memkv_kernel(m_ref, g_ref, wk_ref, wv_ref, k_ref, v_ref, wkbf_ref, wvbf_ref):
    @pl.when(pl.program_id(1) == 0)
    def _():
        wkbf_ref[...] = wk_ref[0].astype(BF16)
        wvbf_ref[...] = wv_ref[0].astype(BF16)

    m = _rms(m_ref[0], g_ref[0]).astype(BF16)
    k = jnp.dot(m, wkbf_ref[...], preferred_element_type=F32)
    v = jnp.dot(m, wvbf_ref[...], preferred_element_type=F32)
    for hd in range(N_HEADS):
        sl = slice(hd * HEAD_DIM, (hd + 1) * HEAD_DIM)
        k_ref[0, :, hd, :] = k[:, sl]
        v_ref[0, :, hd, :] = v[:, sl]


def _mem_kv(mem, g, wk, wv):
    nb = mem.shape[0]
    nl = wk.shape[0]
    kv = jax.ShapeDtypeStruct((nl * nb, N_MEM, N_HEADS, HEAD_DIM), F32)
    out = pl.BlockSpec((1, N_MEM, N_HEADS, HEAD_DIM), lambda l, b: (l * nb + b, 0, 0, 0))
    w = pl.BlockSpec((1, D, D), lambda l, b: (l, 0, 0))
    return pl.pallas_call(
        _memkv_kernel,
        out_shape=(kv, kv),
        grid=(nl, nb),
        in_specs=[pl.BlockSpec((1, N_MEM, D), lambda l, b: (b, 0, 0)),
                  pl.BlockSpec((1, 1, D), lambda l, b: (l, 0, 0)), w, w],
        out_specs=(out, out),
        scratch_shapes=[pltpu.VMEM((D, D), BF16), pltpu.VMEM((D, D), BF16)],
        compiler_params=_params(("arbitrary", "arbitrary"), V7X_VMEM_LIMIT),
        name="mem_kv",
    )(mem, g, wk, wv)


def _attn_kernel(x_ref, k_ref, v_ref, wq_ref, wo_ref, g2_ref, g3_ref, o_ref, wqbf_ref, wobf_ref, kb_ref, vb_ref):
    @pl.when((pl.program_id(0) == 0) & (pl.program_id(1) == 0))
    def _():
        wqbf_ref[...] = wq_ref[...].astype(BF16)
        wobf_ref[...] = wo_ref[...].astype(BF16)

    @pl.when(pl.program_id(1) == 0)
    def _():
        for hd in range(N_HEADS):
            sl = slice(hd * HEAD_DIM, (hd + 1) * HEAD_DIM)
            kb_ref[:, sl] = k_ref[0, :, hd, :].astype(BF16)
            vb_ref[:, sl] = v_ref[0, :, hd, :].astype(BF16)

    x = x_ref[...]
    h = _rms(x, g2_ref[...]).astype(BF16)
    q = (jnp.dot(h, wqbf_ref[...], preferred_element_type=F32) * (HEAD_DIM ** -0.5)).astype(BF16)
    heads = []
    for hd in range(N_HEADS):
        sl = slice(hd * HEAD_DIM, (hd + 1) * HEAD_DIM)
        s = lax.dot_general(q[:, sl], kb_ref[:, sl], (((1,), (1,)), ((), ())), preferred_element_type=F32)
        p = jnp.exp(s - jnp.max(s, axis=-1, keepdims=True))
        p = p / jnp.sum(p, axis=-1, keepdims=True)
        heads.append(jnp.dot(p.astype(BF16), vb_ref[:, sl], preferred_element_type=F32))
    o = jnp.concatenate(heads, axis=1).astype(BF16)
    t = jnp.dot(o, wobf_ref[...], preferred_element_type=F32)
    o_ref[...] = x + _rms(t, g3_ref[...])


def _attn(x, k, v, kv_base, wq, wo, g2, g3, nb, seq, tm):
    tok = pl.BlockSpec((tm, D), lambda b, l: (b * (seq // tm) + l, 0))
    kv = pl.BlockSpec((1, N_MEM, N_HEADS, HEAD_DIM), lambda b, l: (kv_base + b, 0, 0, 0))
    return pl.pallas_call(
        _attn_kernel,
        out_shape=jax.ShapeDtypeStruct((nb * seq, D), F32),
        grid=(nb, seq // tm),
        in_specs=[tok, kv, kv, _const_spec((D, D)), _const_spec((D, D)), _const_spec((1, D)), _const_spec((1, D))],
        out_specs=tok,
        scratch_shapes=[pltpu.VMEM((D, D), BF16), pltpu.VMEM((D, D), BF16),
                        pltpu.VMEM((N_MEM, D), BF16), pltpu.VMEM((N_MEM, D), BF16)],
        compiler_params=_params(("arbitrary", "arbitrary"), V7X_VMEM_LIMIT),
        name="mem_attn",
    )(x, k, v, wq, wo, g2, g3)


def _cast_kernel(w_ref, o_ref):
    o_ref[...] = w_ref[...].astype(BF16)


def _to_bf16(w, nsplit):
    r, c = w.shape
    blk = pl.BlockSpec((r // nsplit, c), lambda i: (i, 0))
    return pl.pallas_call(
        _cast_kernel,
        out_shape=jax.ShapeDtypeStruct((r, c), BF16),
        grid=(nsplit,),
        in_specs=[blk],
        out_specs=blk,
        compiler_params=_params(("arbitrary",)),
        name="to_bf16",
    )(w)


def _ffn_kernel(x_ref, g4_ref, g5_ref, wg_ref, wu_ref, wd_ref, o_ref, h_ref, acc_ref):
    f = pl.program_id(1)

    @pl.when(f == 0)
    def _():
        h_ref[...] = _rms(x_ref[...], g4_ref[...]).astype(BF16)
        acc_ref[...] = jnp.zeros_like(acc_ref)

    h = h_ref[...]
    a = jnp.dot(h, wg_ref[...], preferred_element_type=F32)
    b = jnp.dot(h, wu_ref[...], preferred_element_type=F32)
    hh = (a * jax.nn.sigmoid(a) * b).astype(BF16)
    acc_ref[...] += jnp.dot(hh, wd_ref[...], preferred_element_type=F32)

    @pl.when(f == pl.num_programs(1) - 1)
    def _():
        o_ref[...] = x_ref[...] + _rms(acc_ref[...], g5_ref[...])


def _ffn(x, g4, g5, wg, wu, wd, tm, tf, out_shape, out_index):
    t = x.shape[0]
    ff = wg.shape[1]
    return pl.pallas_call(
        _ffn_kernel,
        out_shape=jax.ShapeDtypeStruct(out_shape, F32),
        grid=(t // tm, ff // tf),
        in_specs=[pl.BlockSpec((tm, D), lambda i, f: (i, 0)),
                  _const_spec((1, D)), _const_spec((1, D)),
                  pl.BlockSpec((D, tf), lambda i, f: (0, f)),
                  pl.BlockSpec((D, tf), lambda i, f: (0, f)),
                  pl.BlockSpec((tf, D), lambda i, f: (f, 0))],
        out_specs=pl.BlockSpec((tm, D), out_index),
        scratch_shapes=[pltpu.VMEM((tm, D), BF16), pltpu.VMEM((tm, D), F32)],
        compiler_params=_params(("arbitrary", "arbitrary"), V7X_VMEM_LIMIT),
        name="dense_ffn",
    )(x, g4, g5, wg, wu, wd)


def _ssm_prep_kernel(lr_ref, li_ref, ldt_ref, br_ref, bi_ref, abr_ref, abi_ref, bbr_ref, bbi_ref,
                     a2r_ref, a2i_ref, abbr_ref, abbi_ref):
    dt = jnp.exp(ldt_ref[...])
    lr = lr_ref[...]
    li = li_ref[...]
    mag = jnp.exp(lr * dt)
    ab_r = mag * jnp.cos(li * dt)
    ab_i = mag * jnp.sin(li * dt)
    den = lr * lr + li * li
    nr = ab_r - 1.0
    k_r = (nr * lr + ab_i * li) / den
    k_i = (ab_i * lr - nr * li) / den
    br = br_ref[...]
    bi = bi_ref[...]
    bb_r = k_r * br - k_i * bi
    bb_i = k_r * bi + k_i * br
    abr_ref[...] = ab_r
    abi_ref[...] = ab_i
    bbr_ref[...] = bb_r
    bbi_ref[...] = bb_i
    a2r_ref[...] = ab_r * ab_r - ab_i * ab_i
    a2i_ref[...] = 2.0 * (ab_r * ab_i)
    abbr_ref[...] = ab_r * bb_r - ab_i * bb_i
    abbi_ref[...] = ab_r * bb_i + ab_i * bb_r


def _ssm_prep(a_re, a_im, log_dt, b_re, b_im):
    n = SSM_P * SSM_C
    rep = lambda a: jnp.repeat(a, SSM_C, axis=1)
    shp = jax.ShapeDtypeStruct((SSM_G, n), F32)
    abr, abi, bbr, bbi, a2r, a2i, abbr, abbi = pl.pallas_call(
        _ssm_prep_kernel,
        out_shape=(shp,) * 8,
        name="ssm_prep",
    )(rep(a_re), rep(a_im), log_dt.reshape(SSM_G, 1), b_re.reshape(SSM_G, n), b_im.reshape(SSM_G, n))
    pick = lambda a: a.reshape(SSM_G, SSM_P, SSM_C)[:, :, 0].reshape(1, SSM_N)
    gpc = lambda a: a.reshape(SSM_G, SSM_P, SSM_C)
    return (pick(abr), pick(abi), pick(a2r), pick(a2i)), (gpc(bbr), gpc(bbi), gpc(abbr), gpc(abbi))


def _block_diag(m, gpb=SSM_CB // SSM_C):
    g, a, b = m.shape
    m = m.reshape(g // gpb, gpb, a, 1, b)
    eye = jnp.eye(gpb, dtype=m.dtype).reshape(1, gpb, 1, gpb, 1)
    return (m * eye).reshape(g // gpb, gpb * a, gpb * b)


def _ssm_kernel(x_ref, g0_ref, g1_ref, s0r_ref, s0i_ref, ar_ref, ai_ref, bmr_ref, bmi_ref, cmr_ref, cmi_ref,
                d_ref, wglu_ref, bglu_ref, o_ref, sr_ref, si_ref,
                bur_ref, bui_ref, *, nb, rb, lc):
    @pl.when(pl.program_id(0) == 0)
    def _():
        sr_ref[...] = s0r_ref[...]
        si_ref[...] = s0i_ref[...]

    tl = rb // nb
    x = x_ref[...].reshape(rb, D)
    h = _rms(x, g0_ref[...])
    r = lax.broadcasted_iota(I32, (rb, rb), 0)
    c = lax.broadcasted_iota(I32, (rb, rb), 1)
    to_time_major = (c == (r & (nb - 1)) * tl + (r >> (nb.bit_length() - 1))).astype(BF16)
    to_batch_major = (c == (r & (tl - 1)) * nb + (r >> (tl.bit_length() - 1))).astype(BF16)
    hb = jnp.dot(to_time_major, h.astype(BF16), preferred_element_type=F32).astype(BF16)
    rows = max(nb, SUBLANES)
    pair = nb < SUBLANES
    if pair:
        odd = ((r >> (nb.bit_length() - 1)) & 1) == 1
        prev_time_major = ((c == (r & (nb - 1)) * tl + (r >> (nb.bit_length() - 1)) - 1) & odd).astype(BF16)
        hprev = jnp.dot(prev_time_major, h.astype(BF16), preferred_element_type=F32).astype(BF16)
        low = lax.broadcasted_iota(I32, (SUBLANES, lc), 0) < nb

    def scan_chunk(cs):
        a_r = ar_ref[:, cs]
        a_i = ai_ref[:, cs]
        s_r = sr_ref[:, cs]
        s_i = si_ref[:, cs]
        for j in range(rb // rows):
            rs = slice(j * rows, (j + 1) * rows)
            n_r = a_r * s_r - a_i * s_i + bur_ref[rs, cs]
            n_i = a_r * s_i + a_i * s_r + bui_ref[rs, cs]
            bur_ref[rs, cs] = n_r
            bui_ref[rs, cs] = n_i
            if pair:
                s_r = jnp.where(low, pltpu.roll(n_r, nb, 0), n_r)
                s_i = jnp.where(low, pltpu.roll(n_i, nb, 0), n_i)
            else:
                s_r, s_i = n_r, n_i
        sr_ref[:, cs] = s_r
        si_ref[:, cs] = s_i

    ist = bmr_ref.shape[2]
    ich = bmr_ref.shape[1] // 2 if pair else bmr_ref.shape[1]
    ys = []
    for cb in range(SSM_NCB):
        ss = slice(cb * SSM_SB, (cb + 1) * SSM_SB)
        for ib in range(cb * (SSM_SB // ist), (cb + 1) * (SSM_SB // ist)):
            hs = hb[:, ib * ich:(ib + 1) * ich]
            if pair:
                hs = jnp.concatenate([hs, hprev[:, ib * ich:(ib + 1) * ich]], axis=1)
            bur_ref[:, ib * ist:(ib + 1) * ist] = jnp.dot(hs, bmr_ref[ib], preferred_element_type=F32)
            bui_ref[:, ib * ist:(ib + 1) * ist] = jnp.dot(hs, bmi_ref[ib], preferred_element_type=F32)
            for c in range(ist // lc):
                scan_chunk(slice(ib * ist + c * lc, ib * ist + (c + 1) * lc))
        ys.append(jnp.dot(bur_ref[:, ss].astype(BF16), cmr_ref[cb], preferred_element_type=F32)
                  + jnp.dot(bui_ref[:, ss].astype(BF16), cmi_ref[cb], preferred_element_type=F32))
    y_tm = jnp.concatenate(ys, axis=1)
    y1 = y_tm.astype(BF16)
    rem = y_tm - y1.astype(F32)
    y2 = rem.astype(BF16)
    y3 = (rem - y2.astype(F32)).astype(BF16)
    y = (jnp.dot(to_batch_major, y1, preferred_element_type=F32)
         + jnp.dot(to_batch_major, y2, preferred_element_type=F32)
         + jnp.dot(to_batch_major, y3, preferred_element_type=F32))
    y = y + d_ref[...] * h
    y = jax.nn.gelu(y).astype(BF16)
    z = jnp.dot(y, wglu_ref[...], preferred_element_type=F32) + bglu_ref[...]
    t = z[:, :D] * jax.nn.sigmoid(z[:, D:])
    o_ref[...] = (x + _rms(t, g1_ref[...])).reshape(nb, tl, D)


def _ssm(x, g0, g1, s0r, s0i, ab_r, ab_i, bm_r, bm_i, cm_r, cm_i, d, wglu, bglu, tl):
    nb, seq, _ = x.shape
    rb = nb * tl
    rows = max(nb, SUBLANES)
    st = jax.ShapeDtypeStruct((rows, SSM_N), F32)
    row = pl.BlockSpec((nb, tl, D), lambda i: (0, i, 0))
    return pl.pallas_call(
        functools.partial(_ssm_kernel, nb=nb, rb=rb, lc=512),
        out_shape=(jax.ShapeDtypeStruct((nb, seq, D), F32), st, st),
        grid=(seq // tl,),
        in_specs=[row, _const_spec((1, D)), _const_spec((1, D)),
                  _const_spec((rows, SSM_N)), _const_spec((rows, SSM_N)),
                  _const_spec((rows, SSM_N)), _const_spec((rows, SSM_N)),
                  _const_spec(bm_r.shape), _const_spec(bm_i.shape),
                  _const_spec((SSM_NCB, SSM_SB, SSM_CB)), _const_spec((SSM_NCB, SSM_SB, SSM_CB)),
                  _const_spec((1, D)), _const_spec((D, 2 * D)), _const_spec((1, 2 * D))],
        out_specs=(row, _const_spec((rows, SSM_N)), _const_spec((rows, SSM_N))),
        scratch_shapes=[pltpu.VMEM((rb, SSM_N), F32), pltpu.VMEM((rb, SSM_N), F32)],
        compiler_params=_params(("arbitrary",), V7X_VMEM_LIMIT),
        name="ssm",
    )(x, g0, g1, s0r, s0i, ab_r, ab_i, bm_r, bm_i, cm_r, cm_i, d, wglu, bglu)


def _route_kernel(xp_ref, xs_ref, g_ref, wrt_ref, h_ref, rk_ref, rkt_ref, cwt_ref, cnt_ref, carry_ref,
                  *, n_prompt_chunks):
    i = pl.program_id(0)

    @pl.when(i == 0)
    def _():
        carry_ref[...] = jnp.zeros_like(carry_ref)

    x = jnp.where(i < n_prompt_chunks, xp_ref[...], xs_ref[...])
    h = _rms(x, g_ref[...])
    h_ref[...] = h.astype(BF16)
    lg = lax.dot_general(wrt_ref[...], h, (((1,), (1,)), ((), ())), preferred_element_type=F32,
                         precision=lax.Precision.HIGHEST)
    ex = lax.broadcasted_iota(I32, lg.shape, 0)
    m1 = jnp.max(lg, axis=0, keepdims=True)
    i1 = jnp.min(jnp.where(lg == m1, ex, N_EXP), axis=0, keepdims=True)
    first = ex == i1
    lg2 = jnp.where(first, -jnp.inf, lg)
    m2 = jnp.max(lg2, axis=0, keepdims=True)
    i2 = jnp.min(jnp.where(lg2 == m2, ex, N_EXP), axis=0, keepdims=True)
    second = ex == i2
    e = jnp.exp(m2 - m1)
    den = 1.0 + e
    cwt_ref[...] = jnp.where(first, 1.0 / den, 0.0) + jnp.where(second, e / den, 0.0)
    assigned = first | second
    r = lax.broadcasted_iota(I32, (CHUNK, CHUNK), 0)
    c = lax.broadcasted_iota(I32, (CHUNK, CHUNK), 1)
    earlier = (r < c).astype(BF16)
    rank = jnp.dot(assigned.astype(BF16), earlier, preferred_element_type=F32) + carry_ref[...]
    rank = jnp.where(assigned, rank, -1.0)
    rkt_ref[...] = rank.astype(I32)
    rk_ref[...] = rank.T.astype(I32)
    carry_ref[...] += jnp.sum(assigned.astype(F32), axis=1, keepdims=True)
    cnt_ref[0] = carry_ref[...]


def _route(xp, xs, g, wrt):
    npc = xp.shape[0] // CHUNK
    nch = npc + xs.shape[0] // CHUNK
    t = nch * CHUNK
    exp_major = pl.BlockSpec((N_EXP, CHUNK), lambda i: (0, i))
    return pl.pallas_call(
        functools.partial(_route_kernel, n_prompt_chunks=npc),
        out_shape=(jax.ShapeDtypeStruct((t, D), BF16), jax.ShapeDtypeStruct((t, N_EXP), I32),
                   jax.ShapeDtypeStruct((N_EXP, t), I32), jax.ShapeDtypeStruct((N_EXP, t), F32),
                   jax.ShapeDtypeStruct((nch, N_EXP, 1), F32)),
        grid=(nch,),
        in_specs=[pl.BlockSpec((CHUNK, D), lambda i: (jnp.minimum(i, npc - 1), 0)),
                  pl.BlockSpec((CHUNK, D), lambda i: (jnp.maximum(i - npc, 0), 0)),
                  _const_spec((1, D)), _const_spec((N_EXP, D))],
        out_specs=(pl.BlockSpec((CHUNK, D), lambda i: (i, 0)),
                   pl.BlockSpec((CHUNK, N_EXP), lambda i: (i, 0)), exp_major, exp_major,
                   pl.BlockSpec((1, N_EXP, 1), lambda i: (i, 0, 0))),
        scratch_shapes=[pltpu.VMEM((N_EXP, 1), F32)],
        compiler_params=_params(("arbitrary",)),
        name="moe_route",
    )(xp, xs, g, wrt)


def _gather_kernel(be_ref, pb_ref, pn_ref, r0_ref, bef_ref, aft_ref, pc_ref, pe_ref, pt_ref,
                   h_hbm, pos_hbm, cw_hbm, o_ref, gs_ref, hbuf, pbuf, cbuf, acc_ref, gacc_ref, sem):
    d = pl.program_id(0)
    e = be_ref[d]
    base = pb_ref[d]
    r0 = r0_ref[d]
    total = pt_ref[0]

    def copies(idx):
        slot = idx % GSLOTS
        c = pc_ref[idx]
        ee = pe_ref[idx]
        return (pltpu.make_async_copy(h_hbm.at[pl.ds(c * CHUNK, CHUNK)], hbuf.at[slot], sem.at[0, slot]),
                pltpu.make_async_copy(pos_hbm.at[ee, c], pbuf.at[slot], sem.at[1, slot]),
                pltpu.make_async_copy(cw_hbm.at[ee, c], cbuf.at[slot], sem.at[2, slot]))

    def start(idx):
        @pl.when(idx < total)
        def _():
            for cp in copies(idx):
                cp.start()

    @pl.when(d == 0)
    def _():
        for k in range(GSLOTS - 1):
            start(k)

    acc_ref[...] = jnp.zeros_like(acc_ref)
    gacc_ref[...] = jnp.zeros_like(gacc_ref)
    row = d * BLK + lax.broadcasted_iota(I32, (GWIN, CHUNK), 0)

    def body(j, carry):
        idx = base + j
        slot = idx % GSLOTS
        for cp in copies(idx):
            cp.wait()
        start(idx + GSLOTS - 1)

        c = pc_ref[idx]
        first = jnp.clip(bef_ref[c * N_EXP + e] - r0, 0, BLK)
        last = jnp.clip(aft_ref[c * N_EXP + e] - r0, 0, BLK)

        def window(w, c2):
            w0 = pl.multiple_of(w * GWIN, GWIN)
            hit = pbuf[slot] == row + w0
            acc_ref[pl.ds(w0, GWIN), :] += jnp.dot(hit.astype(BF16), hbuf[slot], preferred_element_type=F32)
            gacc_ref[pl.ds(w0, GWIN), :] += jnp.sum(jnp.where(hit, cbuf[slot], 0.0), axis=1, keepdims=True)
            return c2

        lax.fori_loop(first // GWIN, (last + GWIN - 1) // GWIN, window, 0)
        return carry

    lax.fori_loop(0, pn_ref[d], body, 0)
    o_ref[...] = acc_ref[...].astype(BF16)
    gs_ref[...] = gacc_ref[...]


def _gather(h, pos_t, cw_t, blk_e, pbase, pcount, r0, before, after, pair_c, pair_e, ptotal, nblk):
    hbm = pl.BlockSpec(memory_space=pl.ANY)
    return pl.pallas_call(
        _gather_kernel,
        out_shape=(jax.ShapeDtypeStruct((nblk * BLK, D), BF16), jax.ShapeDtypeStruct((nblk * BLK, 1), F32)),
        grid_spec=pltpu.PrefetchScalarGridSpec(
            num_scalar_prefetch=9,
            grid=(nblk,),
            in_specs=[hbm, hbm, hbm],
            out_specs=(pl.BlockSpec((BLK, D), lambda d, *_: (d, 0)), pl.BlockSpec((BLK, 1), lambda d, *_: (d, 0))),
            scratch_shapes=[pltpu.VMEM((GSLOTS, CHUNK, D), BF16), pltpu.VMEM((GSLOTS, 1, CHUNK), I32),
                            pltpu.VMEM((GSLOTS, 1, CHUNK), F32), pltpu.VMEM((BLK, D), F32),
                            pltpu.VMEM((BLK, 1), F32), pltpu.SemaphoreType.DMA((3, GSLOTS))]),
        compiler_params=_params(("arbitrary",)),
        name="moe_gather",
    )(blk_e, pbase, pcount, r0, before, after, pair_c, pair_e, ptotal, h, pos_t, cw_t)


def _expert_up_kernel(be_ref, nv_ref, x_ref, wg_ref, wu_ref, o_ref, wgbf_ref, wubf_ref):
    d = pl.program_id(1)
    changed = (d == 0) | (be_ref[d] != be_ref[jnp.maximum(d - 1, 0)])

    @pl.when(changed)
    def _():
        wgbf_ref[...] = wg_ref[0].astype(BF16)
        wubf_ref[...] = wu_ref[0].astype(BF16)

    def up(rows):
        x = x_ref[rows, :]
        a = jnp.dot(x, wgbf_ref[...], preferred_element_type=F32)
        b = jnp.dot(x, wubf_ref[...], preferred_element_type=F32)
        o_ref[rows, :] = (a * jax.nn.sigmoid(a) * b).astype(BF16)

    half = BLK // 2
    nrows = nv_ref[d]

    @pl.when(nrows > half)
    def _():
        up(slice(0, BLK))

    @pl.when((nrows > 0) & (nrows <= half))
    def _():
        up(slice(0, half))
        o_ref[half:, :] = jnp.zeros((BLK - half, o_ref.shape[1]), BF16)

    @pl.when(nrows == 0)
    def _():
        o_ref[...] = jnp.zeros_like(o_ref)


def _expert_up(xs, blk_e, nvalid, wg, wu, tf):
    p = xs.shape[0]
    ff = wg.shape[2]
    return pl.pallas_call(
        _expert_up_kernel,
        out_shape=jax.ShapeDtypeStruct((p, ff), BF16),
        grid_spec=pltpu.PrefetchScalarGridSpec(
            num_scalar_prefetch=2,
            grid=(ff // tf, p // BLK),
            in_specs=[pl.BlockSpec((BLK, D), lambda f, d, be, nv: (d, 0)),
                      pl.BlockSpec((1, D, tf), lambda f, d, be, nv: (be[d], 0, f)),
                      pl.BlockSpec((1, D, tf), lambda f, d, be, nv: (be[d], 0, f))],
            out_specs=pl.BlockSpec((BLK, tf), lambda f, d, be, nv: (d, f)),
            scratch_shapes=[pltpu.VMEM((D, tf), BF16), pltpu.VMEM((D, tf), BF16)]),
        compiler_params=_params(("arbitrary", "arbitrary"), V7X_VMEM_LIMIT),
        name="moe_up",
    )(blk_e, nvalid, xs, wg, wu)


def _expert_down_kernel(be_ref, nv_ref, h_ref, gs_ref, wd_ref, o_ref, wdbf_ref):
    d = pl.program_id(1)
    changed = (d == 0) | (be_ref[d] != be_ref[jnp.maximum(d - 1, 0)])

    @pl.when(changed)
    def _():
        wdbf_ref[...] = wd_ref[0].astype(BF16)

    def down(rows):
        y = jnp.dot(h_ref[rows, :], wdbf_ref[...], preferred_element_type=F32)
        o_ref[rows, :] = (y * gs_ref[rows, :]).astype(BF16)

    half = BLK // 2
    nrows = nv_ref[d]

    @pl.when(nrows > half)
    def _():
        down(slice(0, BLK))

    @pl.when((nrows > 0) & (nrows <= half))
    def _():
        down(slice(0, half))
        o_ref[half:, :] = jnp.zeros((BLK - half, o_ref.shape[1]), BF16)

    @pl.when(nrows == 0)
    def _():
        o_ref[...] = jnp.zeros_like(o_ref)


def _expert_down(hh, gsort, blk_e, nvalid, wd, tn):
    p, ff = hh.shape
    return pl.pallas_call(
        _expert_down_kernel,
        out_shape=jax.ShapeDtypeStruct((p, D), BF16),
        grid_spec=pltpu.PrefetchScalarGridSpec(
            num_scalar_prefetch=2,
            grid=(D // tn, p // BLK),
            in_specs=[pl.BlockSpec((BLK, ff), lambda n, d, be, nv: (d, 0)),
                      pl.BlockSpec((BLK, 1), lambda n, d, be, nv: (d, 0)),
                      pl.BlockSpec((1, ff, tn), lambda n, d, be, nv: (be[d], 0, n))],
            out_specs=pl.BlockSpec((BLK, tn), lambda n, d, be, nv: (d, n)),
            scratch_shapes=[pltpu.VMEM((ff, tn), BF16)]),
        compiler_params=_params(("arbitrary", "arbitrary"), V7X_VMEM_LIMIT),
        name="moe_down",
    )(blk_e, nvalid, hh, gsort, wd)


def _combine_kernel(rs_ref, rc_ref, xp_ref, xs_ref, pos_ref, g_ref, y_hbm, op_ref, os_ref, wbuf, obuf, acc_ref,
                    sem, osem, *, n_prompt_chunks):
    i = pl.program_id(0)
    nch = pl.num_programs(0)

    def window_start(chunk, e):
        return pl.multiple_of(rs_ref[chunk * N_EXP + e] // BF16_ROWS * BF16_ROWS, BF16_ROWS)

    def copy(chunk, e, slot):
        return pltpu.make_async_copy(y_hbm.at[pl.ds(window_start(chunk, e), CWIN)],
                                     wbuf.at[slot, pl.ds(e * CWIN, CWIN)], sem.at[slot, e])

    @pl.when(i == 0)
    def _():
        for e in range(N_EXP):
            copy(0, e, 0).start()

    slot = i % 2

    @pl.when(i + 1 < nch)
    def _():
        for e in range(N_EXP):
            copy(i + 1, e, 1 - slot).start()

    col = lax.broadcasted_iota(I32, (CHUNK, CWIN), 1)
    hits = []
    for e in range(N_EXP):
        copy(i, e, slot).wait()
        hits.append((pos_ref[:, e:e + 1] - window_start(i, e)) == col)
    onehot = jnp.concatenate(hits, axis=1).astype(BF16)
    acc_ref[...] = jnp.dot(onehot, wbuf[slot], preferred_element_type=F32)

    for e in range(N_EXP):
        tail = window_start(i, e) + CWIN

        @pl.when(rs_ref[i * N_EXP + e] + rc_ref[i * N_EXP + e] > tail)
        def _():
            cp = pltpu.make_async_copy(y_hbm.at[pl.ds(tail, CWIN)], obuf, osem)
            cp.start()
            cp.wait()
            hit = ((pos_ref[:, e:e + 1] - tail) == col).astype(BF16)
            acc_ref[...] += jnp.dot(hit, obuf[...], preferred_element_type=F32)

    x = jnp.where(i < n_prompt_chunks, xp_ref[...], xs_ref[...])
    out = x + _rms(acc_ref[...], g_ref[...])

    @pl.when(i < n_prompt_chunks)
    def _():
        op_ref[...] = out

    @pl.when(i >= n_prompt_chunks)
    def _():
        os_ref[...] = out


def _combine(run_start, run_count, xp, xs, pos, g, ys):
    npc = xp.shape[0] // CHUNK
    nsc = xs.shape[0] // CHUNK
    nch = npc + nsc
    tok8 = pl.BlockSpec((CHUNK, N_EXP), lambda i, *_: (i, 0))
    pspec = pl.BlockSpec((CHUNK, D), lambda i, *_: (jnp.minimum(i, npc - 1), 0))
    sspec = pl.BlockSpec((CHUNK, D), lambda i, *_: (jnp.maximum(i - npc, 0), 0))
    return pl.pallas_call(
        functools.partial(_combine_kernel, n_prompt_chunks=npc),
        out_shape=(jax.ShapeDtypeStruct(xp.shape, F32), jax.ShapeDtypeStruct(xs.shape, F32)),
        grid_spec=pltpu.PrefetchScalarGridSpec(
            num_scalar_prefetch=2,
            grid=(nch,),
            in_specs=[pspec, sspec, tok8, pl.BlockSpec((1, D), lambda i, *_: (0, 0)),
                      pl.BlockSpec(memory_space=pl.ANY)],
            out_specs=(pspec, sspec),
            scratch_shapes=[pltpu.VMEM((2, N_EXP * CWIN, D), BF16), pltpu.VMEM((CWIN, D), BF16),
                            pltpu.VMEM((CHUNK, D), F32),
                            pltpu.SemaphoreType.DMA((2, N_EXP)), pltpu.SemaphoreType.DMA]),
        compiler_params=_params(("arbitrary",), V7X_VMEM_LIMIT),
        name="moe_combine",
    )(run_start, run_count, xp, xs, pos, g, ys)


def _moe(xp, xs, g4, g5, w_router, wg, wu, wd):
    npc = xp.shape[0] // CHUNK
    nch = npc + xs.shape[0] // CHUNK
    t = nch * CHUNK
    h, rk, rk_t, cw_t, cnt = _route(xp, xs, g4, w_router.T)

    after = cnt.reshape(nch, N_EXP).astype(I32)
    before = jnp.concatenate([jnp.zeros((1, N_EXP), I32), after[:-1]], axis=0)
    counts = after[-1]
    gsz = (counts + BLK - 1) // BLK * BLK
    gend = jnp.cumsum(gsz)
    gstart = gend - gsz
    nblk = (2 * t + N_EXP * (BLK - 1)) // BLK + 1 + (2 * CWIN + BLK - 1) // BLK
    bstart = jnp.arange(nblk, dtype=I32) * BLK
    blk_e = jnp.minimum(jnp.sum(gend[None, :] <= bstart[:, None], axis=1), N_EXP - 1).astype(I32)
    valid = bstart < gend[-1]
    r0 = bstart - gstart[blk_e]
    aft_e = after[:, blk_e]
    bef_e = before[:, blk_e]
    c_lo = jnp.where(valid, jnp.sum(aft_e <= r0[None, :], axis=0), 0).astype(I32)
    c_hi = jnp.where(valid, jnp.sum(bef_e < (r0 + BLK)[None, :], axis=0) - 1, -1).astype(I32)
    c_lo = jnp.minimum(c_lo, nch - 1)
    pos = jnp.where(rk >= 0, rk + gstart[None, :], -1).astype(I32)
    pos_t = jnp.where(rk_t >= 0, rk_t + gstart[:, None], -1).astype(I32).reshape(N_EXP, nch, 1, CHUNK)
    cw_t = cw_t.reshape(N_EXP, nch, 1, CHUNK)
    run_start = (gstart[None, :] + before).astype(I32).reshape(-1)
    run_count = (after - before).reshape(-1)
    nvalid = jnp.where(valid, jnp.clip(counts[blk_e] - r0, 0, BLK), 0).astype(I32)
    pcount = jnp.maximum(c_hi - c_lo + 1, 0)
    pend = jnp.cumsum(pcount)
    pbase = pend - pcount
    k = jnp.arange(nblk + 2 * nch * N_EXP, dtype=I32)
    pair_blk = jnp.minimum(jnp.sum(pend[None, :] <= k[:, None], axis=1), nblk - 1)
    pair_c = jnp.clip(c_lo[pair_blk] + k - pbase[pair_blk], 0, nch - 1).astype(I32)
    pair_e = blk_e[pair_blk]

    xsort, gsort = _gather(h, pos_t, cw_t, blk_e, pbase.astype(I32), pcount.astype(I32), r0.astype(I32),
                           before.reshape(-1), after.reshape(-1), pair_c, pair_e, pend[-1:].astype(I32), nblk)
    hh = _expert_up(xsort, blk_e, nvalid, wg, wu, tf=1792)
    ysort = _expert_down(hh, gsort, blk_e, nvalid, wd, tn=D)
    return _combine(run_start, run_count, xp, xs, pos, g5, ysort)


def kernel(x_prompt, x_sample, cache_conv, cache_mem_k, cache_mem_v, state_ssm_re, state_ssm_im, mem_prompt, norm_g, mem_norm_g, w_xq, w_xk, w_xv, w_xo, conv_w_pw1, conv_b_pw1, conv_w_dw, conv_b_dw, conv_ln_g, conv_ln_b, conv_w_pw2, conv_b_pw2, ssm_a_re, ssm_a_im, ssm_log_dt, ssm_b_re, ssm_b_im, ssm_c_re, ssm_c_im, ssm_d, ssm_w_glu, ssm_b_glu, ffn_w_gate, ffn_w_up, ffn_w_down, moe_w_router, moe_w_gate, moe_w_up, moe_w_down):
    nbp, seqp, _ = x_prompt.shape
    nbs, seqs, _ = x_sample.shape
    tp = nbp * seqp
    ts = nbs * seqs
    row = lambda a: a.reshape(1, -1)
    g = lambda i, k: norm_g[i, k].reshape(1, D)

    nl = w_xk.shape[0]
    pk, pv = _mem_kv(mem_prompt, mem_norm_g.reshape(nl, 1, D), w_xk, w_xv)
    p_mem_k = pk.reshape(nl, nbp, N_MEM, N_HEADS, HEAD_DIM)
    p_mem_v = pv.reshape(nl, nbp, N_MEM, N_HEADS, HEAD_DIM)
    ck = cache_mem_k.reshape(nl * nbs, N_MEM, N_HEADS, HEAD_DIM)
    cv = cache_mem_v.reshape(nl * nbs, N_MEM, N_HEADS, HEAD_DIM)

    conv_args = (conv_w_dw[0], row(conv_b_dw[0]), row(conv_ln_g[0]), row(conv_ln_b[0]),
                 conv_w_pw2[0], row(conv_b_pw2[0]), g(0, 1))
    xp = x_prompt.reshape(tp, D)
    xs = x_sample.reshape(ts, D)
    up = _conv_pw1(xp, g(0, 0), conv_w_pw1[0], row(conv_b_pw1[0]), tm=512).reshape(nbp, seqp, D)
    us = _conv_pw1(xs, g(0, 0), conv_w_pw1[0], row(conv_b_pw1[0]), tm=ts).reshape(nbs, seqs, D)
    hist_p = jnp.zeros((nbp, HIST, D), F32)
    hist_s = jnp.pad(cache_conv[0], ((0, 0), (HIST - CONV_W + 1, 0), (0, 0)))
    xp = _conv_dw_pw2(up, x_prompt, hist_p, *conv_args, tl=256).reshape(tp, D)
    xs = _conv_dw_pw2(us, x_sample, hist_s, *conv_args, tl=seqs).reshape(ts, D)
    p_conv = up[:, seqp - (CONV_W - 1):][None]
    s_conv = jnp.concatenate([cache_conv[0], us], axis=1)[:, -(CONV_W - 1):][None]

    tma = 512
    xp = _attn(xp, pk, pv, 0, w_xq[0], w_xo[0], g(0, 2), g(0, 3), nbp, seqp, tma)
    xs = _attn(xs, ck, cv, 0, w_xq[0], w_xo[0], g(0, 2), g(0, 3), nbs, seqs, seqs)

    ffn_w = (_to_bf16(ffn_w_gate[0], 4), _to_bf16(ffn_w_up[0], 4), _to_bf16(ffn_w_down[0], 4))
    tff = ffn_w_gate.shape[2] // 2
    xp = _ffn(xp, g(0, 4), g(0, 5), *ffn_w, tm=512, tf=tff, out_shape=(tp, D), out_index=lambda i, f: (i, 0))
    xs = _ffn(xs, g(0, 4), g(0, 5), *ffn_w, tm=ts, tf=tff, out_shape=(ts, D), out_index=lambda i, f: (i, 0))

    (ab_r, ab_i, a2_r, a2_i), (bb_r, bb_i, abb_r, abb_i) = _ssm_prep(
        ssm_a_re[0], ssm_a_im[0], ssm_log_dt[0], ssm_b_re[0], ssm_b_im[0])
    bdiag = lambda m: _block_diag(m.transpose(0, 2, 1)).astype(BF16)
    bm_r, bm_i = bdiag(bb_r), bdiag(bb_i)
    cm_r, cm_i = bdiag(ssm_c_re[0]), bdiag(-ssm_c_im[0])
    tail_args = (cm_r, cm_i, row(ssm_d[0]), _to_bf16(ssm_w_glu[0], 4), row(ssm_b_glu[0]))
    half = SUBLANES // 2
    hdiag = lambda m: _block_diag(m.transpose(0, 2, 1), gpb=SSM_CB // SSM_C // 2).astype(BF16)
    pair_args = (jnp.concatenate([jnp.tile(ab_r, (half, 1)), jnp.tile(a2_r, (half, 1))]),
                 jnp.concatenate([jnp.tile(ab_i, (half, 1)), jnp.tile(a2_i, (half, 1))]),
                 jnp.concatenate([hdiag(bb_r), hdiag(abb_r)], axis=1),
                 jnp.concatenate([hdiag(bb_i), hdiag(abb_i)], axis=1))
    zero_state = jnp.zeros((SUBLANES, SSM_N), F32)
    xp, p_sr, p_si = _ssm(xp.reshape(nbp, seqp, D), g(1, 0), g(1, 1), zero_state, zero_state,
                          *pair_args, *tail_args, tl=64)
    xs, s_sr, s_si = _ssm(xs.reshape(nbs, seqs, D), g(1, 0), g(1, 1), state_ssm_re[0].reshape(nbs, SSM_N),
                          state_ssm_im[0].reshape(nbs, SSM_N), jnp.tile(ab_r, (nbs, 1)), jnp.tile(ab_i, (nbs, 1)),
                          bm_r, bm_i, *tail_args, tl=seqs)
    st = lambda a, n: a[:n].reshape(1, n, SSM_G, SSM_P)
    p_ssm_re, p_ssm_im = st(p_sr, nbp), st(p_si, nbp)
    s_ssm_re, s_ssm_im = st(s_sr, nbs), st(s_si, nbs)

    xp = _attn(xp.reshape(tp, D), pk, pv, nbp, w_xq[1], w_xo[1], g(1, 2), g(1, 3), nbp, seqp, tma)
    xs = _attn(xs.reshape(ts, D), ck, cv, nbs, w_xq[1], w_xo[1], g(1, 2), g(1, 3), nbs, seqs, seqs)

    yp, ysm = _moe(xp, xs, g(1, 4), g(1, 5), moe_w_router[0], moe_w_gate[0], moe_w_up[0], moe_w_down[0])
    return (yp.reshape(nbp, seqp, D), ysm.reshape(nbs, seqs, D), p_conv, p_mem_k, p_mem_v,
            p_ssm_re, p_ssm_im, s_conv, s_ssm_re, s_ssm_im)
```

```python
import functools
import math

import jax
import jax.numpy as jnp
from jax import lax
from jax.experimental import pallas as pl
from jax.experimental.pallas import tpu as pltpu

F32 = jnp.float32
BF16 = jnp.bfloat16
I32 = jnp.int32

D = 1024
CONV_W = 31
HIST = 32
N_MEM = 256
N_HEADS = 4
HEAD_DIM = D // N_HEADS
SSM_G = 64
SSM_C = 16
SSM_P = 64
SSM_N = SSM_G * SSM_P
SSM_CB = 256
SSM_NCB = D // SSM_CB
SSM_SB = SSM_CB // SSM_C * SSM_P
N_EXP = 8
EPS = 1e-6

V7X_VMEM_LIMIT = 56 * 1024 * 1024
SUBLANES = 8

CHUNK = 256
BLK = 512
GWIN = 128
GSLOTS = 4
CWIN = 128
BF16_ROWS = 16


def _params(sem, vmem=None):
    return pltpu.CompilerParams(dimension_semantics=sem, vmem_limit_bytes=vmem)


def _rms(x, g):
    return x * lax.rsqrt(jnp.mean(x * x, axis=-1, keepdims=True) + EPS) * g


def _const_spec(shape):
    nd = len(shape)
    return pl.BlockSpec(shape, lambda *_: (0,) * nd)


def _pw1_kernel(x_ref, g_ref, w_ref, b_ref, u_ref, wbf_ref):
    @pl.when(pl.program_id(0) == 0)
    def _():
        wbf_ref[...] = w_ref[...].astype(BF16)

    h = _rms(x_ref[...], g_ref[...]).astype(BF16)
    z = jnp.dot(h, wbf_ref[...], preferred_element_type=F32) + b_ref[...]
    u_ref[...] = z[:, :D] * jax.nn.sigmoid(z[:, D:])


def _conv_pw1(x, g, w, b, tm):
    t = x.shape[0]
    return pl.pallas_call(
        _pw1_kernel,
        out_shape=jax.ShapeDtypeStruct((t, D), F32),
        grid=(t // tm,),
        in_specs=[pl.BlockSpec((tm, D), lambda i: (i, 0)),
                  _const_spec((1, D)), _const_spec((D, 2 * D)), _const_spec((1, 2 * D))],
        out_specs=pl.BlockSpec((tm, D), lambda i: (i, 0)),
        scratch_shapes=[pltpu.VMEM((D, 2 * D), BF16)],
        compiler_params=_params(("arbitrary",), V7X_VMEM_LIMIT),
        name="conv_pw1",
    )(x, g, w, b)


def _conv2_kernel(u_ref, x_ref, hist_ref, wdw_ref, bdw_ref, lng_ref, lnb_ref, w2_ref, b2_ref,
                  g_ref, o_ref, ext_ref, sh_ref, conv_ref, w2bf_ref, wtap_ref, *, tl, rt):
    bi = pl.program_id(0)
    li = pl.program_id(1)

    @pl.when((bi == 0) & (li == 0))
    def _():
        w2bf_ref[...] = w2_ref[...].astype(BF16)
        for k in range(CONV_W):
            wtap_ref[k] = jnp.broadcast_to(wdw_ref[k:k + 1, :], (SUBLANES, D))

    @pl.when(li == 0)
    def _():
        ext_ref[0:HIST, :] = hist_ref[0]

    @pl.when(li > 0)
    def _():
        ext_ref[0:HIST, :] = ext_ref[tl:tl + HIST, :]

    ext_ref[HIST:HIST + tl, :] = u_ref[0]
    for s in range(1, SUBLANES):
        sh_ref[s - 1] = ext_ref[pl.ds(s, tl + HIST - SUBLANES), :]

    def rows(i, carry):
        r0 = pl.multiple_of(i * rt, rt)
        ntile = rt // SUBLANES
        accs = [jnp.zeros((SUBLANES, D), F32) + bdw_ref[...] for _ in range(ntile)]
        for k in range(CONV_W):
            off = HIST - CONV_W + 1 + k
            s = off % SUBLANES
            base = off - s
            w = wtap_ref[k]
            for ti in range(ntile):
                rows_ti = pl.ds(r0 + base + ti * SUBLANES, SUBLANES)
                src = ext_ref[rows_ti, :] if s == 0 else sh_ref[s - 1, rows_ti, :]
                accs[ti] = accs[ti] + w * src
        for ti in range(ntile):
            conv_ref[pl.ds(r0 + ti * SUBLANES, SUBLANES), :] = accs[ti]
        return carry

    lax.fori_loop(0, tl // rt, rows, 0)
    acc = conv_ref[...]
    mu = jnp.mean(acc, axis=-1, keepdims=True)
    xc = acc - mu
    var = jnp.mean(xc * xc, axis=-1, keepdims=True)
    y = xc * lax.rsqrt(var + EPS) * lng_ref[...] + lnb_ref[...]
    y = y * jax.nn.sigmoid(y)
    t = jnp.dot(y.astype(BF16), w2bf_ref[...], preferred_element_type=F32) + b2_ref[...]
    o_ref[0] = x_ref[0] + _rms(t, g_ref[...])


def _conv_dw_pw2(u, x, hist, wdw, bdw, lng, lnb, w2, b2, g, tl):
    nb, seq, _ = u.shape
    tok = pl.BlockSpec((1, tl, D), lambda b, l: (b, l, 0))
    return pl.pallas_call(
        functools.partial(_conv2_kernel, tl=tl, rt=min(tl, 32)),
        out_shape=jax.ShapeDtypeStruct((nb, seq, D), F32),
        grid=(nb, seq // tl),
        in_specs=[tok, tok, pl.BlockSpec((1, HIST, D), lambda b, l: (b, 0, 0)),
                  _const_spec((CONV_W, D)), _const_spec((1, D)), _const_spec((1, D)), _const_spec((1, D)),
                  _const_spec((D, D)), _const_spec((1, D)), _const_spec((1, D))],
        out_specs=tok,
        scratch_shapes=[pltpu.VMEM((tl + HIST, D), F32),
                        pltpu.VMEM((SUBLANES - 1, tl + HIST - SUBLANES, D), F32),
                        pltpu.VMEM((tl, D), F32), pltpu.VMEM((D, D), BF16),
                        pltpu.VMEM((CONV_W, SUBLANES, D), F32)],
        compiler_params=_params(("arbitrary", "arbitrary"), V7X_VMEM_LIMIT),
        name="conv_dw_pw2",
    )(u, x, hist, wdw, bdw, lng, lnb, w2, b2, g)


def _memkv_kernel(m_ref, g_ref, wk_ref, wv_ref, k_ref, v_ref, wkbf_ref, wvbf_ref):
    @pl.when(pl.program_id(1) == 0)
    def _():
        wkbf_ref[...] = wk_ref[0].astype(BF16)
        wvbf_ref[...] = wv_ref[0].astype(BF16)

    m = _rms(m_ref[0], g_ref[0]).astype(BF16)
    k = jnp.dot(m, wkbf_ref[...], preferred_element_type=F32)
    v = jnp.dot(m, wvbf_ref[...], preferred_element_type=F32)
    for hd in range(N_HEADS):
        sl = slice(hd * HEAD_DIM, (hd + 1) * HEAD_DIM)
        k_ref[0, :, hd, :] = k[:, sl]
        v_ref[0, :, hd, :] = v[:, sl]


def _mem_kv(mem, g, wk, wv):
    nb = mem.shape[0]
    nl = wk.shape[0]
    kv = jax.ShapeDtypeStruct((nl * nb, N_MEM, N_HEADS, HEAD_DIM), F32)
    out = pl.BlockSpec((1, N_MEM, N_HEADS, HEAD_DIM), lambda l, b: (l * nb + b, 0, 0, 0))
    w = pl.BlockSpec((1, D, D), lambda l, b: (l, 0, 0))
    return pl.pallas_call(
        _memkv_kernel,
        out_shape=(kv, kv),
        grid=(nl, nb),
        in_specs=[pl.BlockSpec((1, N_MEM, D), lambda l, b: (b, 0, 0)),
                  pl.BlockSpec((1, 1, D), lambda l, b: (l, 0, 0)), w, w],
        out_specs=(out, out),
        scratch_shapes=[pltpu.VMEM((D, D), BF16), pltpu.VMEM((D, D), BF16)],
        compiler_params=_params(("arbitrary", "arbitrary"), V7X_VMEM_LIMIT),
        name="mem_kv",
    )(mem, g, wk, wv)


def _attn_kernel(x_ref, k_ref, v_ref, wq_ref, wo_ref, g2_ref, g3_ref, o_ref, wqbf_ref, wobf_ref, kb_ref, vb_ref):
    @pl.when((pl.program_id(0) == 0) & (pl.program_id(1) == 0))
    def _():
        wqbf_ref[...] = wq_ref[...].astype(BF16)
        wobf_ref[...] = wo_ref[...].astype(BF16)

    @pl.when(pl.program_id(1) == 0)
    def _():
        for hd in range(N_HEADS):
            sl = slice(hd * HEAD_DIM, (hd + 1) * HEAD_DIM)
            kb_ref[:, sl] = k_ref[0, :, hd, :].astype(BF16)
            vb_ref[:, sl] = v_ref[0, :, hd, :].astype(BF16)

    x = x_ref[...]
    h = _rms(x, g2_ref[...]).astype(BF16)
    q = (jnp.dot(h, wqbf_ref[...], preferred_element_type=F32) * (HEAD_DIM ** -0.5)).astype(BF16)
    heads = []
    for hd in range(N_HEADS):
        sl = slice(hd * HEAD_DIM, (hd + 1) * HEAD_DIM)
        s = lax.dot_general(q[:, sl], kb_ref[:, sl], (((1,), (1,)), ((), ())), preferred_element_type=F32)
        p = jnp.exp(s - jnp.max(s, axis=-1, keepdims=True))
        p = p / jnp.sum(p, axis=-1, keepdims=True)
        heads.append(jnp.dot(p.astype(BF16), vb_ref[:, sl], preferred_element_type=F32))
    o = jnp.concatenate(heads, axis=1).astype(BF16)
    t = jnp.dot(o, wobf_ref[...], preferred_element_type=F32)
    o_ref[...] = x + _rms(t, g3_ref[...])


def _attn(x, k, v, kv_base, wq, wo, g2, g3, nb, seq, tm):
    tok = pl.BlockSpec((tm, D), lambda b, l: (b * (seq // tm) + l, 0))
    kv = pl.BlockSpec((1, N_MEM, N_HEADS, HEAD_DIM), lambda b, l: (kv_base + b, 0, 0, 0))
    return pl.pallas_call(
        _attn_kernel,
        out_shape=jax.ShapeDtypeStruct((nb * seq, D), F32),
        grid=(nb, seq // tm),
        in_specs=[tok, kv, kv, _const_spec((D, D)), _const_spec((D, D)), _const_spec((1, D)), _const_spec((1, D))],
        out_specs=tok,
        scratch_shapes=[pltpu.VMEM((D, D), BF16), pltpu.VMEM((D, D), BF16),
                        pltpu.VMEM((N_MEM, D), BF16), pltpu.VMEM((N_MEM, D), BF16)],
        compiler_params=_params(("arbitrary", "arbitrary"), V7X_VMEM_LIMIT),
        name="mem_attn",
    )(x, k, v, wq, wo, g2, g3)


def _cast_kernel(w_ref, o_ref):
    o_ref[...] = w_ref[...].astype(BF16)


def _to_bf16(w, nsplit):
    r, c = w.shape
    blk = pl.BlockSpec((r // nsplit, c), lambda i: (i, 0))
    return pl.pallas_call(
        _cast_kernel,
        out_shape=jax.ShapeDtypeStruct((r, c), BF16),
        grid=(nsplit,),
        in_specs=[blk],
        out_specs=blk,
        compiler_params=_params(("arbitrary",)),
        name="to_bf16",
    )(w)


def _ffn_kernel(x_ref, g4_ref, g5_ref, wg_ref, wu_ref, wd_ref, o_ref, h_ref, acc_ref):
    f = pl.program_id(1)

    @pl.when(f == 0)
    def _():
        h_ref[...] = _rms(x_ref[...], g4_ref[...]).astype(BF16)
        acc_ref[...] = jnp.zeros_like(acc_ref)

    h = h_ref[...]
    a = jnp.dot(h, wg_ref[...], preferred_element_type=F32)
    b = jnp.dot(h, wu_ref[...], preferred_element_type=F32)
    hh = (a * jax.nn.sigmoid(a) * b).astype(BF16)
    acc_ref[...] += jnp.dot(hh, wd_ref[...], preferred_element_type=F32)

    @pl.when(f == pl.num_programs(1) - 1)
    def _():
        o_ref[...] = x_ref[...] + _rms(acc_ref[...], g5_ref[...])


def _ffn(x, g4, g5, wg, wu, wd, tm, tf, out_shape, out_index):
    t = x.shape[0]
    ff = wg.shape[1]
    return pl.pallas_call(
        _ffn_kernel,
        out_shape=jax.ShapeDtypeStruct(out_shape, F32),
        grid=(t // tm, ff // tf),
        in_specs=[pl.BlockSpec((tm, D), lambda i, f: (i, 0)),
                  _const_spec((1, D)), _const_spec((1, D)),
                  pl.BlockSpec((D, tf), lambda i, f: (0, f)),
                  pl.BlockSpec((D, tf), lambda i, f: (0, f)),
                  pl.BlockSpec((tf, D), lambda i, f: (f, 0))],
        out_specs=pl.BlockSpec((tm, D), out_index),
        scratch_shapes=[pltpu.VMEM((tm, D), BF16), pltpu.VMEM((tm, D), F32)],
        compiler_params=_params(("arbitrary", "arbitrary"), V7X_VMEM_LIMIT),
        name="dense_ffn",
    )(x, g4, g5, wg, wu, wd)


def _ssm_prep_kernel(lr_ref, li_ref, ldt_ref, br_ref, bi_ref, abr_ref, abi_ref, bbr_ref, bbi_ref,
                     a2r_ref, a2i_ref, abbr_ref, abbi_ref):
    dt = jnp.exp(ldt_ref[...])
    lr = lr_ref[...]
    li = li_ref[...]
    mag = jnp.exp(lr * dt)
    ab_r = mag * jnp.cos(li * dt)
    ab_i = mag * jnp.sin(li * dt)
    den = lr * lr + li * li
    nr = ab_r - 1.0
    k_r = (nr * lr + ab_i * li) / den
    k_i = (ab_i * lr - nr * li) / den
    br = br_ref[...]
    bi = bi_ref[...]
    bb_r = k_r * br - k_i * bi
    bb_i = k_r * bi + k_i * br
    abr_ref[...] = ab_r
    abi_ref[...] = ab_i
    bbr_ref[...] = bb_r
    bbi_ref[...] = bb_i
    a2r_ref[...] = ab_r * ab_r - ab_i * ab_i
    a2i_ref[...] = 2.0 * (ab_r * ab_i)
    abbr_ref[...] = ab_r * bb_r - ab_i * bb_i
    abbi_ref[...] = ab_r * bb_i + ab_i * bb_r


def _ssm_prep(a_re, a_im, log_dt, b_re, b_im):
    n = SSM_P * SSM_C
    rep = lambda a: jnp.repeat(a, SSM_C, axis=1)
    shp = jax.ShapeDtypeStruct((SSM_G, n), F32)
    abr, abi, bbr, bbi, a2r, a2i, abbr, abbi = pl.pallas_call(
        _ssm_prep_kernel,
        out_shape=(shp,) * 8,
        name="ssm_prep",
    )(rep(a_re), rep(a_im), log_dt.reshape(SSM_G, 1), b_re.reshape(SSM_G, n), b_im.reshape(SSM_G, n))
    pick = lambda a: a.reshape(SSM_G, SSM_P, SSM_C)[:, :, 0].reshape(1, SSM_N)
    gpc = lambda a: a.reshape(SSM_G, SSM_P, SSM_C)
    return (pick(abr), pick(abi), pick(a2r), pick(a2i)), (gpc(bbr), gpc(bbi), gpc(abbr), gpc(abbi))


def _block_diag(m, gpb=SSM_CB // SSM_C):
    g, a, b = m.shape
    m = m.reshape(g // gpb, gpb, a, 1, b)
    eye = jnp.eye(gpb, dtype=m.dtype).reshape(1, gpb, 1, gpb, 1)
    return (m * eye).reshape(g // gpb, gpb * a, gpb * b)


def _ssm_kernel(x_ref, g0_ref, g1_ref, s0r_ref, s0i_ref, ar_ref, ai_ref, bmr_ref, bmi_ref, cmr_ref, cmi_ref,
                d_ref, wglu_ref, bglu_ref, o_ref, sr_ref, si_ref,
                bur_ref, bui_ref, *, nb, rb, lc):
    @pl.when(pl.program_id(0) == 0)
    def _():
        sr_ref[...] = s0r_ref[...]
        si_ref[...] = s0i_ref[...]

    tl = rb // nb
    x = x_ref[...].reshape(rb, D)
    h = _rms(x, g0_ref[...])
    r = lax.broadcasted_iota(I32, (rb, rb), 0)
    c = lax.broadcasted_iota(I32, (rb, rb), 1)
    to_time_major = (c == (r & (nb - 1)) * tl + (r >> (nb.bit_length() - 1))).astype(BF16)
    to_batch_major = (c == (r & (tl - 1)) * nb + (r >> (tl.bit_length() - 1))).astype(BF16)
    hb = jnp.dot(to_time_major, h.astype(BF16), preferred_element_type=F32).astype(BF16)
    rows = max(nb, SUBLANES)
    pair = nb < SUBLANES
    if pair:
        odd = ((r >> (nb.bit_length() - 1)) & 1) == 1
        prev_time_major = ((c == (r & (nb - 1)) * tl + (r >> (nb.bit_length() - 1)) - 1) & odd).astype(BF16)
        hprev = jnp.dot(prev_time_major, h.astype(BF16), preferred_element_type=F32).astype(BF16)
        low = lax.broadcasted_iota(I32, (SUBLANES, lc), 0) < nb

    def scan_chunk(cs):
        a_r = ar_ref[:, cs]
        a_i = ai_ref[:, cs]
        s_r = sr_ref[:, cs]
        s_i = si_ref[:, cs]
        for j in range(rb // rows):
            rs = slice(j * rows, (j + 1) * rows)
            n_r = a_r * s_r - a_i * s_i + bur_ref[rs, cs]
            n_i = a_r * s_i + a_i * s_r + bui_ref[rs, cs]
            bur_ref[rs, cs] = n_r
            bui_ref[rs, cs] = n_i
            if pair:
                s_r = jnp.where(low, pltpu.roll(n_r, nb, 0), n_r)
                s_i = jnp.where(low, pltpu.roll(n_i, nb, 0), n_i)
            else:
                s_r, s_i = n_r, n_i
        sr_ref[:, cs] = s_r
        si_ref[:, cs] = s_i

    ist = bmr_ref.shape[2]
    ich = bmr_ref.shape[1] // 2 if pair else bmr_ref.shape[1]
    ys = []
    for cb in range(SSM_NCB):
        ss = slice(cb * SSM_SB, (cb + 1) * SSM_SB)
        for ib in range(cb * (SSM_SB // ist), (cb + 1) * (SSM_SB // ist)):
            hs = hb[:, ib * ich:(ib + 1) * ich]
            if pair:
                hs = jnp.concatenate([hs, hprev[:, ib * ich:(ib + 1) * ich]], axis=1)
            bur_ref[:, ib * ist:(ib + 1) * ist] = jnp.dot(hs, bmr_ref[ib], preferred_element_type=F32)
            bui_ref[:, ib * ist:(ib + 1) * ist] = jnp.dot(hs, bmi_ref[ib], preferred_element_type=F32)
            for c in range(ist // lc):
                scan_chunk(slice(ib * ist + c * lc, ib * ist + (c + 1) * lc))
        ys.append(jnp.dot(bur_ref[:, ss].astype(BF16), cmr_ref[cb], preferred_element_type=F32)
                  + jnp.dot(bui_ref[:, ss].astype(BF16), cmi_ref[cb], preferred_element_type=F32))
    y_tm = jnp.concatenate(ys, axis=1)
    y1 = y_tm.astype(BF16)
    rem = y_tm - y1.astype(F32)
    y2 = rem.astype(BF16)
    y3 = (rem - y2.astype(F32)).astype(BF16)
    y = (jnp.dot(to_batch_major, y1, preferred_element_type=F32)
         + jnp.dot(to_batch_major, y2, preferred_element_type=F32)
         + jnp.dot(to_batch_major, y3, preferred_element_type=F32))
    y = y + d_ref[...] * h
    y = jax.nn.gelu(y).astype(BF16)
    z = jnp.dot(y, wglu_ref[...], preferred_element_type=F32) + bglu_ref[...]
    t = z[:, :D] * jax.nn.sigmoid(z[:, D:])
    o_ref[...] = (x + _rms(t, g1_ref[...])).reshape(nb, tl, D)


def _ssm(x, g0, g1, s0r, s0i, ab_r, ab_i, bm_r, bm_i, cm_r, cm_i, d, wglu, bglu, tl):
    nb, seq, _ = x.shape
    rb = nb * tl
    rows = max(nb, SUBLANES)
    st = jax.ShapeDtypeStruct((rows, SSM_N), F32)
    row = pl.BlockSpec((nb, tl, D), lambda i: (0, i, 0))
    return pl.pallas_call(
        functools.partial(_ssm_kernel, nb=nb, rb=rb, lc=512),
        out_shape=(jax.ShapeDtypeStruct((nb, seq, D), F32), st, st),
        grid=(seq // tl,),
        in_specs=[row, _const_spec((1, D)), _const_spec((1, D)),
                  _const_spec((rows, SSM_N)), _const_spec((rows, SSM_N)),
                  _const_spec((rows, SSM_N)), _const_spec((rows, SSM_N)),
                  _const_spec(bm_r.shape), _const_spec(bm_i.shape),
                  _const_spec((SSM_NCB, SSM_SB, SSM_CB)), _const_spec((SSM_NCB, SSM_SB, SSM_CB)),
                  _const_spec((1, D)), _const_spec((D, 2 * D)), _const_spec((1, 2 * D))],
        out_specs=(row, _const_spec((rows, SSM_N)), _const_spec((rows, SSM_N))),
        scratch_shapes=[pltpu.VMEM((rb, SSM_N), F32), pltpu.VMEM((rb, SSM_N), F32)],
        compiler_params=_params(("arbitrary",), V7X_VMEM_LIMIT),
        name="ssm",
    )(x, g0, g1, s0r, s0i, ab_r, ab_i, bm_r, bm_i, cm_r, cm_i, d, wglu, bglu)


def _route_kernel(xp_ref, xs_ref, g_ref, wrt_ref, h_ref, rk_ref, rkt_ref, cwt_ref, cnt_ref, carry_ref,
                  *, n_prompt_chunks):
    i = pl.program_id(0)

    @pl.when(i == 0)
    def _():
        carry_ref[...] = jnp.zeros_like(carry_ref)

    x = jnp.where(i < n_prompt_chunks, xp_ref[...], xs_ref[...])
    h = _rms(x, g_ref[...])
    h_ref[...] = h.astype(BF16)
    lg = lax.dot_general(wrt_ref[...], h, (((1,), (1,)), ((), ())), preferred_element_type=F32,
                         precision=lax.Precision.HIGHEST)
    ex = lax.broadcasted_iota(I32, lg.shape, 0)
    m1 = jnp.max(lg, axis=0, keepdims=True)
    i1 = jnp.min(jnp.where(lg == m1, ex, N_EXP), axis=0, keepdims=True)
    first = ex == i1
    lg2 = jnp.where(first, -jnp.inf, lg)
    m2 = jnp.max(lg2, axis=0, keepdims=True)
    i2 = jnp.min(jnp.where(lg2 == m2, ex, N_EXP), axis=0, keepdims=True)
    second = ex == i2
    e = jnp.exp(m2 - m1)
    den = 1.0 + e
    cwt_ref[...] = jnp.where(first, 1.0 / den, 0.0) + jnp.where(second, e / den, 0.0)
    assigned = first | second
    r = lax.broadcasted_iota(I32, (CHUNK, CHUNK), 0)
    c = lax.broadcasted_iota(I32, (CHUNK, CHUNK), 1)
    earlier = (r < c).astype(BF16)
    rank = jnp.dot(assigned.astype(BF16), earlier, preferred_element_type=F32) + carry_ref[...]
    rank = jnp.where(assigned, rank, -1.0)
    rkt_ref[...] = rank.astype(I32)
    rk_ref[...] = rank.T.astype(I32)
    carry_ref[...] += jnp.sum(assigned.astype(F32), axis=1, keepdims=True)
    cnt_ref[0] = carry_ref[...]


def _route(xp, xs, g, wrt):
    npc = xp.shape[0] // CHUNK
    nch = npc + xs.shape[0] // CHUNK
    t = nch * CHUNK
    exp_major = pl.BlockSpec((N_EXP, CHUNK), lambda i: (0, i))
    return pl.pallas_call(
        functools.partial(_route_kernel, n_prompt_chunks=npc),
        out_shape=(jax.ShapeDtypeStruct((t, D), BF16), jax.ShapeDtypeStruct((t, N_EXP), I32),
                   jax.ShapeDtypeStruct((N_EXP, t), I32), jax.ShapeDtypeStruct((N_EXP, t), F32),
                   jax.ShapeDtypeStruct((nch, N_EXP, 1), F32)),
        grid=(nch,),
        in_specs=[pl.BlockSpec((CHUNK, D), lambda i: (jnp.minimum(i, npc - 1), 0)),
                  pl.BlockSpec((CHUNK, D), lambda i: (jnp.maximum(i - npc, 0), 0)),
                  _const_spec((1, D)), _const_spec((N_EXP, D))],
        out_specs=(pl.BlockSpec((CHUNK, D), lambda i: (i, 0)),
                   pl.BlockSpec((CHUNK, N_EXP), lambda i: (i, 0)), exp_major, exp_major,
                   pl.BlockSpec((1, N_EXP, 1), lambda i: (i, 0, 0))),
        scratch_shapes=[pltpu.VMEM((N_EXP, 1), F32)],
        compiler_params=_params(("arbitrary",)),
        name="moe_route",
    )(xp, xs, g, wrt)


def _gather_kernel(be_ref, pb_ref, pn_ref, r0_ref, bef_ref, aft_ref, pc_ref, pe_ref, pt_ref,
                   h_hbm, pos_hbm, cw_hbm, o_ref, gs_ref, hbuf, pbuf, cbuf, acc_ref, gacc_ref, sem):
    d = pl.program_id(0)
    e = be_ref[d]
    base = pb_ref[d]
    r0 = r0_ref[d]
    total = pt_ref[0]

    def copies(idx):
        slot = idx % GSLOTS
        c = pc_ref[idx]
        ee = pe_ref[idx]
        return (pltpu.make_async_copy(h_hbm.at[pl.ds(c * CHUNK, CHUNK)], hbuf.at[slot], sem.at[0, slot]),
                pltpu.make_async_copy(pos_hbm.at[ee, c], pbuf.at[slot], sem.at[1, slot]),
                pltpu.make_async_copy(cw_hbm.at[ee, c], cbuf.at[slot], sem.at[2, slot]))

    def start(idx):
        @pl.when(idx < total)
        def _():
            for cp in copies(idx):
                cp.start()

    @pl.when(d == 0)
    def _():
        for k in range(GSLOTS - 1):
            start(k)

    acc_ref[...] = jnp.zeros_like(acc_ref)
    gacc_ref[...] = jnp.zeros_like(gacc_ref)
    row = d * BLK + lax.broadcasted_iota(I32, (GWIN, CHUNK), 0)
    row2 = d * BLK + lax.broadcasted_iota(I32, (2 * GWIN, CHUNK), 0)

    def body(j, carry):
        idx = base + j
        slot = idx % GSLOTS
        for cp in copies(idx):
            cp.wait()
        start(idx + GSLOTS - 1)

        c = pc_ref[idx]
        first = jnp.clip(bef_ref[c * N_EXP + e] - r0, 0, BLK)
        last = jnp.clip(aft_ref[c * N_EXP + e] - r0, 0, BLK)

        def fill(w, rows_at):
            w0 = pl.multiple_of(w * GWIN, GWIN)
            n = rows_at.shape[0]
            hit = pbuf[slot] == rows_at + w0
            acc_ref[pl.ds(w0, n), :] += jnp.dot(hit.astype(BF16), hbuf[slot], preferred_element_type=F32)
            gacc_ref[pl.ds(w0, n), :] += jnp.sum(jnp.where(hit, cbuf[slot], 0.0), axis=1, keepdims=True)

        w_first = jnp.minimum(first // GWIN, BLK // GWIN - 2)
        fill(w_first, row2)

        def window(w, c2):
            fill(w, row)
            return c2

        lax.fori_loop(w_first + 2, (last + GWIN - 1) // GWIN, window, 0)
        return carry

    lax.fori_loop(0, pn_ref[d], body, 0)
    o_ref[...] = acc_ref[...].astype(BF16)
    gs_ref[...] = gacc_ref[...]


def _gather(h, pos_t, cw_t, blk_e, pbase, pcount, r0, before, after, pair_c, pair_e, ptotal, nblk):
    hbm = pl.BlockSpec(memory_space=pl.ANY)
    return pl.pallas_call(
        _gather_kernel,
        out_shape=(jax.ShapeDtypeStruct((nblk * BLK, D), BF16), jax.ShapeDtypeStruct((nblk * BLK, 1), F32)),
        grid_spec=pltpu.PrefetchScalarGridSpec(
            num_scalar_prefetch=9,
            grid=(nblk,),
            in_specs=[hbm, hbm, hbm],
            out_specs=(pl.BlockSpec((BLK, D), lambda d, *_: (d, 0)), pl.BlockSpec((BLK, 1), lambda d, *_: (d, 0))),
            scratch_shapes=[pltpu.VMEM((GSLOTS, CHUNK, D), BF16), pltpu.VMEM((GSLOTS, 1, CHUNK), I32),
                            pltpu.VMEM((GSLOTS, 1, CHUNK), F32), pltpu.VMEM((BLK, D), F32),
                            pltpu.VMEM((BLK, 1), F32), pltpu.SemaphoreType.DMA((3, GSLOTS))]),
        compiler_params=_params(("arbitrary",)),
        name="moe_gather",
    )(blk_e, pbase, pcount, r0, before, after, pair_c, pair_e, ptotal, h, pos_t, cw_t)


def _expert_up_kernel(be_ref, nv_ref, x_ref, wg_ref, wu_ref, o_ref, wgbf_ref, wubf_ref):
    d = pl.program_id(1)
    changed = (d == 0) | (be_ref[d] != be_ref[jnp.maximum(d - 1, 0)])

    @pl.when(changed)
    def _():
        wgbf_ref[...] = wg_ref[0].astype(BF16)
        wubf_ref[...] = wu_ref[0].astype(BF16)

    def up(rows):
        x = x_ref[rows, :]
        a = jnp.dot(x, wgbf_ref[...], preferred_element_type=F32)
        b = jnp.dot(x, wubf_ref[...], preferred_element_type=F32)
        o_ref[rows, :] = (a * jax.nn.sigmoid(a) * b).astype(BF16)

    half = BLK // 2
    nrows = nv_ref[d]

    @pl.when(nrows > half)
    def _():
        up(slice(0, BLK))

    @pl.when((nrows > 0) & (nrows <= half))
    def _():
        up(slice(0, half))
        o_ref[half:, :] = jnp.zeros((BLK - half, o_ref.shape[1]), BF16)

    @pl.when(nrows == 0)
    def _():
        o_ref[...] = jnp.zeros_like(o_ref)


def _expert_up(xs, blk_e, nvalid, wg, wu, tf):
    p = xs.shape[0]
    ff = wg.shape[2]
    return pl.pallas_call(
        _expert_up_kernel,
        out_shape=jax.ShapeDtypeStruct((p, ff), BF16),
        grid_spec=pltpu.PrefetchScalarGridSpec(
            num_scalar_prefetch=2,
            grid=(ff // tf, p // BLK),
            in_specs=[pl.BlockSpec((BLK, D), lambda f, d, be, nv: (d, 0)),
                      pl.BlockSpec((1, D, tf), lambda f, d, be, nv: (be[d], 0, f)),
                      pl.BlockSpec((1, D, tf), lambda f, d, be, nv: (be[d], 0, f))],
            out_specs=pl.BlockSpec((BLK, tf), lambda f, d, be, nv: (d, f)),
            scratch_shapes=[pltpu.VMEM((D, tf), BF16), pltpu.VMEM((D, tf), BF16)]),
        compiler_params=_params(("arbitrary", "arbitrary"), V7X_VMEM_LIMIT),
        name="moe_up",
    )(blk_e, nvalid, xs, wg, wu)


def _expert_down_kernel(be_ref, nv_ref, h_ref, gs_ref, wd_ref, o_ref, wdbf_ref):
    d = pl.program_id(1)
    changed = (d == 0) | (be_ref[d] != be_ref[jnp.maximum(d - 1, 0)])

    @pl.when(changed)
    def _():
        wdbf_ref[...] = wd_ref[0].astype(BF16)

    def down(rows):
        y = jnp.dot(h_ref[rows, :], wdbf_ref[...], preferred_element_type=F32)
        o_ref[rows, :] = (y * gs_ref[rows, :]).astype(BF16)

    half = BLK // 2
    nrows = nv_ref[d]

    @pl.when(nrows > half)
    def _():
        down(slice(0, BLK))

    @pl.when((nrows > 0) & (nrows <= half))
    def _():
        down(slice(0, half))
        o_ref[half:, :] = jnp.zeros((BLK - half, o_ref.shape[1]), BF16)

    @pl.when(nrows == 0)
    def _():
        o_ref[...] = jnp.zeros_like(o_ref)


def _expert_down(hh, gsort, blk_e, nvalid, wd, tn):
    p, ff = hh.shape
    return pl.pallas_call(
        _expert_down_kernel,
        out_shape=jax.ShapeDtypeStruct((p, D), BF16),
        grid_spec=pltpu.PrefetchScalarGridSpec(
            num_scalar_prefetch=2,
            grid=(D // tn, p // BLK),
            in_specs=[pl.BlockSpec((BLK, ff), lambda n, d, be, nv: (d, 0)),
                      pl.BlockSpec((BLK, 1), lambda n, d, be, nv: (d, 0)),
                      pl.BlockSpec((1, ff, tn), lambda n, d, be, nv: (be[d], 0, n))],
            out_specs=pl.BlockSpec((BLK, tn), lambda n, d, be, nv: (d, n)),
            scratch_shapes=[pltpu.VMEM((ff, tn), BF16)]),
        compiler_params=_params(("arbitrary", "arbitrary"), V7X_VMEM_LIMIT),
        name="moe_down",
    )(blk_e, nvalid, hh, gsort, wd)


def _combine_kernel(rs_ref, rc_ref, xp_ref, xs_ref, pos_ref, g_ref, y_hbm, op_ref, os_ref, wbuf, obuf, acc_ref,
                    sem, osem, *, n_prompt_chunks):
    i = pl.program_id(0)
    nch = pl.num_programs(0)

    def window_start(chunk, e):
        return pl.multiple_of(rs_ref[chunk * N_EXP + e] // BF16_ROWS * BF16_ROWS, BF16_ROWS)

    def copy(chunk, e, slot):
        return pltpu.make_async_copy(y_hbm.at[pl.ds(window_start(chunk, e), CWIN)],
                                     wbuf.at[slot, pl.ds(e * CWIN, CWIN)], sem.at[slot, e])

    @pl.when(i == 0)
    def _():
        for e in range(N_EXP):
            copy(0, e, 0).start()

    slot = i % 2

    @pl.when(i + 1 < nch)
    def _():
        for e in range(N_EXP):
            copy(i + 1, e, 1 - slot).start()

    col = lax.broadcasted_iota(I32, (CHUNK, CWIN), 1)
    hits = []
    for e in range(N_EXP):
        copy(i, e, slot).wait()
        hits.append((pos_ref[:, e:e + 1] - window_start(i, e)) == col)
    onehot = jnp.concatenate(hits, axis=1).astype(BF16)
    acc_ref[...] = jnp.dot(onehot, wbuf[slot], preferred_element_type=F32)

    for e in range(N_EXP):
        for k in range(1, (CHUNK + BF16_ROWS - 1) // CWIN + 1):
            tail = window_start(i, e) + k * CWIN

            @pl.when(rs_ref[i * N_EXP + e] + rc_ref[i * N_EXP + e] > tail)
            def _():
                cp = pltpu.make_async_copy(y_hbm.at[pl.ds(tail, CWIN)], obuf, osem)
                cp.start()
                cp.wait()
                hit = ((pos_ref[:, e:e + 1] - tail) == col).astype(BF16)
                acc_ref[...] += jnp.dot(hit, obuf[...], preferred_element_type=F32)

    x = jnp.where(i < n_prompt_chunks, xp_ref[...], xs_ref[...])
    out = x + _rms(acc_ref[...], g_ref[...])

    @pl.when(i < n_prompt_chunks)
    def _():
        op_ref[...] = out

    @pl.when(i >= n_prompt_chunks)
    def _():
        os_ref[...] = out


def _combine(run_start, run_count, xp, xs, pos, g, ys):
    npc = xp.shape[0] // CHUNK
    nsc = xs.shape[0] // CHUNK
    nch = npc + nsc
    tok8 = pl.BlockSpec((CHUNK, N_EXP), lambda i, *_: (i, 0))
    pspec = pl.BlockSpec((CHUNK, D), lambda i, *_: (jnp.minimum(i, npc - 1), 0))
    sspec = pl.BlockSpec((CHUNK, D), lambda i, *_: (jnp.maximum(i - npc, 0), 0))
    return pl.pallas_call(
        functools.partial(_combine_kernel, n_prompt_chunks=npc),
        out_shape=(jax.ShapeDtypeStruct(xp.shape, F32), jax.ShapeDtypeStruct(xs.shape, F32)),
        grid_spec=pltpu.PrefetchScalarGridSpec(
            num_scalar_prefetch=2,
            grid=(nch,),
            in_specs=[pspec, sspec, tok8, pl.BlockSpec((1, D), lambda i, *_: (0, 0)),
                      pl.BlockSpec(memory_space=pl.ANY)],
            out_specs=(pspec, sspec),
            scratch_shapes=[pltpu.VMEM((2, N_EXP * CWIN, D), BF16), pltpu.VMEM((CWIN, D), BF16),
                            pltpu.VMEM((CHUNK, D), F32),
                            pltpu.SemaphoreType.DMA((2, N_EXP)), pltpu.SemaphoreType.DMA]),
        compiler_params=_params(("arbitrary",), V7X_VMEM_LIMIT),
        name="moe_combine",
    )(run_start, run_count, xp, xs, pos, g, ys)


def _moe(xp, xs, g4, g5, w_router, wg, wu, wd):
    npc = xp.shape[0] // CHUNK
    nch = npc + xs.shape[0] // CHUNK
    t = nch * CHUNK
    h, rk, rk_t, cw_t, cnt = _route(xp, xs, g4, w_router.T)

    after = cnt.reshape(nch, N_EXP).astype(I32)
    before = jnp.concatenate([jnp.zeros((1, N_EXP), I32), after[:-1]], axis=0)
    counts = after[-1]
    gsz = (counts + BLK - 1) // BLK * BLK
    gend = jnp.cumsum(gsz)
    gstart = gend - gsz
    nblk = (2 * t + N_EXP * (BLK - 1)) // BLK + 1 + (2 * CWIN + BLK - 1) // BLK
    bstart = jnp.arange(nblk, dtype=I32) * BLK
    blk_e = jnp.minimum(jnp.sum(gend[None, :] <= bstart[:, None], axis=1), N_EXP - 1).astype(I32)
    valid = bstart < gend[-1]
    r0 = bstart - gstart[blk_e]
    aft_e = after[:, blk_e]
    bef_e = before[:, blk_e]
    c_lo = jnp.where(valid, jnp.sum(aft_e <= r0[None, :], axis=0), 0).astype(I32)
    c_hi = jnp.where(valid, jnp.sum(bef_e < (r0 + BLK)[None, :], axis=0) - 1, -1).astype(I32)
    c_lo = jnp.minimum(c_lo, nch - 1)
    pos = jnp.where(rk >= 0, rk + gstart[None, :], -1).astype(I32)
    pos_t = jnp.where(rk_t >= 0, rk_t + gstart[:, None], -1).astype(I32).reshape(N_EXP, nch, 1, CHUNK)
    cw_t = cw_t.reshape(N_EXP, nch, 1, CHUNK)
    run_start = (gstart[None, :] + before).astype(I32).reshape(-1)
    run_count = (after - before).reshape(-1)
    nvalid = jnp.where(valid, jnp.clip(counts[blk_e] - r0, 0, BLK), 0).astype(I32)
    pcount = jnp.maximum(c_hi - c_lo + 1, 0)
    pend = jnp.cumsum(pcount)
    pbase = pend - pcount
    k = jnp.arange(nblk + 2 * nch * N_EXP, dtype=I32)
    pair_blk = jnp.minimum(jnp.sum(pend[None, :] <= k[:, None], axis=1), nblk - 1)
    pair_c = jnp.clip(c_lo[pair_blk] + k - pbase[pair_blk], 0, nch - 1).astype(I32)
    pair_e = blk_e[pair_blk]

    xsort, gsort = _gather(h, pos_t, cw_t, blk_e, pbase.astype(I32), pcount.astype(I32), r0.astype(I32),
                           before.reshape(-1), after.reshape(-1), pair_c, pair_e, pend[-1:].astype(I32), nblk)
    hh = _expert_up(xsort, blk_e, nvalid, wg, wu, tf=1792)
    ysort = _expert_down(hh, gsort, blk_e, nvalid, wd, tn=D)
    return _combine(run_start, run_count, xp, xs, pos, g5, ysort)


def kernel(x_prompt, x_sample, cache_conv, cache_mem_k, cache_mem_v, state_ssm_re, state_ssm_im, mem_prompt, norm_g, mem_norm_g, w_xq, w_xk, w_xv, w_xo, conv_w_pw1, conv_b_pw1, conv_w_dw, conv_b_dw, conv_ln_g, conv_ln_b, conv_w_pw2, conv_b_pw2, ssm_a_re, ssm_a_im, ssm_log_dt, ssm_b_re, ssm_b_im, ssm_c_re, ssm_c_im, ssm_d, ssm_w_glu, ssm_b_glu, ffn_w_gate, ffn_w_up, ffn_w_down, moe_w_router, moe_w_gate, moe_w_up, moe_w_down):
    nbp, seqp, _ = x_prompt.shape
    nbs, seqs, _ = x_sample.shape
    tp = nbp * seqp
    ts = nbs * seqs
    row = lambda a: a.reshape(1, -1)
    g = lambda i, k: norm_g[i, k].reshape(1, D)

    nl = w_xk.shape[0]
    pk, pv = _mem_kv(mem_prompt, mem_norm_g.reshape(nl, 1, D), w_xk, w_xv)
    p_mem_k = pk.reshape(nl, nbp, N_MEM, N_HEADS, HEAD_DIM)
    p_mem_v = pv.reshape(nl, nbp, N_MEM, N_HEADS, HEAD_DIM)
    ck = cache_mem_k.reshape(nl * nbs, N_MEM, N_HEADS, HEAD_DIM)
    cv = cache_mem_v.reshape(nl * nbs, N_MEM, N_HEADS, HEAD_DIM)

    conv_args = (conv_w_dw[0], row(conv_b_dw[0]), row(conv_ln_g[0]), row(conv_ln_b[0]),
                 conv_w_pw2[0], row(conv_b_pw2[0]), g(0, 1))
    xp = x_prompt.reshape(tp, D)
    xs = x_sample.reshape(ts, D)
    up = _conv_pw1(xp, g(0, 0), conv_w_pw1[0], row(conv_b_pw1[0]), tm=512).reshape(nbp, seqp, D)
    us = _conv_pw1(xs, g(0, 0), conv_w_pw1[0], row(conv_b_pw1[0]), tm=ts).reshape(nbs, seqs, D)
    hist_p = jnp.zeros((nbp, HIST, D), F32)
    hist_s = jnp.pad(cache_conv[0], ((0, 0), (HIST - CONV_W + 1, 0), (0, 0)))
    xp = _conv_dw_pw2(up, x_prompt, hist_p, *conv_args, tl=256).reshape(tp, D)
    xs = _conv_dw_pw2(us, x_sample, hist_s, *conv_args, tl=seqs).reshape(ts, D)
    p_conv = up[:, seqp - (CONV_W - 1):][None]
    s_conv = jnp.concatenate([cache_conv[0], us], axis=1)[:, -(CONV_W - 1):][None]

    tma = 512
    xp = _attn(xp, pk, pv, 0, w_xq[0], w_xo[0], g(0, 2), g(0, 3), nbp, seqp, tma)
    xs = _attn(xs, ck, cv, 0, w_xq[0], w_xo[0], g(0, 2), g(0, 3), nbs, seqs, seqs)

    ffn_w = (_to_bf16(ffn_w_gate[0], 4), _to_bf16(ffn_w_up[0], 4), _to_bf16(ffn_w_down[0], 4))
    tff = ffn_w_gate.shape[2] // 2
    xp = _ffn(xp, g(0, 4), g(0, 5), *ffn_w, tm=512, tf=tff, out_shape=(tp, D), out_index=lambda i, f: (i, 0))
    xs = _ffn(xs, g(0, 4), g(0, 5), *ffn_w, tm=ts, tf=tff, out_shape=(ts, D), out_index=lambda i, f: (i, 0))

    (ab_r, ab_i, a2_r, a2_i), (bb_r, bb_i, abb_r, abb_i) = _ssm_prep(
        ssm_a_re[0], ssm_a_im[0], ssm_log_dt[0], ssm_b_re[0], ssm_b_im[0])
    bdiag = lambda m: _block_diag(m.transpose(0, 2, 1)).astype(BF16)
    bm_r, bm_i = bdiag(bb_r), bdiag(bb_i)
    cm_r, cm_i = bdiag(ssm_c_re[0]), bdiag(-ssm_c_im[0])
    tail_args = (cm_r, cm_i, row(ssm_d[0]), _to_bf16(ssm_w_glu[0], 4), row(ssm_b_glu[0]))
    half = SUBLANES // 2
    hdiag = lambda m: _block_diag(m.transpose(0, 2, 1), gpb=SSM_CB // SSM_C // 2).astype(BF16)
    pair_args = (jnp.concatenate([jnp.tile(ab_r, (half, 1)), jnp.tile(a2_r, (half, 1))]),
                 jnp.concatenate([jnp.tile(ab_i, (half, 1)), jnp.tile(a2_i, (half, 1))]),
                 jnp.concatenate([hdiag(bb_r), hdiag(abb_r)], axis=1),
                 jnp.concatenate([hdiag(bb_i), hdiag(abb_i)], axis=1))
    zero_state = jnp.zeros((SUBLANES, SSM_N), F32)
    xp, p_sr, p_si = _ssm(xp.reshape(nbp, seqp, D), g(1, 0), g(1, 1), zero_state, zero_state,
                          *pair_args, *tail_args, tl=64)
    xs, s_sr, s_si = _ssm(xs.reshape(nbs, seqs, D), g(1, 0), g(1, 1), state_ssm_re[0].reshape(nbs, SSM_N),
                          state_ssm_im[0].reshape(nbs, SSM_N), jnp.tile(ab_r, (nbs, 1)), jnp.tile(ab_i, (nbs, 1)),
                          bm_r, bm_i, *tail_args, tl=seqs)
    st = lambda a, n: a[:n].reshape(1, n, SSM_G, SSM_P)
    p_ssm_re, p_ssm_im = st(p_sr, nbp), st(p_si, nbp)
    s_ssm_re, s_ssm_im = st(s_sr, nbs), st(s_si, nbs)

    xp = _attn(xp.reshape(tp, D), pk, pv, nbp, w_xq[1], w_xo[1], g(1, 2), g(1, 3), nbp, seqp, tma)
    xs = _attn(xs.reshape(ts, D), ck, cv, nbs, w_xq[1], w_xo[1], g(1, 2), g(1, 3), nbs, seqs, seqs)

    yp, ysm = _moe(xp, xs, g(1, 4), g(1, 5), moe_w_router[0], moe_w_gate[0], moe_w_up[0], moe_w_down[0])
    return (yp.reshape(nbp, seqp, D), ysm.reshape(nbs, seqs, D), p_conv, p_mem_k, p_mem_v,
            p_ssm_re, p_ssm_im, s_conv, s_ssm_re, s_ssm_im)
```

```python
import functools
import math

import jax
import jax.numpy as jnp
from jax import lax
from jax.experimental import pallas as pl
from jax.experimental.pallas import tpu as pltpu

F32 = jnp.float32
BF16 = jnp.bfloat16
I32 = jnp.int32

D = 1024
CONV_W = 31
HIST = 32
N_MEM = 256
N_HEADS = 4
HEAD_DIM = D // N_HEADS
SSM_G = 64
SSM_C = 16
SSM_P = 64
SSM_N = SSM_G * SSM_P
SSM_CB = 256
SSM_NCB = D // SSM_CB
SSM_SB = SSM_CB // SSM_C * SSM_P
N_EXP = 8
EPS = 1e-6

V7X_VMEM_LIMIT = 56 * 1024 * 1024
SUBLANES = 8

CHUNK = 256
BLK = 512
GWIN = 128
GSLOTS = 4
CWIN = 256
BF16_ROWS = 16


def _params(sem, vmem=None):
    return pltpu.CompilerParams(dimension_semantics=sem, vmem_limit_bytes=vmem)


def _rms(x, g):
    return x * lax.rsqrt(jnp.mean(x * x, axis=-1, keepdims=True) + EPS) * g


def _const_spec(shape):
    nd = len(shape)
    return pl.BlockSpec(shape, lambda *_: (0,) * nd)


def _pw1_kernel(x_ref, g_ref, w_ref, b_ref, u_ref, wbf_ref):
    @pl.when(pl.program_id(0) == 0)
    def _():
        wbf_ref[...] = w_ref[...].astype(BF16)

    h = _rms(x_ref[...], g_ref[...]).astype(BF16)
    z = jnp.dot(h, wbf_ref[...], preferred_element_type=F32) + b_ref[...]
    u_ref[...] = z[:, :D] * jax.nn.sigmoid(z[:, D:])


def _conv_pw1(x, g, w, b, tm):
    t = x.shape[0]
    return pl.pallas_call(
        _pw1_kernel,
        out_shape=jax.ShapeDtypeStruct((t, D), F32),
        grid=(t // tm,),
        in_specs=[pl.BlockSpec((tm, D), lambda i: (i, 0)),
                  _const_spec((1, D)), _const_spec((D, 2 * D)), _const_spec((1, 2 * D))],
        out_specs=pl.BlockSpec((tm, D), lambda i: (i, 0)),
        scratch_shapes=[pltpu.VMEM((D, 2 * D), BF16)],
        compiler_params=_params(("arbitrary",), V7X_VMEM_LIMIT),
        name="conv_pw1",
    )(x, g, w, b)


def _conv2_kernel(u_ref, x_ref, hist_ref, wdw_ref, bdw_ref, lng_ref, lnb_ref, w2_ref, b2_ref,
                  g_ref, o_ref, ext_ref, sh_ref, conv_ref, w2bf_ref, wtap_ref, *, tl, rt):
    bi = pl.program_id(0)
    li = pl.program_id(1)

    @pl.when((bi == 0) & (li == 0))
    def _():
        w2bf_ref[...] = w2_ref[...].astype(BF16)
        for k in range(CONV_W):
            wtap_ref[k] = jnp.broadcast_to(wdw_ref[k:k + 1, :], (SUBLANES, D))

    @pl.when(li == 0)
    def _():
        ext_ref[0:HIST, :] = hist_ref[0]

    @pl.when(li > 0)
    def _():
        ext_ref[0:HIST, :] = ext_ref[tl:tl + HIST, :]

    ext_ref[HIST:HIST + tl, :] = u_ref[0]
    for s in range(1, SUBLANES):
        sh_ref[s - 1] = ext_ref[pl.ds(s, tl + HIST - SUBLANES), :]

    def rows(i, carry):
        r0 = pl.multiple_of(i * rt, rt)
        ntile = rt // SUBLANES
        accs = [jnp.zeros((SUBLANES, D), F32) + bdw_ref[...] for _ in range(ntile)]
        for k in range(CONV_W):
            off = HIST - CONV_W + 1 + k
            s = off % SUBLANES
            base = off - s
            w = wtap_ref[k]
            for ti in range(ntile):
                rows_ti = pl.ds(r0 + base + ti * SUBLANES, SUBLANES)
                src = ext_ref[rows_ti, :] if s == 0 else sh_ref[s - 1, rows_ti, :]
                accs[ti] = accs[ti] + w * src
        for ti in range(ntile):
            conv_ref[pl.ds(r0 + ti * SUBLANES, SUBLANES), :] = accs[ti]
        return carry

    lax.fori_loop(0, tl // rt, rows, 0)
    acc = conv_ref[...]
    mu = jnp.mean(acc, axis=-1, keepdims=True)
    xc = acc - mu
    var = jnp.mean(xc * xc, axis=-1, keepdims=True)
    y = xc * lax.rsqrt(var + EPS) * lng_ref[...] + lnb_ref[...]
    y = y * jax.nn.sigmoid(y)
    t = jnp.dot(y.astype(BF16), w2bf_ref[...], preferred_element_type=F32) + b2_ref[...]
    o_ref[0] = x_ref[0] + _rms(t, g_ref[...])


def _conv_dw_pw2(u, x, hist, wdw, bdw, lng, lnb, w2, b2, g, tl):
    nb, seq, _ = u.shape
    tok = pl.BlockSpec((1, tl, D), lambda b, l: (b, l, 0))
    return pl.pallas_call(
        functools.partial(_conv2_kernel, tl=tl, rt=min(tl, 32)),
        out_shape=jax.ShapeDtypeStruct((nb, seq, D), F32),
        grid=(nb, seq // tl),
        in_specs=[tok, tok, pl.BlockSpec((1, HIST, D), lambda b, l: (b, 0, 0)),
                  _const_spec((CONV_W, D)), _const_spec((1, D)), _const_spec((1, D)), _const_spec((1, D)),
                  _const_spec((D, D)), _const_spec((1, D)), _const_spec((1, D))],
        out_specs=tok,
        scratch_shapes=[pltpu.VMEM((tl + HIST, D), F32),
                        pltpu.VMEM((SUBLANES - 1, tl + HIST - SUBLANES, D), F32),
                        pltpu.VMEM((tl, D), F32), pltpu.VMEM((D, D), BF16),
                        pltpu.VMEM((CONV_W, SUBLANES, D), F32)],
        compiler_params=_params(("arbitrary", "arbitrary"), V7X_VMEM_LIMIT),
        name="conv_dw_pw2",
    )(u, x, hist, wdw, bdw, lng, lnb, w2, b2, g)


def _memkv_kernel(m_ref, g_ref, wk_ref, wv_ref, k_ref, v_ref, wkbf_ref, wvbf_ref):
    @pl.when(pl.program_id(1) == 0)
    def _():
        wkbf_ref[...] = wk_ref[0].astype(BF16)
        wvbf_ref[...] = wv_ref[0].astype(BF16)

    m = _rms(m_ref[0], g_ref[0]).astype(BF16)
    k = jnp.dot(m, wkbf_ref[...], preferred_element_type=F32)
    v = jnp.dot(m, wvbf_ref[...], preferred_element_type=F32)
    for hd in range(N_HEADS):
        sl = slice(hd * HEAD_DIM, (hd + 1) * HEAD_DIM)
        k_ref[0, :, hd, :] = k[:, sl]
        v_ref[0, :, hd, :] = v[:, sl]


def _mem_kv(mem, g, wk, wv):
    nb = mem.shape[0]
    nl = wk.shape[0]
    kv = jax.ShapeDtypeStruct((nl * nb, N_MEM, N_HEADS, HEAD_DIM), F32)
    out = pl.BlockSpec((1, N_MEM, N_HEADS, HEAD_DIM), lambda l, b: (l * nb + b, 0, 0, 0))
    w = pl.BlockSpec((1, D, D), lambda l, b: (l, 0, 0))
    return pl.pallas_call(
        _memkv_kernel,
        out_shape=(kv, kv),
        grid=(nl, nb),
        in_specs=[pl.BlockSpec((1, N_MEM, D), lambda l, b: (b, 0, 0)),
                  pl.BlockSpec((1, 1, D), lambda l, b: (l, 0, 0)), w, w],
        out_specs=(out, out),
        scratch_shapes=[pltpu.VMEM((D, D), BF16), pltpu.VMEM((D, D), BF16)],
        compiler_params=_params(("arbitrary", "arbitrary"), V7X_VMEM_LIMIT),
        name="mem_kv",
    )(mem, g, wk, wv)


def _attn_kernel(x_ref, k_ref, v_ref, wq_ref, wo_ref, g2_ref, g3_ref, o_ref, wqbf_ref, wobf_ref, kb_ref, vb_ref):
    @pl.when((pl.program_id(0) == 0) & (pl.program_id(1) == 0))
    def _():
        wqbf_ref[...] = wq_ref[...].astype(BF16)
        wobf_ref[...] = wo_ref[...].astype(BF16)

    @pl.when(pl.program_id(1) == 0)
    def _():
        for hd in range(N_HEADS):
            sl = slice(hd * HEAD_DIM, (hd + 1) * HEAD_DIM)
            kb_ref[:, sl] = k_ref[0, :, hd, :].astype(BF16)
            vb_ref[:, sl] = v_ref[0, :, hd, :].astype(BF16)

    x = x_ref[...]
    h = _rms(x, g2_ref[...]).astype(BF16)
    q = (jnp.dot(h, wqbf_ref[...], preferred_element_type=F32) * (HEAD_DIM ** -0.5)).astype(BF16)
    heads = []
    for hd in range(N_HEADS):
        sl = slice(hd * HEAD_DIM, (hd + 1) * HEAD_DIM)
        s = lax.dot_general(q[:, sl], kb_ref[:, sl], (((1,), (1,)), ((), ())), preferred_element_type=F32)
        p = jnp.exp(s - jnp.max(s, axis=-1, keepdims=True))
        p = p / jnp.sum(p, axis=-1, keepdims=True)
        heads.append(jnp.dot(p.astype(BF16), vb_ref[:, sl], preferred_element_type=F32))
    o = jnp.concatenate(heads, axis=1).astype(BF16)
    t = jnp.dot(o, wobf_ref[...], preferred_element_type=F32)
    o_ref[...] = x + _rms(t, g3_ref[...])


def _attn(x, k, v, kv_base, wq, wo, g2, g3, nb, seq, tm):
    tok = pl.BlockSpec((tm, D), lambda b, l: (b * (seq // tm) + l, 0))
    kv = pl.BlockSpec((1, N_MEM, N_HEADS, HEAD_DIM), lambda b, l: (kv_base + b, 0, 0, 0))
    return pl.pallas_call(
        _attn_kernel,
        out_shape=jax.ShapeDtypeStruct((nb * seq, D), F32),
        grid=(nb, seq // tm),
        in_specs=[tok, kv, kv, _const_spec((D, D)), _const_spec((D, D)), _const_spec((1, D)), _const_spec((1, D))],
        out_specs=tok,
        scratch_shapes=[pltpu.VMEM((D, D), BF16), pltpu.VMEM((D, D), BF16),
                        pltpu.VMEM((N_MEM, D), BF16), pltpu.VMEM((N_MEM, D), BF16)],
        compiler_params=_params(("arbitrary", "arbitrary"), V7X_VMEM_LIMIT),
        name="mem_attn",
    )(x, k, v, wq, wo, g2, g3)


def _cast_kernel(w_ref, o_ref):
    o_ref[...] = w_ref[...].astype(BF16)


def _to_bf16(w, nsplit):
    r, c = w.shape
    blk = pl.BlockSpec((r // nsplit, c), lambda i: (i, 0))
    return pl.pallas_call(
        _cast_kernel,
        out_shape=jax.ShapeDtypeStruct((r, c), BF16),
        grid=(nsplit,),
        in_specs=[blk],
        out_specs=blk,
        compiler_params=_params(("arbitrary",)),
        name="to_bf16",
    )(w)


def _ffn_kernel(x_ref, g4_ref, g5_ref, wg_ref, wu_ref, wd_ref, o_ref, h_ref, acc_ref):
    f = pl.program_id(1)

    @pl.when(f == 0)
    def _():
        h_ref[...] = _rms(x_ref[...], g4_ref[...]).astype(BF16)
        acc_ref[...] = jnp.zeros_like(acc_ref)

    h = h_ref[...]
    a = jnp.dot(h, wg_ref[...], preferred_element_type=F32)
    b = jnp.dot(h, wu_ref[...], preferred_element_type=F32)
    hh = (a * jax.nn.sigmoid(a) * b).astype(BF16)
    acc_ref[...] += jnp.dot(hh, wd_ref[...], preferred_element_type=F32)

    @pl.when(f == pl.num_programs(1) - 1)
    def _():
        o_ref[...] = x_ref[...] + _rms(acc_ref[...], g5_ref[...])


def _ffn(x, g4, g5, wg, wu, wd, tm, tf, out_shape, out_index):
    t = x.shape[0]
    ff = wg.shape[1]
    return pl.pallas_call(
        _ffn_kernel,
        out_shape=jax.ShapeDtypeStruct(out_shape, F32),
        grid=(t // tm, ff // tf),
        in_specs=[pl.BlockSpec((tm, D), lambda i, f: (i, 0)),
                  _const_spec((1, D)), _const_spec((1, D)),
                  pl.BlockSpec((D, tf), lambda i, f: (0, f)),
                  pl.BlockSpec((D, tf), lambda i, f: (0, f)),
                  pl.BlockSpec((tf, D), lambda i, f: (f, 0))],
        out_specs=pl.BlockSpec((tm, D), out_index),
        scratch_shapes=[pltpu.VMEM((tm, D), BF16), pltpu.VMEM((tm, D), F32)],
        compiler_params=_params(("arbitrary", "arbitrary"), V7X_VMEM_LIMIT),
        name="dense_ffn",
    )(x, g4, g5, wg, wu, wd)


def _ssm_prep_kernel(lr_ref, li_ref, ldt_ref, br_ref, bi_ref, abr_ref, abi_ref, bbr_ref, bbi_ref,
                     a2r_ref, a2i_ref, abbr_ref, abbi_ref):
    dt = jnp.exp(ldt_ref[...])
    lr = lr_ref[...]
    li = li_ref[...]
    mag = jnp.exp(lr * dt)
    ab_r = mag * jnp.cos(li * dt)
    ab_i = mag * jnp.sin(li * dt)
    den = lr * lr + li * li
    nr = ab_r - 1.0
    k_r = (nr * lr + ab_i * li) / den
    k_i = (ab_i * lr - nr * li) / den
    br = br_ref[...]
    bi = bi_ref[...]
    bb_r = k_r * br - k_i * bi
    bb_i = k_r * bi + k_i * br
    abr_ref[...] = ab_r
    abi_ref[...] = ab_i
    bbr_ref[...] = bb_r
    bbi_ref[...] = bb_i
    a2r_ref[...] = ab_r * ab_r - ab_i * ab_i
    a2i_ref[...] = 2.0 * (ab_r * ab_i)
    abbr_ref[...] = ab_r * bb_r - ab_i * bb_i
    abbi_ref[...] = ab_r * bb_i + ab_i * bb_r


def _ssm_prep(a_re, a_im, log_dt, b_re, b_im):
    n = SSM_P * SSM_C
    rep = lambda a: jnp.repeat(a, SSM_C, axis=1)
    shp = jax.ShapeDtypeStruct((SSM_G, n), F32)
    abr, abi, bbr, bbi, a2r, a2i, abbr, abbi = pl.pallas_call(
        _ssm_prep_kernel,
        out_shape=(shp,) * 8,
        name="ssm_prep",
    )(rep(a_re), rep(a_im), log_dt.reshape(SSM_G, 1), b_re.reshape(SSM_G, n), b_im.reshape(SSM_G, n))
    pick = lambda a: a.reshape(SSM_G, SSM_P, SSM_C)[:, :, 0].reshape(1, SSM_N)
    gpc = lambda a: a.reshape(SSM_G, SSM_P, SSM_C)
    return (pick(abr), pick(abi), pick(a2r), pick(a2i)), (gpc(bbr), gpc(bbi), gpc(abbr), gpc(abbi))


def _block_diag(m, gpb=SSM_CB // SSM_C):
    g, a, b = m.shape
    m = m.reshape(g // gpb, gpb, a, 1, b)
    eye = jnp.eye(gpb, dtype=m.dtype).reshape(1, gpb, 1, gpb, 1)
    return (m * eye).reshape(g // gpb, gpb * a, gpb * b)


def _ssm_kernel(x_ref, g0_ref, g1_ref, s0r_ref, s0i_ref, ar_ref, ai_ref, bmr_ref, bmi_ref, cmr_ref, cmi_ref,
                d_ref, wglu_ref, bglu_ref, o_ref, sr_ref, si_ref,
                bur_ref, bui_ref, *, nb, rb, lc):
    @pl.when(pl.program_id(0) == 0)
    def _():
        sr_ref[...] = s0r_ref[...]
        si_ref[...] = s0i_ref[...]

    tl = rb // nb
    x = x_ref[...].reshape(rb, D)
    h = _rms(x, g0_ref[...])
    r = lax.broadcasted_iota(I32, (rb, rb), 0)
    c = lax.broadcasted_iota(I32, (rb, rb), 1)
    to_time_major = (c == (r & (nb - 1)) * tl + (r >> (nb.bit_length() - 1))).astype(BF16)
    to_batch_major = (c == (r & (tl - 1)) * nb + (r >> (tl.bit_length() - 1))).astype(BF16)
    hb = jnp.dot(to_time_major, h.astype(BF16), preferred_element_type=F32).astype(BF16)
    rows = max(nb, SUBLANES)
    pair = nb < SUBLANES
    if pair:
        odd = ((r >> (nb.bit_length() - 1)) & 1) == 1
        prev_time_major = ((c == (r & (nb - 1)) * tl + (r >> (nb.bit_length() - 1)) - 1) & odd).astype(BF16)
        hprev = jnp.dot(prev_time_major, h.astype(BF16), preferred_element_type=F32).astype(BF16)
        low = lax.broadcasted_iota(I32, (SUBLANES, lc), 0) < nb

    def scan_chunk(cs):
        a_r = ar_ref[:, cs]
        a_i = ai_ref[:, cs]
        s_r = sr_ref[:, cs]
        s_i = si_ref[:, cs]
        for j in range(rb // rows):
            rs = slice(j * rows, (j + 1) * rows)
            n_r = a_r * s_r - a_i * s_i + bur_ref[rs, cs]
            n_i = a_r * s_i + a_i * s_r + bui_ref[rs, cs]
            bur_ref[rs, cs] = n_r
            bui_ref[rs, cs] = n_i
            if pair:
                s_r = jnp.where(low, pltpu.roll(n_r, nb, 0), n_r)
                s_i = jnp.where(low, pltpu.roll(n_i, nb, 0), n_i)
            else:
                s_r, s_i = n_r, n_i
        sr_ref[:, cs] = s_r
        si_ref[:, cs] = s_i

    ist = bmr_ref.shape[2]
    ich = bmr_ref.shape[1] // 2 if pair else bmr_ref.shape[1]
    ys = []
    for cb in range(SSM_NCB):
        ss = slice(cb * SSM_SB, (cb + 1) * SSM_SB)
        for ib in range(cb * (SSM_SB // ist), (cb + 1) * (SSM_SB // ist)):
            hs = hb[:, ib * ich:(ib + 1) * ich]
            if pair:
                hs = jnp.concatenate([hs, hprev[:, ib * ich:(ib + 1) * ich]], axis=1)
            bur_ref[:, ib * ist:(ib + 1) * ist] = jnp.dot(hs, bmr_ref[ib], preferred_element_type=F32)
            bui_ref[:, ib * ist:(ib + 1) * ist] = jnp.dot(hs, bmi_ref[ib], preferred_element_type=F32)
            for c in range(ist // lc):
                scan_chunk(slice(ib * ist + c * lc, ib * ist + (c + 1) * lc))
        ys.append(jnp.dot(bur_ref[:, ss].astype(BF16), cmr_ref[cb], preferred_element_type=F32)
                  + jnp.dot(bui_ref[:, ss].astype(BF16), cmi_ref[cb], preferred_element_type=F32))
    y_tm = jnp.concatenate(ys, axis=1)
    y1 = y_tm.astype(BF16)
    rem = y_tm - y1.astype(F32)
    y2 = rem.astype(BF16)
    y3 = (rem - y2.astype(F32)).astype(BF16)
    y = (jnp.dot(to_batch_major, y1, preferred_element_type=F32)
         + jnp.dot(to_batch_major, y2, preferred_element_type=F32)
         + jnp.dot(to_batch_major, y3, preferred_element_type=F32))
    y = y + d_ref[...] * h
    y = jax.nn.gelu(y).astype(BF16)
    z = jnp.dot(y, wglu_ref[...], preferred_element_type=F32) + bglu_ref[...]
    t = z[:, :D] * jax.nn.sigmoid(z[:, D:])
    o_ref[...] = (x + _rms(t, g1_ref[...])).reshape(nb, tl, D)


def _ssm(x, g0, g1, s0r, s0i, ab_r, ab_i, bm_r, bm_i, cm_r, cm_i, d, wglu, bglu, tl):
    nb, seq, _ = x.shape
    rb = nb * tl
    rows = max(nb, SUBLANES)
    st = jax.ShapeDtypeStruct((rows, SSM_N), F32)
    row = pl.BlockSpec((nb, tl, D), lambda i: (0, i, 0))
    return pl.pallas_call(
        functools.partial(_ssm_kernel, nb=nb, rb=rb, lc=512),
        out_shape=(jax.ShapeDtypeStruct((nb, seq, D), F32), st, st),
        grid=(seq // tl,),
        in_specs=[row, _const_spec((1, D)), _const_spec((1, D)),
                  _const_spec((rows, SSM_N)), _const_spec((rows, SSM_N)),
                  _const_spec((rows, SSM_N)), _const_spec((rows, SSM_N)),
                  _const_spec(bm_r.shape), _const_spec(bm_i.shape),
                  _const_spec((SSM_NCB, SSM_SB, SSM_CB)), _const_spec((SSM_NCB, SSM_SB, SSM_CB)),
                  _const_spec((1, D)), _const_spec((D, 2 * D)), _const_spec((1, 2 * D))],
        out_specs=(row, _const_spec((rows, SSM_N)), _const_spec((rows, SSM_N))),
        scratch_shapes=[pltpu.VMEM((rb, SSM_N), F32), pltpu.VMEM((rb, SSM_N), F32)],
        compiler_params=_params(("arbitrary",), V7X_VMEM_LIMIT),
        name="ssm",
    )(x, g0, g1, s0r, s0i, ab_r, ab_i, bm_r, bm_i, cm_r, cm_i, d, wglu, bglu)


def _route_kernel(xp_ref, xs_ref, g_ref, wrt_ref, h_ref, rk_ref, rkt_ref, cwt_ref, cnt_ref, carry_ref,
                  *, n_prompt_chunks):
    i = pl.program_id(0)

    @pl.when(i == 0)
    def _():
        carry_ref[...] = jnp.zeros_like(carry_ref)

    x = jnp.where(i < n_prompt_chunks, xp_ref[...], xs_ref[...])
    h = _rms(x, g_ref[...])
    h_ref[...] = h.astype(BF16)
    lg = lax.dot_general(wrt_ref[...], h, (((1,), (1,)), ((), ())), preferred_element_type=F32,
                         precision=lax.Precision.HIGHEST)
    ex = lax.broadcasted_iota(I32, lg.shape, 0)
    m1 = jnp.max(lg, axis=0, keepdims=True)
    i1 = jnp.min(jnp.where(lg == m1, ex, N_EXP), axis=0, keepdims=True)
    first = ex == i1
    lg2 = jnp.where(first, -jnp.inf, lg)
    m2 = jnp.max(lg2, axis=0, keepdims=True)
    i2 = jnp.min(jnp.where(lg2 == m2, ex, N_EXP), axis=0, keepdims=True)
    second = ex == i2
    e = jnp.exp(m2 - m1)
    den = 1.0 + e
    cwt_ref[...] = jnp.where(first, 1.0 / den, 0.0) + jnp.where(second, e / den, 0.0)
    assigned = first | second
    r = lax.broadcasted_iota(I32, (CHUNK, CHUNK), 0)
    c = lax.broadcasted_iota(I32, (CHUNK, CHUNK), 1)
    earlier = (r < c).astype(BF16)
    rank = jnp.dot(assigned.astype(BF16), earlier, preferred_element_type=F32) + carry_ref[...]
    rank = jnp.where(assigned, rank, -1.0)
    rkt_ref[...] = rank.astype(I32)
    rk_ref[...] = rank.T.astype(I32)
    carry_ref[...] += jnp.sum(assigned.astype(F32), axis=1, keepdims=True)
    cnt_ref[0] = carry_ref[...]


def _route(xp, xs, g, wrt):
    npc = xp.shape[0] // CHUNK
    nch = npc + xs.shape[0] // CHUNK
    t = nch * CHUNK
    exp_major = pl.BlockSpec((N_EXP, CHUNK), lambda i: (0, i))
    return pl.pallas_call(
        functools.partial(_route_kernel, n_prompt_chunks=npc),
        out_shape=(jax.ShapeDtypeStruct((t, D), BF16), jax.ShapeDtypeStruct((t, N_EXP), I32),
                   jax.ShapeDtypeStruct((N_EXP, t), I32), jax.ShapeDtypeStruct((N_EXP, t), F32),
                   jax.ShapeDtypeStruct((nch, N_EXP, 1), F32)),
        grid=(nch,),
        in_specs=[pl.BlockSpec((CHUNK, D), lambda i: (jnp.minimum(i, npc - 1), 0)),
                  pl.BlockSpec((CHUNK, D), lambda i: (jnp.maximum(i - npc, 0), 0)),
                  _const_spec((1, D)), _const_spec((N_EXP, D))],
        out_specs=(pl.BlockSpec((CHUNK, D), lambda i: (i, 0)),
                   pl.BlockSpec((CHUNK, N_EXP), lambda i: (i, 0)), exp_major, exp_major,
                   pl.BlockSpec((1, N_EXP, 1), lambda i: (i, 0, 0))),
        scratch_shapes=[pltpu.VMEM((N_EXP, 1), F32)],
        compiler_params=_params(("arbitrary",)),
        name="moe_route",
    )(xp, xs, g, wrt)


def _gather_kernel(be_ref, pb_ref, pn_ref, r0_ref, bef_ref, aft_ref, pc_ref, pe_ref, pt_ref,
                   h_hbm, pos_hbm, cw_hbm, o_ref, gs_ref, hbuf, pbuf, cbuf, acc_ref, gacc_ref, sem):
    d = pl.program_id(0)
    e = be_ref[d]
    base = pb_ref[d]
    r0 = r0_ref[d]
    total = pt_ref[0]

    def copies(idx):
        slot = idx % GSLOTS
        c = pc_ref[idx]
        ee = pe_ref[idx]
        return (pltpu.make_async_copy(h_hbm.at[pl.ds(c * CHUNK, CHUNK)], hbuf.at[slot], sem.at[0, slot]),
                pltpu.make_async_copy(pos_hbm.at[ee, c], pbuf.at[slot], sem.at[1, slot]),
                pltpu.make_async_copy(cw_hbm.at[ee, c], cbuf.at[slot], sem.at[2, slot]))

    def start(idx):
        @pl.when(idx < total)
        def _():
            for cp in copies(idx):
                cp.start()

    @pl.when(d == 0)
    def _():
        for k in range(GSLOTS - 1):
            start(k)

    acc_ref[...] = jnp.zeros_like(acc_ref)
    gacc_ref[...] = jnp.zeros_like(gacc_ref)
    row = d * BLK + lax.broadcasted_iota(I32, (GWIN, CHUNK), 0)
    row2 = d * BLK + lax.broadcasted_iota(I32, (2 * GWIN, CHUNK), 0)

    def body(j, carry):
        idx = base + j
        slot = idx % GSLOTS
        for cp in copies(idx):
            cp.wait()
        start(idx + GSLOTS - 1)

        c = pc_ref[idx]
        first = jnp.clip(bef_ref[c * N_EXP + e] - r0, 0, BLK)
        last = jnp.clip(aft_ref[c * N_EXP + e] - r0, 0, BLK)

        def fill(w, rows_at):
            w0 = pl.multiple_of(w * GWIN, GWIN)
            n = rows_at.shape[0]
            hit = pbuf[slot] == rows_at + w0
            acc_ref[pl.ds(w0, n), :] += jnp.dot(hit.astype(BF16), hbuf[slot], preferred_element_type=F32)
            gacc_ref[pl.ds(w0, n), :] += jnp.sum(jnp.where(hit, cbuf[slot], 0.0), axis=1, keepdims=True)

        w_first = jnp.minimum(first // GWIN, BLK // GWIN - 2)
        fill(w_first, row2)

        def window(w, c2):
            fill(w, row)
            return c2

        lax.fori_loop(w_first + 2, (last + GWIN - 1) // GWIN, window, 0)
        return carry

    lax.fori_loop(0, pn_ref[d], body, 0)
    o_ref[...] = acc_ref[...].astype(BF16)
    gs_ref[...] = gacc_ref[...]


def _gather(h, pos_t, cw_t, blk_e, pbase, pcount, r0, before, after, pair_c, pair_e, ptotal, nblk):
    hbm = pl.BlockSpec(memory_space=pl.ANY)
    return pl.pallas_call(
        _gather_kernel,
        out_shape=(jax.ShapeDtypeStruct((nblk * BLK, D), BF16), jax.ShapeDtypeStruct((nblk * BLK, 1), F32)),
        grid_spec=pltpu.PrefetchScalarGridSpec(
            num_scalar_prefetch=9,
            grid=(nblk,),
            in_specs=[hbm, hbm, hbm],
            out_specs=(pl.BlockSpec((BLK, D), lambda d, *_: (d, 0)), pl.BlockSpec((BLK, 1), lambda d, *_: (d, 0))),
            scratch_shapes=[pltpu.VMEM((GSLOTS, CHUNK, D), BF16), pltpu.VMEM((GSLOTS, 1, CHUNK), I32),
                            pltpu.VMEM((GSLOTS, 1, CHUNK), F32), pltpu.VMEM((BLK, D), F32),
                            pltpu.VMEM((BLK, 1), F32), pltpu.SemaphoreType.DMA((3, GSLOTS))]),
        compiler_params=_params(("arbitrary",)),
        name="moe_gather",
    )(blk_e, pbase, pcount, r0, before, after, pair_c, pair_e, ptotal, h, pos_t, cw_t)


def _expert_up_kernel(be_ref, nv_ref, x_ref, wg_ref, wu_ref, o_ref, wgbf_ref, wubf_ref):
    d = pl.program_id(1)
    changed = (d == 0) | (be_ref[d] != be_ref[jnp.maximum(d - 1, 0)])

    @pl.when(changed)
    def _():
        wgbf_ref[...] = wg_ref[0].astype(BF16)
        wubf_ref[...] = wu_ref[0].astype(BF16)

    def up(rows):
        x = x_ref[rows, :]
        a = jnp.dot(x, wgbf_ref[...], preferred_element_type=F32)
        b = jnp.dot(x, wubf_ref[...], preferred_element_type=F32)
        o_ref[rows, :] = (a * jax.nn.sigmoid(a) * b).astype(BF16)

    half = BLK // 2
    nrows = nv_ref[d]

    @pl.when(nrows > half)
    def _():
        up(slice(0, BLK))

    @pl.when((nrows > 0) & (nrows <= half))
    def _():
        up(slice(0, half))
        o_ref[half:, :] = jnp.zeros((BLK - half, o_ref.shape[1]), BF16)

    @pl.when(nrows == 0)
    def _():
        o_ref[...] = jnp.zeros_like(o_ref)


def _expert_up(xs, blk_e, nvalid, wg, wu, tf):
    p = xs.shape[0]
    ff = wg.shape[2]
    return pl.pallas_call(
        _expert_up_kernel,
        out_shape=jax.ShapeDtypeStruct((p, ff), BF16),
        grid_spec=pltpu.PrefetchScalarGridSpec(
            num_scalar_prefetch=2,
            grid=(ff // tf, p // BLK),
            in_specs=[pl.BlockSpec((BLK, D), lambda f, d, be, nv: (d, 0)),
                      pl.BlockSpec((1, D, tf), lambda f, d, be, nv: (be[d], 0, f)),
                      pl.BlockSpec((1, D, tf), lambda f, d, be, nv: (be[d], 0, f))],
            out_specs=pl.BlockSpec((BLK, tf), lambda f, d, be, nv: (d, f)),
            scratch_shapes=[pltpu.VMEM((D, tf), BF16), pltpu.VMEM((D, tf), BF16)]),
        compiler_params=_params(("arbitrary", "arbitrary"), V7X_VMEM_LIMIT),
        name="moe_up",
    )(blk_e, nvalid, xs, wg, wu)


def _expert_down_kernel(be_ref, nv_ref, h_ref, gs_ref, wd_ref, o_ref, wdbf_ref):
    d = pl.program_id(1)
    changed = (d == 0) | (be_ref[d] != be_ref[jnp.maximum(d - 1, 0)])

    @pl.when(changed)
    def _():
        wdbf_ref[...] = wd_ref[0].astype(BF16)

    def down(rows):
        y = jnp.dot(h_ref[rows, :], wdbf_ref[...], preferred_element_type=F32)
        o_ref[rows, :] = (y * gs_ref[rows, :]).astype(BF16)

    half = BLK // 2
    nrows = nv_ref[d]

    @pl.when(nrows > half)
    def _():
        down(slice(0, BLK))

    @pl.when((nrows > 0) & (nrows <= half))
    def _():
        down(slice(0, half))
        o_ref[half:, :] = jnp.zeros((BLK - half, o_ref.shape[1]), BF16)

    @pl.when(nrows == 0)
    def _():
        o_ref[...] = jnp.zeros_like(o_ref)


def _expert_down(hh, gsort, blk_e, nvalid, wd, tn):
    p, ff = hh.shape
    return pl.pallas_call(
        _expert_down_kernel,
        out_shape=jax.ShapeDtypeStruct((p, D), BF16),
        grid_spec=pltpu.PrefetchScalarGridSpec(
            num_scalar_prefetch=2,
            grid=(D // tn, p // BLK),
            in_specs=[pl.BlockSpec((BLK, ff), lambda n, d, be, nv: (d, 0)),
                      pl.BlockSpec((BLK, 1), lambda n, d, be, nv: (d, 0)),
                      pl.BlockSpec((1, ff, tn), lambda n, d, be, nv: (be[d], 0, n))],
            out_specs=pl.BlockSpec((BLK, tn), lambda n, d, be, nv: (d, n)),
            scratch_shapes=[pltpu.VMEM((ff, tn), BF16)]),
        compiler_params=_params(("arbitrary", "arbitrary"), V7X_VMEM_LIMIT),
        name="moe_down",
    )(blk_e, nvalid, hh, gsort, wd)


def _combine_kernel(rs_ref, rc_ref, xp_ref, xs_ref, pos_ref, g_ref, y_hbm, op_ref, os_ref, wbuf, obuf, acc_ref,
                    sem, osem, *, n_prompt_chunks):
    i = pl.program_id(0)
    nch = pl.num_programs(0)

    def window_start(chunk, e):
        return pl.multiple_of(rs_ref[chunk * N_EXP + e] // BF16_ROWS * BF16_ROWS, BF16_ROWS)

    def copy(chunk, e, slot):
        return pltpu.make_async_copy(y_hbm.at[pl.ds(window_start(chunk, e), CWIN)],
                                     wbuf.at[slot, pl.ds(e * CWIN, CWIN)], sem.at[slot, e])

    @pl.when(i == 0)
    def _():
        for e in range(N_EXP):
            copy(0, e, 0).start()

    slot = i % 2

    @pl.when(i + 1 < nch)
    def _():
        for e in range(N_EXP):
            copy(i + 1, e, 1 - slot).start()

    col = lax.broadcasted_iota(I32, (CHUNK, CWIN), 1)
    hits = []
    for e in range(N_EXP):
        copy(i, e, slot).wait()
        hits.append((pos_ref[:, e:e + 1] - window_start(i, e)) == col)
    onehot = jnp.concatenate(hits, axis=1).astype(BF16)
    acc_ref[...] = jnp.dot(onehot, wbuf[slot], preferred_element_type=F32)

    for e in range(N_EXP):
        for k in range(1, (CHUNK + BF16_ROWS - 1) // CWIN + 1):
            tail = window_start(i, e) + k * CWIN

            @pl.when(rs_ref[i * N_EXP + e] + rc_ref[i * N_EXP + e] > tail)
            def _():
                cp = pltpu.make_async_copy(y_hbm.at[pl.ds(tail, CWIN)], obuf, osem)
                cp.start()
                cp.wait()
                hit = ((pos_ref[:, e:e + 1] - tail) == col).astype(BF16)
                acc_ref[...] += jnp.dot(hit, obuf[...], preferred_element_type=F32)

    x = jnp.where(i < n_prompt_chunks, xp_ref[...], xs_ref[...])
    out = x + _rms(acc_ref[...], g_ref[...])

    @pl.when(i < n_prompt_chunks)
    def _():
        op_ref[...] = out

    @pl.when(i >= n_prompt_chunks)
    def _():
        os_ref[...] = out


def _combine(run_start, run_count, xp, xs, pos, g, ys):
    npc = xp.shape[0] // CHUNK
    nsc = xs.shape[0] // CHUNK
    nch = npc + nsc
    tok8 = pl.BlockSpec((CHUNK, N_EXP), lambda i, *_: (i, 0))
    pspec = pl.BlockSpec((CHUNK, D), lambda i, *_: (jnp.minimum(i, npc - 1), 0))
    sspec = pl.BlockSpec((CHUNK, D), lambda i, *_: (jnp.maximum(i - npc, 0), 0))
    return pl.pallas_call(
        functools.partial(_combine_kernel, n_prompt_chunks=npc),
        out_shape=(jax.ShapeDtypeStruct(xp.shape, F32), jax.ShapeDtypeStruct(xs.shape, F32)),
        grid_spec=pltpu.PrefetchScalarGridSpec(
            num_scalar_prefetch=2,
            grid=(nch,),
            in_specs=[pspec, sspec, tok8, pl.BlockSpec((1, D), lambda i, *_: (0, 0)),
                      pl.BlockSpec(memory_space=pl.ANY)],
            out_specs=(pspec, sspec),
            scratch_shapes=[pltpu.VMEM((2, N_EXP * CWIN, D), BF16), pltpu.VMEM((CWIN, D), BF16),
                            pltpu.VMEM((CHUNK, D), F32),
                            pltpu.SemaphoreType.DMA((2, N_EXP)), pltpu.SemaphoreType.DMA]),
        compiler_params=_params(("arbitrary",), V7X_VMEM_LIMIT),
        name="moe_combine",
    )(run_start, run_count, xp, xs, pos, g, ys)


def _moe(xp, xs, g4, g5, w_router, wg, wu, wd):
    npc = xp.shape[0] // CHUNK
    nch = npc + xs.shape[0] // CHUNK
    t = nch * CHUNK
    h, rk, rk_t, cw_t, cnt = _route(xp, xs, g4, w_router.T)

    after = cnt.reshape(nch, N_EXP).astype(I32)
    before = jnp.concatenate([jnp.zeros((1, N_EXP), I32), after[:-1]], axis=0)
    counts = after[-1]
    gsz = (counts + BLK - 1) // BLK * BLK
    gend = jnp.cumsum(gsz)
    gstart = gend - gsz
    nblk = (2 * t + N_EXP * (BLK - 1)) // BLK + 1 + (2 * CWIN + BLK - 1) // BLK
    bstart = jnp.arange(nblk, dtype=I32) * BLK
    blk_e = jnp.minimum(jnp.sum(gend[None, :] <= bstart[:, None], axis=1), N_EXP - 1).astype(I32)
    valid = bstart < gend[-1]
    r0 = bstart - gstart[blk_e]
    aft_e = after[:, blk_e]
    bef_e = before[:, blk_e]
    c_lo = jnp.where(valid, jnp.sum(aft_e <= r0[None, :], axis=0), 0).astype(I32)
    c_hi = jnp.where(valid, jnp.sum(bef_e < (r0 + BLK)[None, :], axis=0) - 1, -1).astype(I32)
    c_lo = jnp.minimum(c_lo, nch - 1)
    pos = jnp.where(rk >= 0, rk + gstart[None, :], -1).astype(I32)
    pos_t = jnp.where(rk_t >= 0, rk_t + gstart[:, None], -1).astype(I32).reshape(N_EXP, nch, 1, CHUNK)
    cw_t = cw_t.reshape(N_EXP, nch, 1, CHUNK)
    run_start = (gstart[None, :] + before).astype(I32).reshape(-1)
    run_count = (after - before).reshape(-1)
    nvalid = jnp.where(valid, jnp.clip(counts[blk_e] - r0, 0, BLK), 0).astype(I32)
    pcount = jnp.maximum(c_hi - c_lo + 1, 0)
    pend = jnp.cumsum(pcount)
    pbase = pend - pcount
    k = jnp.arange(nblk + 2 * nch * N_EXP, dtype=I32)
    pair_blk = jnp.minimum(jnp.sum(pend[None, :] <= k[:, None], axis=1), nblk - 1)
    pair_c = jnp.clip(c_lo[pair_blk] + k - pbase[pair_blk], 0, nch - 1).astype(I32)
    pair_e = blk_e[pair_blk]

    xsort, gsort = _gather(h, pos_t, cw_t, blk_e, pbase.astype(I32), pcount.astype(I32), r0.astype(I32),
                           before.reshape(-1), after.reshape(-1), pair_c, pair_e, pend[-1:].astype(I32), nblk)
    hh = _expert_up(xsort, blk_e, nvalid, wg, wu, tf=1792)
    ysort = _expert_down(hh, gsort, blk_e, nvalid, wd, tn=D)
    return _combine(run_start, run_count, xp, xs, pos, g5, ysort)


def kernel(x_prompt, x_sample, cache_conv, cache_mem_k, cache_mem_v, state_ssm_re, state_ssm_im, mem_prompt, norm_g, mem_norm_g, w_xq, w_xk, w_xv, w_xo, conv_w_pw1, conv_b_pw1, conv_w_dw, conv_b_dw, conv_ln_g, conv_ln_b, conv_w_pw2, conv_b_pw2, ssm_a_re, ssm_a_im, ssm_log_dt, ssm_b_re, ssm_b_im, ssm_c_re, ssm_c_im, ssm_d, ssm_w_glu, ssm_b_glu, ffn_w_gate, ffn_w_up, ffn_w_down, moe_w_router, moe_w_gate, moe_w_up, moe_w_down):
    nbp, seqp, _ = x_prompt.shape
    nbs, seqs, _ = x_sample.shape
    tp = nbp * seqp
    ts = nbs * seqs
    row = lambda a: a.reshape(1, -1)
    g = lambda i, k: norm_g[i, k].reshape(1, D)

    nl = w_xk.shape[0]
    pk, pv = _mem_kv(mem_prompt, mem_norm_g.reshape(nl, 1, D), w_xk, w_xv)
    p_mem_k = pk.reshape(nl, nbp, N_MEM, N_HEADS, HEAD_DIM)
    p_mem_v = pv.reshape(nl, nbp, N_MEM, N_HEADS, HEAD_DIM)
    ck = cache_mem_k.reshape(nl * nbs, N_MEM, N_HEADS, HEAD_DIM)
    cv = cache_mem_v.reshape(nl * nbs, N_MEM, N_HEADS, HEAD_DIM)

    conv_args = (conv_w_dw[0], row(conv_b_dw[0]), row(conv_ln_g[0]), row(conv_ln_b[0]),
                 conv_w_pw2[0], row(conv_b_pw2[0]), g(0, 1))
    xp = x_prompt.reshape(tp, D)
    xs = x_sample.reshape(ts, D)
    up = _conv_pw1(xp, g(0, 0), conv_w_pw1[0], row(conv_b_pw1[0]), tm=1024).reshape(nbp, seqp, D)
    us = _conv_pw1(xs, g(0, 0), conv_w_pw1[0], row(conv_b_pw1[0]), tm=ts).reshape(nbs, seqs, D)
    hist_p = jnp.zeros((nbp, HIST, D), F32)
    hist_s = jnp.pad(cache_conv[0], ((0, 0), (HIST - CONV_W + 1, 0), (0, 0)))
    xp = _conv_dw_pw2(up, x_prompt, hist_p, *conv_args, tl=512).reshape(tp, D)
    xs = _conv_dw_pw2(us, x_sample, hist_s, *conv_args, tl=seqs).reshape(ts, D)
    p_conv = up[:, seqp - (CONV_W - 1):][None]
    s_conv = jnp.concatenate([cache_conv[0], us], axis=1)[:, -(CONV_W - 1):][None]

    tma = 1024
    xp = _attn(xp, pk, pv, 0, w_xq[0], w_xo[0], g(0, 2), g(0, 3), nbp, seqp, tma)
    xs = _attn(xs, ck, cv, 0, w_xq[0], w_xo[0], g(0, 2), g(0, 3), nbs, seqs, seqs)

    ffn_w = (_to_bf16(ffn_w_gate[0], 4), _to_bf16(ffn_w_up[0], 4), _to_bf16(ffn_w_down[0], 4))
    tff = ffn_w_gate.shape[2] // 2
    xp = _ffn(xp, g(0, 4), g(0, 5), *ffn_w, tm=512, tf=tff, out_shape=(tp, D), out_index=lambda i, f: (i, 0))
    xs = _ffn(xs, g(0, 4), g(0, 5), *ffn_w, tm=ts, tf=tff, out_shape=(ts, D), out_index=lambda i, f: (i, 0))

    (ab_r, ab_i, a2_r, a2_i), (bb_r, bb_i, abb_r, abb_i) = _ssm_prep(
        ssm_a_re[0], ssm_a_im[0], ssm_log_dt[0], ssm_b_re[0], ssm_b_im[0])
    bdiag = lambda m: _block_diag(m.transpose(0, 2, 1)).astype(BF16)
    bm_r, bm_i = bdiag(bb_r), bdiag(bb_i)
    cm_r, cm_i = bdiag(ssm_c_re[0]), bdiag(-ssm_c_im[0])
    tail_args = (cm_r, cm_i, row(ssm_d[0]), _to_bf16(ssm_w_glu[0], 4), row(ssm_b_glu[0]))
    half = SUBLANES // 2
    hdiag = lambda m: _block_diag(m.transpose(0, 2, 1), gpb=SSM_CB // SSM_C // 2).astype(BF16)
    pair_args = (jnp.concatenate([jnp.tile(ab_r, (half, 1)), jnp.tile(a2_r, (half, 1))]),
                 jnp.concatenate([jnp.tile(ab_i, (half, 1)), jnp.tile(a2_i, (half, 1))]),
                 jnp.concatenate([hdiag(bb_r), hdiag(abb_r)], axis=1),
                 jnp.concatenate([hdiag(bb_i), hdiag(abb_i)], axis=1))
    zero_state = jnp.zeros((SUBLANES, SSM_N), F32)
    xp, p_sr, p_si = _ssm(xp.reshape(nbp, seqp, D), g(1, 0), g(1, 1), zero_state, zero_state,
                          *pair_args, *tail_args, tl=64)
    xs, s_sr, s_si = _ssm(xs.reshape(nbs, seqs, D), g(1, 0), g(1, 1), state_ssm_re[0].reshape(nbs, SSM_N),
                          state_ssm_im[0].reshape(nbs, SSM_N), jnp.tile(ab_r, (nbs, 1)), jnp.tile(ab_i, (nbs, 1)),
                          bm_r, bm_i, *tail_args, tl=seqs)
    st = lambda a, n: a[:n].reshape(1, n, SSM_G, SSM_P)
    p_ssm_re, p_ssm_im = st(p_sr, nbp), st(p_si, nbp)
    s_ssm_re, s_ssm_im = st(s_sr, nbs), st(s_si, nbs)

    xp = _attn(xp.reshape(tp, D), pk, pv, nbp, w_xq[1], w_xo[1], g(1, 2), g(1, 3), nbp, seqp, tma)
    xs = _attn(xs.reshape(ts, D), ck, cv, nbs, w_xq[1], w_xo[1], g(1, 2), g(1, 3), nbs, seqs, seqs)

    yp, ysm = _moe(xp, xs, g(1, 4), g(1, 5), moe_w_router[0], moe_w_gate[0], moe_w_up[0], moe_w_down[0])
    return (yp.reshape(nbp, seqp, D), ysm.reshape(nbs, seqs, D), p_conv, p_mem_k, p_mem_v,
            p_ssm_re, p_ssm_im, s_conv, s_ssm_re, s_ssm_im)
```

```python
import functools
import math

import jax
import jax.numpy as jnp
from jax import lax
from jax.experimental import pallas as pl
from jax.experimental.pallas import tpu as pltpu

F32 = jnp.float32
BF16 = jnp.bfloat16
I32 = jnp.int32

D = 1024
CONV_W = 31
HIST = 32
N_MEM = 256
N_HEADS = 4
HEAD_DIM = D // N_HEADS
SSM_G = 64
SSM_C = 16
SSM_P = 64
SSM_N = SSM_G * SSM_P
SSM_CB = 256
SSM_NCB = D // SSM_CB
SSM_SB = SSM_CB // SSM_C * SSM_P
N_EXP = 8
EPS = 1e-6

V7X_VMEM_LIMIT = 56 * 1024 * 1024
SUBLANES = 8

CHUNK = 256
BLK = 512
GWIN = 128
GSLOTS = 8
CWIN = 256
BF16_ROWS = 16


def _params(sem, vmem=None):
    return pltpu.CompilerParams(dimension_semantics=sem, vmem_limit_bytes=vmem)


def _rms(x, g):
    return x * lax.rsqrt(jnp.mean(x * x, axis=-1, keepdims=True) + EPS) * g


def _const_spec(shape):
    nd = len(shape)
    return pl.BlockSpec(shape, lambda *_: (0,) * nd)


def _pw1_kernel(x_ref, g_ref, w_ref, b_ref, u_ref, wbf_ref):
    @pl.when(pl.program_id(0) == 0)
    def _():
        wbf_ref[...] = w_ref[...].astype(BF16)

    h = _rms(x_ref[...], g_ref[...]).astype(BF16)
    z = jnp.dot(h, wbf_ref[...], preferred_element_type=F32) + b_ref[...]
    u_ref[...] = z[:, :D] * jax.nn.sigmoid(z[:, D:])


def _conv_pw1(x, g, w, b, tm):
    t = x.shape[0]
    return pl.pallas_call(
        _pw1_kernel,
        out_shape=jax.ShapeDtypeStruct((t, D), F32),
        grid=(t // tm,),
        in_specs=[pl.BlockSpec((tm, D), lambda i: (i, 0)),
                  _const_spec((1, D)), _const_spec((D, 2 * D)), _const_spec((1, 2 * D))],
        out_specs=pl.BlockSpec((tm, D), lambda i: (i, 0)),
        scratch_shapes=[pltpu.VMEM((D, 2 * D), BF16)],
        compiler_params=_params(("arbitrary",), V7X_VMEM_LIMIT),
        name="conv_pw1",
    )(x, g, w, b)


def _conv2_kernel(u_ref, x_ref, hist_ref, wdw_ref, bdw_ref, lng_ref, lnb_ref, w2_ref, b2_ref,
                  g_ref, o_ref, ext_ref, sh_ref, conv_ref, w2bf_ref, wtap_ref, *, tl, rt):
    bi = pl.program_id(0)
    li = pl.program_id(1)

    @pl.when((bi == 0) & (li == 0))
    def _():
        w2bf_ref[...] = w2_ref[...].astype(BF16)
        for k in range(CONV_W):
            wtap_ref[k] = jnp.broadcast_to(wdw_ref[k:k + 1, :], (SUBLANES, D))

    @pl.when(li == 0)
    def _():
        ext_ref[0:HIST, :] = hist_ref[0]

    @pl.when(li > 0)
    def _():
        ext_ref[0:HIST, :] = ext_ref[tl:tl + HIST, :]

    ext_ref[HIST:HIST + tl, :] = u_ref[0]
    for s in range(1, SUBLANES):
        sh_ref[s - 1] = ext_ref[pl.ds(s, tl + HIST - SUBLANES), :]

    def rows(i, carry):
        r0 = pl.multiple_of(i * rt, rt)
        ntile = rt // SUBLANES
        accs = [jnp.zeros((SUBLANES, D), F32) + bdw_ref[...] for _ in range(ntile)]
        for k in range(CONV_W):
            off = HIST - CONV_W + 1 + k
            s = off % SUBLANES
            base = off - s
            w = wtap_ref[k]
            for ti in range(ntile):
                rows_ti = pl.ds(r0 + base + ti * SUBLANES, SUBLANES)
                src = ext_ref[rows_ti, :] if s == 0 else sh_ref[s - 1, rows_ti, :]
                accs[ti] = accs[ti] + w * src
        for ti in range(ntile):
            conv_ref[pl.ds(r0 + ti * SUBLANES, SUBLANES), :] = accs[ti]
        return carry

    lax.fori_loop(0, tl // rt, rows, 0)
    acc = conv_ref[...]
    mu = jnp.mean(acc, axis=-1, keepdims=True)
    xc = acc - mu
    var = jnp.mean(xc * xc, axis=-1, keepdims=True)
    y = xc * lax.rsqrt(var + EPS) * lng_ref[...] + lnb_ref[...]
    y = y * jax.nn.sigmoid(y)
    t = jnp.dot(y.astype(BF16), w2bf_ref[...], preferred_element_type=F32) + b2_ref[...]
    o_ref[0] = x_ref[0] + _rms(t, g_ref[...])


def _conv_dw_pw2(u, x, hist, wdw, bdw, lng, lnb, w2, b2, g, tl):
    nb, seq, _ = u.shape
    tok = pl.BlockSpec((1, tl, D), lambda b, l: (b, l, 0))
    return pl.pallas_call(
        functools.partial(_conv2_kernel, tl=tl, rt=min(tl, 32)),
        out_shape=jax.ShapeDtypeStruct((nb, seq, D), F32),
        grid=(nb, seq // tl),
        in_specs=[tok, tok, pl.BlockSpec((1, HIST, D), lambda b, l: (b, 0, 0)),
                  _const_spec((CONV_W, D)), _const_spec((1, D)), _const_spec((1, D)), _const_spec((1, D)),
                  _const_spec((D, D)), _const_spec((1, D)), _const_spec((1, D))],
        out_specs=tok,
        scratch_shapes=[pltpu.VMEM((tl + HIST, D), F32),
                        pltpu.VMEM((SUBLANES - 1, tl + HIST - SUBLANES, D), F32),
                        pltpu.VMEM((tl, D), F32), pltpu.VMEM((D, D), BF16),
                        pltpu.VMEM((CONV_W, SUBLANES, D), F32)],
        compiler_params=_params(("arbitrary", "arbitrary"), V7X_VMEM_LIMIT),
        name="conv_dw_pw2",
    )(u, x, hist, wdw, bdw, lng, lnb, w2, b2, g)


def _memkv_kernel(m_ref, g_ref, wk_ref, wv_ref, k_ref, v_ref, wkbf_ref, wvbf_ref):
    @pl.when(pl.program_id(1) == 0)
    def _():
        wkbf_ref[...] = wk_ref[0].astype(BF16)
        wvbf_ref[...] = wv_ref[0].astype(BF16)

    m = _rms(m_ref[0], g_ref[0]).astype(BF16)
    k = jnp.dot(m, wkbf_ref[...], preferred_element_type=F32)
    v = jnp.dot(m, wvbf_ref[...], preferred_element_type=F32)
    for hd in range(N_HEADS):
        sl = slice(hd * HEAD_DIM, (hd + 1) * HEAD_DIM)
        k_ref[0, :, hd, :] = k[:, sl]
        v_ref[0, :, hd, :] = v[:, sl]


def _mem_kv(mem, g, wk, wv):
    nb = mem.shape[0]
    nl = wk.shape[0]
    kv = jax.ShapeDtypeStruct((nl * nb, N_MEM, N_HEADS, HEAD_DIM), F32)
    out = pl.BlockSpec((1, N_MEM, N_HEADS, HEAD_DIM), lambda l, b: (l * nb + b, 0, 0, 0))
    w = pl.BlockSpec((1, D, D), lambda l, b: (l, 0, 0))
    return pl.pallas_call(
        _memkv_kernel,
        out_shape=(kv, kv),
        grid=(nl, nb),
        in_specs=[pl.BlockSpec((1, N_MEM, D), lambda l, b: (b, 0, 0)),
                  pl.BlockSpec((1, 1, D), lambda l, b: (l, 0, 0)), w, w],
        out_specs=(out, out),
        scratch_shapes=[pltpu.VMEM((D, D), BF16), pltpu.VMEM((D, D), BF16)],
        compiler_params=_params(("arbitrary", "arbitrary"), V7X_VMEM_LIMIT),
        name="mem_kv",
    )(mem, g, wk, wv)


def _attn_kernel(x_ref, k_ref, v_ref, wq_ref, wo_ref, g2_ref, g3_ref, o_ref, wqbf_ref, wobf_ref, kb_ref, vb_ref):
    @pl.when((pl.program_id(0) == 0) & (pl.program_id(1) == 0))
    def _():
        wqbf_ref[...] = wq_ref[...].astype(BF16)
        wobf_ref[...] = wo_ref[...].astype(BF16)

    @pl.when(pl.program_id(1) == 0)
    def _():
        for hd in range(N_HEADS):
            sl = slice(hd * HEAD_DIM, (hd + 1) * HEAD_DIM)
            kb_ref[:, sl] = k_ref[0, :, hd, :].astype(BF16)
            vb_ref[:, sl] = v_ref[0, :, hd, :].astype(BF16)

    x = x_ref[...]
    h = _rms(x, g2_ref[...]).astype(BF16)
    q = (jnp.dot(h, wqbf_ref[...], preferred_element_type=F32) * (HEAD_DIM ** -0.5)).astype(BF16)
    heads = []
    for hd in range(N_HEADS):
        sl = slice(hd * HEAD_DIM, (hd + 1) * HEAD_DIM)
        s = lax.dot_general(q[:, sl], kb_ref[:, sl], (((1,), (1,)), ((), ())), preferred_element_type=F32)
        p = jnp.exp(s - jnp.max(s, axis=-1, keepdims=True))
        p = p / jnp.sum(p, axis=-1, keepdims=True)
        heads.append(jnp.dot(p.astype(BF16), vb_ref[:, sl], preferred_element_type=F32))
    o = jnp.concatenate(heads, axis=1).astype(BF16)
    t = jnp.dot(o, wobf_ref[...], preferred_element_type=F32)
    o_ref[...] = x + _rms(t, g3_ref[...])


def _attn(x, k, v, kv_base, wq, wo, g2, g3, nb, seq, tm):
    tok = pl.BlockSpec((tm, D), lambda b, l: (b * (seq // tm) + l, 0))
    kv = pl.BlockSpec((1, N_MEM, N_HEADS, HEAD_DIM), lambda b, l: (kv_base + b, 0, 0, 0))
    return pl.pallas_call(
        _attn_kernel,
        out_shape=jax.ShapeDtypeStruct((nb * seq, D), F32),
        grid=(nb, seq // tm),
        in_specs=[tok, kv, kv, _const_spec((D, D)), _const_spec((D, D)), _const_spec((1, D)), _const_spec((1, D))],
        out_specs=tok,
        scratch_shapes=[pltpu.VMEM((D, D), BF16), pltpu.VMEM((D, D), BF16),
                        pltpu.VMEM((N_MEM, D), BF16), pltpu.VMEM((N_MEM, D), BF16)],
        compiler_params=_params(("arbitrary", "arbitrary"), V7X_VMEM_LIMIT),
        name="mem_attn",
    )(x, k, v, wq, wo, g2, g3)


def _cast_kernel(w_ref, o_ref):
    o_ref[...] = w_ref[...].astype(BF16)


def _to_bf16(w, nsplit):
    r, c = w.shape
    blk = pl.BlockSpec((r // nsplit, c), lambda i: (i, 0))
    return pl.pallas_call(
        _cast_kernel,
        out_shape=jax.ShapeDtypeStruct((r, c), BF16),
        grid=(nsplit,),
        in_specs=[blk],
        out_specs=blk,
        compiler_params=_params(("arbitrary",)),
        name="to_bf16",
    )(w)


def _ffn_kernel(x_ref, g4_ref, g5_ref, wg_ref, wu_ref, wd_ref, o_ref, h_ref, acc_ref):
    f = pl.program_id(1)

    @pl.when(f == 0)
    def _():
        h_ref[...] = _rms(x_ref[...], g4_ref[...]).astype(BF16)
        acc_ref[...] = jnp.zeros_like(acc_ref)

    h = h_ref[...]
    a = jnp.dot(h, wg_ref[...], preferred_element_type=F32)
    b = jnp.dot(h, wu_ref[...], preferred_element_type=F32)
    hh = (a * jax.nn.sigmoid(a) * b).astype(BF16)
    acc_ref[...] += jnp.dot(hh, wd_ref[...], preferred_element_type=F32)

    @pl.when(f == pl.num_programs(1) - 1)
    def _():
        o_ref[...] = x_ref[...] + _rms(acc_ref[...], g5_ref[...])


def _ffn(x, g4, g5, wg, wu, wd, tm, tf, out_shape, out_index):
    t = x.shape[0]
    ff = wg.shape[1]
    return pl.pallas_call(
        _ffn_kernel,
        out_shape=jax.ShapeDtypeStruct(out_shape, F32),
        grid=(t // tm, ff // tf),
        in_specs=[pl.BlockSpec((tm, D), lambda i, f: (i, 0)),
                  _const_spec((1, D)), _const_spec((1, D)),
                  pl.BlockSpec((D, tf), lambda i, f: (0, f)),
                  pl.BlockSpec((D, tf), lambda i, f: (0, f)),
                  pl.BlockSpec((tf, D), lambda i, f: (f, 0))],
        out_specs=pl.BlockSpec((tm, D), out_index),
        scratch_shapes=[pltpu.VMEM((tm, D), BF16), pltpu.VMEM((tm, D), F32)],
        compiler_params=_params(("arbitrary", "arbitrary"), V7X_VMEM_LIMIT),
        name="dense_ffn",
    )(x, g4, g5, wg, wu, wd)


def _ssm_prep_kernel(lr_ref, li_ref, ldt_ref, br_ref, bi_ref, abr_ref, abi_ref, bbr_ref, bbi_ref,
                     a2r_ref, a2i_ref, abbr_ref, abbi_ref):
    dt = jnp.exp(ldt_ref[...])
    lr = lr_ref[...]
    li = li_ref[...]
    mag = jnp.exp(lr * dt)
    ab_r = mag * jnp.cos(li * dt)
    ab_i = mag * jnp.sin(li * dt)
    den = lr * lr + li * li
    nr = ab_r - 1.0
    k_r = (nr * lr + ab_i * li) / den
    k_i = (ab_i * lr - nr * li) / den
    br = br_ref[...]
    bi = bi_ref[...]
    bb_r = k_r * br - k_i * bi
    bb_i = k_r * bi + k_i * br
    abr_ref[...] = ab_r
    abi_ref[...] = ab_i
    bbr_ref[...] = bb_r
    bbi_ref[...] = bb_i
    a2r_ref[...] = ab_r * ab_r - ab_i * ab_i
    a2i_ref[...] = 2.0 * (ab_r * ab_i)
    abbr_ref[...] = ab_r * bb_r - ab_i * bb_i
    abbi_ref[...] = ab_r * bb_i + ab_i * bb_r


def _ssm_prep(a_re, a_im, log_dt, b_re, b_im):
    n = SSM_P * SSM_C
    rep = lambda a: jnp.repeat(a, SSM_C, axis=1)
    shp = jax.ShapeDtypeStruct((SSM_G, n), F32)
    abr, abi, bbr, bbi, a2r, a2i, abbr, abbi = pl.pallas_call(
        _ssm_prep_kernel,
        out_shape=(shp,) * 8,
        name="ssm_prep",
    )(rep(a_re), rep(a_im), log_dt.reshape(SSM_G, 1), b_re.reshape(SSM_G, n), b_im.reshape(SSM_G, n))
    pick = lambda a: a.reshape(SSM_G, SSM_P, SSM_C)[:, :, 0].reshape(1, SSM_N)
    gpc = lambda a: a.reshape(SSM_G, SSM_P, SSM_C)
    return (pick(abr), pick(abi), pick(a2r), pick(a2i)), (gpc(bbr), gpc(bbi), gpc(abbr), gpc(abbi))


def _block_diag(m, gpb=SSM_CB // SSM_C):
    g, a, b = m.shape
    m = m.reshape(g // gpb, gpb, a, 1, b)
    eye = jnp.eye(gpb, dtype=m.dtype).reshape(1, gpb, 1, gpb, 1)
    return (m * eye).reshape(g // gpb, gpb * a, gpb * b)


def _ssm_kernel(x_ref, g0_ref, g1_ref, s0r_ref, s0i_ref, ar_ref, ai_ref, bmr_ref, bmi_ref, cmr_ref, cmi_ref,
                d_ref, wglu_ref, bglu_ref, o_ref, sr_ref, si_ref,
                bur_ref, bui_ref, *, nb, rb, lc):
    @pl.when(pl.program_id(0) == 0)
    def _():
        sr_ref[...] = s0r_ref[...]
        si_ref[...] = s0i_ref[...]

    tl = rb // nb
    x = x_ref[...].reshape(rb, D)
    h = _rms(x, g0_ref[...])
    r = lax.broadcasted_iota(I32, (rb, rb), 0)
    c = lax.broadcasted_iota(I32, (rb, rb), 1)
    to_time_major = (c == (r & (nb - 1)) * tl + (r >> (nb.bit_length() - 1))).astype(BF16)
    to_batch_major = (c == (r & (tl - 1)) * nb + (r >> (tl.bit_length() - 1))).astype(BF16)
    hb = jnp.dot(to_time_major, h.astype(BF16), preferred_element_type=F32).astype(BF16)
    rows = max(nb, SUBLANES)
    pair = nb < SUBLANES
    if pair:
        odd = ((r >> (nb.bit_length() - 1)) & 1) == 1
        prev_time_major = ((c == (r & (nb - 1)) * tl + (r >> (nb.bit_length() - 1)) - 1) & odd).astype(BF16)
        hprev = jnp.dot(prev_time_major, h.astype(BF16), preferred_element_type=F32).astype(BF16)
        low = lax.broadcasted_iota(I32, (SUBLANES, lc), 0) < nb

    def scan_chunk(cs):
        a_r = ar_ref[:, cs]
        a_i = ai_ref[:, cs]
        s_r = sr_ref[:, cs]
        s_i = si_ref[:, cs]
        for j in range(rb // rows):
            rs = slice(j * rows, (j + 1) * rows)
            n_r = a_r * s_r - a_i * s_i + bur_ref[rs, cs]
            n_i = a_r * s_i + a_i * s_r + bui_ref[rs, cs]
            bur_ref[rs, cs] = n_r
            bui_ref[rs, cs] = n_i
            if pair:
                s_r = jnp.where(low, pltpu.roll(n_r, nb, 0), n_r)
                s_i = jnp.where(low, pltpu.roll(n_i, nb, 0), n_i)
            else:
                s_r, s_i = n_r, n_i
        sr_ref[:, cs] = s_r
        si_ref[:, cs] = s_i

    ist = bmr_ref.shape[2]
    ich = bmr_ref.shape[1] // 2 if pair else bmr_ref.shape[1]
    ys = []
    for cb in range(SSM_NCB):
        ss = slice(cb * SSM_SB, (cb + 1) * SSM_SB)
        for ib in range(cb * (SSM_SB // ist), (cb + 1) * (SSM_SB // ist)):
            hs = hb[:, ib * ich:(ib + 1) * ich]
            if pair:
                hs = jnp.concatenate([hs, hprev[:, ib * ich:(ib + 1) * ich]], axis=1)
            bur_ref[:, ib * ist:(ib + 1) * ist] = jnp.dot(hs, bmr_ref[ib], preferred_element_type=F32)
            bui_ref[:, ib * ist:(ib + 1) * ist] = jnp.dot(hs, bmi_ref[ib], preferred_element_type=F32)
            for c in range(ist // lc):
                scan_chunk(slice(ib * ist + c * lc, ib * ist + (c + 1) * lc))
        ys.append(jnp.dot(bur_ref[:, ss].astype(BF16), cmr_ref[cb], preferred_element_type=F32)
                  + jnp.dot(bui_ref[:, ss].astype(BF16), cmi_ref[cb], preferred_element_type=F32))
    y_tm = jnp.concatenate(ys, axis=1)
    y1 = y_tm.astype(BF16)
    rem = y_tm - y1.astype(F32)
    y2 = rem.astype(BF16)
    y3 = (rem - y2.astype(F32)).astype(BF16)
    y = (jnp.dot(to_batch_major, y1, preferred_element_type=F32)
         + jnp.dot(to_batch_major, y2, preferred_element_type=F32)
         + jnp.dot(to_batch_major, y3, preferred_element_type=F32))
    y = y + d_ref[...] * h
    y = jax.nn.gelu(y).astype(BF16)
    z = jnp.dot(y, wglu_ref[...], preferred_element_type=F32) + bglu_ref[...]
    t = z[:, :D] * jax.nn.sigmoid(z[:, D:])
    o_ref[...] = (x + _rms(t, g1_ref[...])).reshape(nb, tl, D)


def _ssm(x, g0, g1, s0r, s0i, ab_r, ab_i, bm_r, bm_i, cm_r, cm_i, d, wglu, bglu, tl):
    nb, seq, _ = x.shape
    rb = nb * tl
    rows = max(nb, SUBLANES)
    st = jax.ShapeDtypeStruct((rows, SSM_N), F32)
    row = pl.BlockSpec((nb, tl, D), lambda i: (0, i, 0))
    return pl.pallas_call(
        functools.partial(_ssm_kernel, nb=nb, rb=rb, lc=512),
        out_shape=(jax.ShapeDtypeStruct((nb, seq, D), F32), st, st),
        grid=(seq // tl,),
        in_specs=[row, _const_spec((1, D)), _const_spec((1, D)),
                  _const_spec((rows, SSM_N)), _const_spec((rows, SSM_N)),
                  _const_spec((rows, SSM_N)), _const_spec((rows, SSM_N)),
                  _const_spec(bm_r.shape), _const_spec(bm_i.shape),
                  _const_spec((SSM_NCB, SSM_SB, SSM_CB)), _const_spec((SSM_NCB, SSM_SB, SSM_CB)),
                  _const_spec((1, D)), _const_spec((D, 2 * D)), _const_spec((1, 2 * D))],
        out_specs=(row, _const_spec((rows, SSM_N)), _const_spec((rows, SSM_N))),
        scratch_shapes=[pltpu.VMEM((rb, SSM_N), F32), pltpu.VMEM((rb, SSM_N), F32)],
        compiler_params=_params(("arbitrary",), V7X_VMEM_LIMIT),
        name="ssm",
    )(x, g0, g1, s0r, s0i, ab_r, ab_i, bm_r, bm_i, cm_r, cm_i, d, wglu, bglu)


def _route_kernel(xp_ref, xs_ref, g_ref, wrt_ref, h_ref, rk_ref, rkt_ref, cwt_ref, cnt_ref, carry_ref,
                  *, n_prompt_chunks):
    i = pl.program_id(0)

    @pl.when(i == 0)
    def _():
        carry_ref[...] = jnp.zeros_like(carry_ref)

    x = jnp.where(i < n_prompt_chunks, xp_ref[...], xs_ref[...])
    h = _rms(x, g_ref[...])
    h_ref[...] = h.astype(BF16)
    lg = lax.dot_general(wrt_ref[...], h, (((1,), (1,)), ((), ())), preferred_element_type=F32,
                         precision=lax.Precision.HIGHEST)
    ex = lax.broadcasted_iota(I32, lg.shape, 0)
    m1 = jnp.max(lg, axis=0, keepdims=True)
    i1 = jnp.min(jnp.where(lg == m1, ex, N_EXP), axis=0, keepdims=True)
    first = ex == i1
    lg2 = jnp.where(first, -jnp.inf, lg)
    m2 = jnp.max(lg2, axis=0, keepdims=True)
    i2 = jnp.min(jnp.where(lg2 == m2, ex, N_EXP), axis=0, keepdims=True)
    second = ex == i2
    e = jnp.exp(m2 - m1)
    den = 1.0 + e
    cwt_ref[...] = jnp.where(first, 1.0 / den, 0.0) + jnp.where(second, e / den, 0.0)
    assigned = first | second
    r = lax.broadcasted_iota(I32, (CHUNK, CHUNK), 0)
    c = lax.broadcasted_iota(I32, (CHUNK, CHUNK), 1)
    earlier = (r < c).astype(BF16)
    rank = jnp.dot(assigned.astype(BF16), earlier, preferred_element_type=F32) + carry_ref[...]
    rank = jnp.where(assigned, rank, -1.0)
    rkt_ref[...] = rank.astype(I32)
    rk_ref[...] = rank.T.astype(I32)
    carry_ref[...] += jnp.sum(assigned.astype(F32), axis=1, keepdims=True)
    cnt_ref[0] = carry_ref[...]


def _route(xp, xs, g, wrt):
    npc = xp.shape[0] // CHUNK
    nch = npc + xs.shape[0] // CHUNK
    t = nch * CHUNK
    exp_major = pl.BlockSpec((N_EXP, CHUNK), lambda i: (0, i))
    return pl.pallas_call(
        functools.partial(_route_kernel, n_prompt_chunks=npc),
        out_shape=(jax.ShapeDtypeStruct((t, D), BF16), jax.ShapeDtypeStruct((t, N_EXP), I32),
                   jax.ShapeDtypeStruct((N_EXP, t), I32), jax.ShapeDtypeStruct((N_EXP, t), F32),
                   jax.ShapeDtypeStruct((nch, N_EXP, 1), F32)),
        grid=(nch,),
        in_specs=[pl.BlockSpec((CHUNK, D), lambda i: (jnp.minimum(i, npc - 1), 0)),
                  pl.BlockSpec((CHUNK, D), lambda i: (jnp.maximum(i - npc, 0), 0)),
                  _const_spec((1, D)), _const_spec((N_EXP, D))],
        out_specs=(pl.BlockSpec((CHUNK, D), lambda i: (i, 0)),
                   pl.BlockSpec((CHUNK, N_EXP), lambda i: (i, 0)), exp_major, exp_major,
                   pl.BlockSpec((1, N_EXP, 1), lambda i: (i, 0, 0))),
        scratch_shapes=[pltpu.VMEM((N_EXP, 1), F32)],
        compiler_params=_params(("arbitrary",)),
        name="moe_route",
    )(xp, xs, g, wrt)


def _gather_kernel(be_ref, pb_ref, pn_ref, r0_ref, bef_ref, aft_ref, pc_ref, pe_ref, pk_ref, pt_ref,
                   h_hbm, pos_hbm, cw_hbm, o_ref, gs_ref, hbuf, pbuf, cbuf, acc_ref, gacc_ref, sem):
    d = pl.program_id(0)
    e = be_ref[d]
    base = pb_ref[d]
    r0 = r0_ref[d]
    total = pt_ref[0]
    ahead = GSLOTS - 2

    def copies(idx):
        slot = idx % GSLOTS
        c = pc_ref[idx]
        ee = pe_ref[idx]
        return (pltpu.make_async_copy(h_hbm.at[pl.ds(c * CHUNK, CHUNK)], hbuf.at[slot], sem.at[0, slot]),
                pltpu.make_async_copy(pos_hbm.at[ee, c], pbuf.at[slot], sem.at[1, slot]),
                pltpu.make_async_copy(cw_hbm.at[ee, c], cbuf.at[slot], sem.at[2, slot]))

    def start(idx):
        @pl.when(idx < total)
        def _():
            for cp in copies(idx):
                cp.start()

    @pl.when(d == 0)
    def _():
        for k in range(ahead):
            start(k)

    acc_ref[...] = jnp.zeros_like(acc_ref)
    gacc_ref[...] = jnp.zeros_like(gacc_ref)
    row = d * BLK + lax.broadcasted_iota(I32, (GWIN, CHUNK), 0)
    row2 = d * BLK + lax.broadcasted_iota(I32, (2 * GWIN, CHUNK), 0)

    def body(j, carry):
        plan = []
        for half in range(2):
            idx = base + 2 * j + half
            slot = idx % GSLOTS
            for cp in copies(idx):
                cp.wait()
            start(idx + ahead)
            c = pc_ref[idx]
            ok = pk_ref[idx] > 0
            first = jnp.where(ok, jnp.clip(bef_ref[c * N_EXP + e] - r0, 0, BLK), 0)
            last = jnp.where(ok, jnp.clip(aft_ref[c * N_EXP + e] - r0, 0, BLK), 0)
            plan.append((half, slot, ok, jnp.minimum(first // GWIN, BLK // GWIN - 2), last))

        def fill(half, slot, ok, w, rows_at):
            w0 = pl.multiple_of(w * GWIN, GWIN)
            n = rows_at.shape[0]
            hit = (pbuf[slot] == rows_at + w0) & ok
            acc_ref[half, pl.ds(w0, n), :] += jnp.dot(hit.astype(BF16), hbuf[slot], preferred_element_type=F32)
            gacc_ref[half, pl.ds(w0, n), :] += jnp.sum(jnp.where(hit, cbuf[slot], 0.0), axis=1, keepdims=True)

        for half, slot, ok, w_first, last in plan:
            fill(half, slot, ok, w_first, row2)

        for half, slot, ok, w_first, last in plan:
            def window(w, c2, half=half, slot=slot, ok=ok):
                fill(half, slot, ok, w, row)
                return c2

            lax.fori_loop(w_first + 2, (last + GWIN - 1) // GWIN, window, 0)
        return carry

    lax.fori_loop(0, pn_ref[d] // 2, body, 0)
    o_ref[...] = (acc_ref[0] + acc_ref[1]).astype(BF16)
    gs_ref[...] = gacc_ref[0] + gacc_ref[1]


def _gather(h, pos_t, cw_t, blk_e, pbase, pcount, r0, before, after, pair_c, pair_e, pair_ok, ptotal, nblk):
    hbm = pl.BlockSpec(memory_space=pl.ANY)
    return pl.pallas_call(
        _gather_kernel,
        out_shape=(jax.ShapeDtypeStruct((nblk * BLK, D), BF16), jax.ShapeDtypeStruct((nblk * BLK, 1), F32)),
        grid_spec=pltpu.PrefetchScalarGridSpec(
            num_scalar_prefetch=10,
            grid=(nblk,),
            in_specs=[hbm, hbm, hbm],
            out_specs=(pl.BlockSpec((BLK, D), lambda d, *_: (d, 0)), pl.BlockSpec((BLK, 1), lambda d, *_: (d, 0))),
            scratch_shapes=[pltpu.VMEM((GSLOTS, CHUNK, D), BF16), pltpu.VMEM((GSLOTS, 1, CHUNK), I32),
                            pltpu.VMEM((GSLOTS, 1, CHUNK), F32), pltpu.VMEM((2, BLK, D), F32),
                            pltpu.VMEM((2, BLK, 1), F32), pltpu.SemaphoreType.DMA((3, GSLOTS))]),
        compiler_params=_params(("arbitrary",)),
        name="moe_gather",
    )(blk_e, pbase, pcount, r0, before, after, pair_c, pair_e, pair_ok, ptotal, h, pos_t, cw_t)


def _expert_up_kernel(be_ref, nv_ref, x_ref, wg_ref, wu_ref, o_ref, wgbf_ref, wubf_ref):
    d = pl.program_id(1)
    changed = (d == 0) | (be_ref[d] != be_ref[jnp.maximum(d - 1, 0)])

    @pl.when(changed)
    def _():
        wgbf_ref[...] = wg_ref[0].astype(BF16)
        wubf_ref[...] = wu_ref[0].astype(BF16)

    def up(rows):
        x = x_ref[rows, :]
        a = jnp.dot(x, wgbf_ref[...], preferred_element_type=F32)
        b = jnp.dot(x, wubf_ref[...], preferred_element_type=F32)
        o_ref[rows, :] = (a * jax.nn.sigmoid(a) * b).astype(BF16)

    half = BLK // 2
    nrows = nv_ref[d]

    @pl.when(nrows > half)
    def _():
        up(slice(0, BLK))

    @pl.when((nrows > 0) & (nrows <= half))
    def _():
        up(slice(0, half))
        o_ref[half:, :] = jnp.zeros((BLK - half, o_ref.shape[1]), BF16)

    @pl.when(nrows == 0)
    def _():
        o_ref[...] = jnp.zeros_like(o_ref)


def _expert_up(xs, blk_e, nvalid, wg, wu, tf):
    p = xs.shape[0]
    ff = wg.shape[2]
    return pl.pallas_call(
        _expert_up_kernel,
        out_shape=jax.ShapeDtypeStruct((p, ff), BF16),
        grid_spec=pltpu.PrefetchScalarGridSpec(
            num_scalar_prefetch=2,
            grid=(ff // tf, p // BLK),
            in_specs=[pl.BlockSpec((BLK, D), lambda f, d, be, nv: (d, 0)),
                      pl.BlockSpec((1, D, tf), lambda f, d, be, nv: (be[d], 0, f)),
                      pl.BlockSpec((1, D, tf), lambda f, d, be, nv: (be[d], 0, f))],
            out_specs=pl.BlockSpec((BLK, tf), lambda f, d, be, nv: (d, f)),
            scratch_shapes=[pltpu.VMEM((D, tf), BF16), pltpu.VMEM((D, tf), BF16)]),
        compiler_params=_params(("arbitrary", "arbitrary"), V7X_VMEM_LIMIT),
        name="moe_up",
    )(blk_e, nvalid, xs, wg, wu)


def _expert_down_kernel(be_ref, nv_ref, h_ref, gs_ref, wd_ref, o_ref, wdbf_ref):
    d = pl.program_id(1)
    changed = (d == 0) | (be_ref[d] != be_ref[jnp.maximum(d - 1, 0)])

    @pl.when(changed)
    def _():
        wdbf_ref[...] = wd_ref[0].astype(BF16)

    def down(rows):
        y = jnp.dot(h_ref[rows, :], wdbf_ref[...], preferred_element_type=F32)
        o_ref[rows, :] = (y * gs_ref[rows, :]).astype(BF16)

    half = BLK // 2
    nrows = nv_ref[d]

    @pl.when(nrows > half)
    def _():
        down(slice(0, BLK))

    @pl.when((nrows > 0) & (nrows <= half))
    def _():
        down(slice(0, half))
        o_ref[half:, :] = jnp.zeros((BLK - half, o_ref.shape[1]), BF16)

    @pl.when(nrows == 0)
    def _():
        o_ref[...] = jnp.zeros_like(o_ref)


def _expert_down(hh, gsort, blk_e, nvalid, wd, tn):
    p, ff = hh.shape
    return pl.pallas_call(
        _expert_down_kernel,
        out_shape=jax.ShapeDtypeStruct((p, D), BF16),
        grid_spec=pltpu.PrefetchScalarGridSpec(
            num_scalar_prefetch=2,
            grid=(D // tn, p // BLK),
            in_specs=[pl.BlockSpec((BLK, ff), lambda n, d, be, nv: (d, 0)),
                      pl.BlockSpec((BLK, 1), lambda n, d, be, nv: (d, 0)),
                      pl.BlockSpec((1, ff, tn), lambda n, d, be, nv: (be[d], 0, n))],
            out_specs=pl.BlockSpec((BLK, tn), lambda n, d, be, nv: (d, n)),
            scratch_shapes=[pltpu.VMEM((ff, tn), BF16)]),
        compiler_params=_params(("arbitrary", "arbitrary"), V7X_VMEM_LIMIT),
        name="moe_down",
    )(blk_e, nvalid, hh, gsort, wd)


def _combine_kernel(rs_ref, rc_ref, xp_ref, xs_ref, pos_ref, g_ref, y_hbm, op_ref, os_ref, wbuf, obuf, acc_ref,
                    sem, osem, *, n_prompt_chunks):
    i = pl.program_id(0)
    nch = pl.num_programs(0)

    def window_start(chunk, e):
        return pl.multiple_of(rs_ref[chunk * N_EXP + e] // BF16_ROWS * BF16_ROWS, BF16_ROWS)

    def copy(chunk, e, slot):
        return pltpu.make_async_copy(y_hbm.at[pl.ds(window_start(chunk, e), CWIN)],
                                     wbuf.at[slot, pl.ds(e * CWIN, CWIN)], sem.at[slot, e])

    @pl.when(i == 0)
    def _():
        for e in range(N_EXP):
            copy(0, e, 0).start()

    slot = i % 2

    @pl.when(i + 1 < nch)
    def _():
        for e in range(N_EXP):
            copy(i + 1, e, 1 - slot).start()

    col = lax.broadcasted_iota(I32, (CHUNK, CWIN), 1)
    hits = []
    for e in range(N_EXP):
        copy(i, e, slot).wait()
        hits.append((pos_ref[:, e:e + 1] - window_start(i, e)) == col)
    onehot = jnp.concatenate(hits, axis=1).astype(BF16)
    acc_ref[...] = jnp.dot(onehot, wbuf[slot], preferred_element_type=F32)

    for e in range(N_EXP):
        for k in range(1, (CHUNK + BF16_ROWS - 1) // CWIN + 1):
            tail = window_start(i, e) + k * CWIN

            @pl.when(rs_ref[i * N_EXP + e] + rc_ref[i * N_EXP + e] > tail)
            def _():
                cp = pltpu.make_async_copy(y_hbm.at[pl.ds(tail, CWIN)], obuf, osem)
                cp.start()
                cp.wait()
                hit = ((pos_ref[:, e:e + 1] - tail) == col).astype(BF16)
                acc_ref[...] += jnp.dot(hit, obuf[...], preferred_element_type=F32)

    x = jnp.where(i < n_prompt_chunks, xp_ref[...], xs_ref[...])
    out = x + _rms(acc_ref[...], g_ref[...])

    @pl.when(i < n_prompt_chunks)
    def _():
        op_ref[...] = out

    @pl.when(i >= n_prompt_chunks)
    def _():
        os_ref[...] = out


def _combine(run_start, run_count, xp, xs, pos, g, ys):
    npc = xp.shape[0] // CHUNK
    nsc = xs.shape[0] // CHUNK
    nch = npc + nsc
    tok8 = pl.BlockSpec((CHUNK, N_EXP), lambda i, *_: (i, 0))
    pspec = pl.BlockSpec((CHUNK, D), lambda i, *_: (jnp.minimum(i, npc - 1), 0))
    sspec = pl.BlockSpec((CHUNK, D), lambda i, *_: (jnp.maximum(i - npc, 0), 0))
    return pl.pallas_call(
        functools.partial(_combine_kernel, n_prompt_chunks=npc),
        out_shape=(jax.ShapeDtypeStruct(xp.shape, F32), jax.ShapeDtypeStruct(xs.shape, F32)),
        grid_spec=pltpu.PrefetchScalarGridSpec(
            num_scalar_prefetch=2,
            grid=(nch,),
            in_specs=[pspec, sspec, tok8, pl.BlockSpec((1, D), lambda i, *_: (0, 0)),
                      pl.BlockSpec(memory_space=pl.ANY)],
            out_specs=(pspec, sspec),
            scratch_shapes=[pltpu.VMEM((2, N_EXP * CWIN, D), BF16), pltpu.VMEM((CWIN, D), BF16),
                            pltpu.VMEM((CHUNK, D), F32),
                            pltpu.SemaphoreType.DMA((2, N_EXP)), pltpu.SemaphoreType.DMA]),
        compiler_params=_params(("arbitrary",), V7X_VMEM_LIMIT),
        name="moe_combine",
    )(run_start, run_count, xp, xs, pos, g, ys)


def _moe(xp, xs, g4, g5, w_router, wg, wu, wd):
    npc = xp.shape[0] // CHUNK
    nch = npc + xs.shape[0] // CHUNK
    t = nch * CHUNK
    h, rk, rk_t, cw_t, cnt = _route(xp, xs, g4, w_router.T)

    after = cnt.reshape(nch, N_EXP).astype(I32)
    before = jnp.concatenate([jnp.zeros((1, N_EXP), I32), after[:-1]], axis=0)
    counts = after[-1]
    gsz = (counts + BLK - 1) // BLK * BLK
    gend = jnp.cumsum(gsz)
    gstart = gend - gsz
    nblk = (2 * t + N_EXP * (BLK - 1)) // BLK + 1 + (2 * CWIN + BLK - 1) // BLK
    bstart = jnp.arange(nblk, dtype=I32) * BLK
    blk_e = jnp.minimum(jnp.sum(gend[None, :] <= bstart[:, None], axis=1), N_EXP - 1).astype(I32)
    valid = bstart < gend[-1]
    r0 = bstart - gstart[blk_e]
    aft_e = after[:, blk_e]
    bef_e = before[:, blk_e]
    c_lo = jnp.where(valid, jnp.sum(aft_e <= r0[None, :], axis=0), 0).astype(I32)
    c_hi = jnp.where(valid, jnp.sum(bef_e < (r0 + BLK)[None, :], axis=0) - 1, -1).astype(I32)
    c_lo = jnp.minimum(c_lo, nch - 1)
    pos = jnp.where(rk >= 0, rk + gstart[None, :], -1).astype(I32)
    pos_t = jnp.where(rk_t >= 0, rk_t + gstart[:, None], -1).astype(I32).reshape(N_EXP, nch, 1, CHUNK)
    cw_t = cw_t.reshape(N_EXP, nch, 1, CHUNK)
    run_start = (gstart[None, :] + before).astype(I32).reshape(-1)
    run_count = (after - before).reshape(-1)
    nvalid = jnp.where(valid, jnp.clip(counts[blk_e] - r0, 0, BLK), 0).astype(I32)
    nsrc = jnp.maximum(c_hi - c_lo + 1, 0)
    pcount = nsrc + (nsrc & 1)
    pend = jnp.cumsum(pcount)
    pbase = pend - pcount
    k = jnp.arange(2 * nblk + 2 * nch * N_EXP, dtype=I32)
    pair_blk = jnp.minimum(jnp.sum(pend[None, :] <= k[:, None], axis=1), nblk - 1)
    k_local = k - pbase[pair_blk]
    pair_ok = (k_local < nsrc[pair_blk]).astype(I32)
    pair_c = jnp.clip(c_lo[pair_blk] + jnp.minimum(k_local, nsrc[pair_blk] - 1), 0, nch - 1).astype(I32)
    pair_e = blk_e[pair_blk]

    xsort, gsort = _gather(h, pos_t, cw_t, blk_e, pbase.astype(I32), pcount.astype(I32), r0.astype(I32),
                           before.reshape(-1), after.reshape(-1), pair_c, pair_e, pair_ok,
                           pend[-1:].astype(I32), nblk)
    hh = _expert_up(xsort, blk_e, nvalid, wg, wu, tf=1792)
    ysort = _expert_down(hh, gsort, blk_e, nvalid, wd, tn=D)
    return _combine(run_start, run_count, xp, xs, pos, g5, ysort)


def kernel(x_prompt, x_sample, cache_conv, cache_mem_k, cache_mem_v, state_ssm_re, state_ssm_im, mem_prompt, norm_g, mem_norm_g, w_xq, w_xk, w_xv, w_xo, conv_w_pw1, conv_b_pw1, conv_w_dw, conv_b_dw, conv_ln_g, conv_ln_b, conv_w_pw2, conv_b_pw2, ssm_a_re, ssm_a_im, ssm_log_dt, ssm_b_re, ssm_b_im, ssm_c_re, ssm_c_im, ssm_d, ssm_w_glu, ssm_b_glu, ffn_w_gate, ffn_w_up, ffn_w_down, moe_w_router, moe_w_gate, moe_w_up, moe_w_down):
    nbp, seqp, _ = x_prompt.shape
    nbs, seqs, _ = x_sample.shape
    tp = nbp * seqp
    ts = nbs * seqs
    row = lambda a: a.reshape(1, -1)
    g = lambda i, k: norm_g[i, k].reshape(1, D)

    nl = w_xk.shape[0]
    pk, pv = _mem_kv(mem_prompt, mem_norm_g.reshape(nl, 1, D), w_xk, w_xv)
    p_mem_k = pk.reshape(nl, nbp, N_MEM, N_HEADS, HEAD_DIM)
    p_mem_v = pv.reshape(nl, nbp, N_MEM, N_HEADS, HEAD_DIM)
    ck = cache_mem_k.reshape(nl * nbs, N_MEM, N_HEADS, HEAD_DIM)
    cv = cache_mem_v.reshape(nl * nbs, N_MEM, N_HEADS, HEAD_DIM)

    conv_args = (conv_w_dw[0], row(conv_b_dw[0]), row(conv_ln_g[0]), row(conv_ln_b[0]),
                 conv_w_pw2[0], row(conv_b_pw2[0]), g(0, 1))
    xp = x_prompt.reshape(tp, D)
    xs = x_sample.reshape(ts, D)
    up = _conv_pw1(xp, g(0, 0), conv_w_pw1[0], row(conv_b_pw1[0]), tm=1024).reshape(nbp, seqp, D)
    us = _conv_pw1(xs, g(0, 0), conv_w_pw1[0], row(conv_b_pw1[0]), tm=ts).reshape(nbs, seqs, D)
    hist_p = jnp.zeros((nbp, HIST, D), F32)
    hist_s = jnp.pad(cache_conv[0], ((0, 0), (HIST - CONV_W + 1, 0), (0, 0)))
    xp = _conv_dw_pw2(up, x_prompt, hist_p, *conv_args, tl=512).reshape(tp, D)
    xs = _conv_dw_pw2(us, x_sample, hist_s, *conv_args, tl=seqs).reshape(ts, D)
    p_conv = up[:, seqp - (CONV_W - 1):][None]
    s_conv = jnp.concatenate([cache_conv[0], us], axis=1)[:, -(CONV_W - 1):][None]

    tma = 1024
    xp = _attn(xp, pk, pv, 0, w_xq[0], w_xo[0], g(0, 2), g(0, 3), nbp, seqp, tma)
    xs = _attn(xs, ck, cv, 0, w_xq[0], w_xo[0], g(0, 2), g(0, 3), nbs, seqs, seqs)

    ffn_w = (_to_bf16(ffn_w_gate[0], 4), _to_bf16(ffn_w_up[0], 4), _to_bf16(ffn_w_down[0], 4))
    tff = ffn_w_gate.shape[2] // 2
    xp = _ffn(xp, g(0, 4), g(0, 5), *ffn_w, tm=1024, tf=tff, out_shape=(tp, D), out_index=lambda i, f: (i, 0))
    xs = _ffn(xs, g(0, 4), g(0, 5), *ffn_w, tm=ts, tf=tff, out_shape=(ts, D), out_index=lambda i, f: (i, 0))

    (ab_r, ab_i, a2_r, a2_i), (bb_r, bb_i, abb_r, abb_i) = _ssm_prep(
        ssm_a_re[0], ssm_a_im[0], ssm_log_dt[0], ssm_b_re[0], ssm_b_im[0])
    bdiag = lambda m: _block_diag(m.transpose(0, 2, 1)).astype(BF16)
    bm_r, bm_i = bdiag(bb_r), bdiag(bb_i)
    cm_r, cm_i = bdiag(ssm_c_re[0]), bdiag(-ssm_c_im[0])
    tail_args = (cm_r, cm_i, row(ssm_d[0]), _to_bf16(ssm_w_glu[0], 4), row(ssm_b_glu[0]))
    half = SUBLANES // 2
    hdiag = lambda m: _block_diag(m.transpose(0, 2, 1), gpb=SSM_CB // SSM_C // 2).astype(BF16)
    pair_args = (jnp.concatenate([jnp.tile(ab_r, (half, 1)), jnp.tile(a2_r, (half, 1))]),
                 jnp.concatenate([jnp.tile(ab_i, (half, 1)), jnp.tile(a2_i, (half, 1))]),
                 jnp.concatenate([hdiag(bb_r), hdiag(abb_r)], axis=1),
                 jnp.concatenate([hdiag(bb_i), hdiag(abb_i)], axis=1))
    zero_state = jnp.zeros((SUBLANES, SSM_N), F32)
    xp, p_sr, p_si = _ssm(xp.reshape(nbp, seqp, D), g(1, 0), g(1, 1), zero_state, zero_state,
                          *pair_args, *tail_args, tl=64)
    xs, s_sr, s_si = _ssm(xs.reshape(nbs, seqs, D), g(1, 0), g(1, 1), state_ssm_re[0].reshape(nbs, SSM_N),
                          state_ssm_im[0].reshape(nbs, SSM_N), jnp.tile(ab_r, (nbs, 1)), jnp.tile(ab_i, (nbs, 1)),
                          bm_r, bm_i, *tail_args, tl=seqs)
    st = lambda a, n: a[:n].reshape(1, n, SSM_G, SSM_P)
    p_ssm_re, p_ssm_im = st(p_sr, nbp), st(p_si, nbp)
    s_ssm_re, s_ssm_im = st(s_sr, nbs), st(s_si, nbs)

    xp = _attn(xp.reshape(tp, D), pk, pv, nbp, w_xq[1], w_xo[1], g(1, 2), g(1, 3), nbp, seqp, tma)
    xs = _attn(xs.reshape(ts, D), ck, cv, nbs, w_xq[1], w_xo[1], g(1, 2), g(1, 3), nbs, seqs, seqs)

    yp, ysm = _moe(xp, xs, g(1, 4), g(1, 5), moe_w_router[0], moe_w_gate[0], moe_w_up[0], moe_w_down[0])
    return (yp.reshape(nbp, seqp, D), ysm.reshape(nbs, seqs, D), p_conv, p_mem_k, p_mem_v,
            p_ssm_re, p_ssm_im, s_conv, s_ssm_re, s_ssm_im)
```

```python
import functools

import jax
import jax.numpy as jnp
from jax import lax
from jax.experimental import pallas as pl
from jax.experimental.pallas import tpu as pltpu

F32 = jnp.float32
BF16 = jnp.bfloat16
I32 = jnp.int32

D = 1024
CONV_W = 31
HIST = 32
N_MEM = 256
N_HEADS = 4
HEAD_DIM = D // N_HEADS
SSM_G = 64
SSM_C = 16
SSM_P = 64
SSM_N = SSM_G * SSM_P
SSM_CB = 256
SSM_NCB = D // SSM_CB
SSM_SB = SSM_CB // SSM_C * SSM_P
N_EXP = 8
EPS = 1e-6

V7X_VMEM_LIMIT = 56 * 1024 * 1024
SUBLANES = 8

CHUNK = 256
BLK = 512
GWIN = 128
GSLOTS = 8
CWIN = 256
BF16_ROWS = 16


def _params(sem, vmem=None):
    return pltpu.CompilerParams(dimension_semantics=sem, vmem_limit_bytes=vmem)


def _rms(x, g):
    return x * lax.rsqrt(jnp.mean(x * x, axis=-1, keepdims=True) + EPS) * g


def _const_spec(shape):
    nd = len(shape)
    return pl.BlockSpec(shape, lambda *_: (0,) * nd)


def _pw1_kernel(x_ref, g_ref, w_ref, b_ref, u_ref, wbf_ref):
    @pl.when(pl.program_id(0) == 0)
    def _():
        wbf_ref[...] = w_ref[...].astype(BF16)

    h = _rms(x_ref[...], g_ref[...]).astype(BF16)
    z = jnp.dot(h, wbf_ref[...], preferred_element_type=F32) + b_ref[...]
    u_ref[...] = z[:, :D] * jax.nn.sigmoid(z[:, D:])


def _conv_pw1(x, g, w, b, tm):
    t = x.shape[0]
    return pl.pallas_call(
        _pw1_kernel,
        out_shape=jax.ShapeDtypeStruct((t, D), F32),
        grid=(t // tm,),
        in_specs=[pl.BlockSpec((tm, D), lambda i: (i, 0)),
                  _const_spec((1, D)), _const_spec((D, 2 * D)), _const_spec((1, 2 * D))],
        out_specs=pl.BlockSpec((tm, D), lambda i: (i, 0)),
        scratch_shapes=[pltpu.VMEM((D, 2 * D), BF16)],
        compiler_params=_params(("arbitrary",), V7X_VMEM_LIMIT),
        name="conv_pw1",
    )(x, g, w, b)


def _conv2_kernel(u_ref, x_ref, hist_ref, wdw_ref, bdw_ref, lng_ref, lnb_ref, w2_ref, b2_ref,
                  g_ref, o_ref, ext_ref, sh_ref, conv_ref, w2bf_ref, wtap_ref, *, tl, rt):
    bi = pl.program_id(0)
    li = pl.program_id(1)

    @pl.when((bi == 0) & (li == 0))
    def _():
        w2bf_ref[...] = w2_ref[...].astype(BF16)
        for k in range(CONV_W):
            wtap_ref[k] = jnp.broadcast_to(wdw_ref[k:k + 1, :], (SUBLANES, D))

    @pl.when(li == 0)
    def _():
        ext_ref[0:HIST, :] = hist_ref[0]

    @pl.when(li > 0)
    def _():
        ext_ref[0:HIST, :] = ext_ref[tl:tl + HIST, :]

    ext_ref[HIST:HIST + tl, :] = u_ref[0]
    for s in range(1, SUBLANES):
        sh_ref[s - 1] = ext_ref[pl.ds(s, tl + HIST - SUBLANES), :]

    def rows(i, carry):
        r0 = pl.multiple_of(i * rt, rt)
        ntile = rt // SUBLANES
        accs = [jnp.zeros((SUBLANES, D), F32) + bdw_ref[...] for _ in range(ntile)]
        for k in range(CONV_W):
            off = HIST - CONV_W + 1 + k
            s = off % SUBLANES
            base = off - s
            w = wtap_ref[k]
            for ti in range(ntile):
                rows_ti = pl.ds(r0 + base + ti * SUBLANES, SUBLANES)
                src = ext_ref[rows_ti, :] if s == 0 else sh_ref[s - 1, rows_ti, :]
                accs[ti] = accs[ti] + w * src
        for ti in range(ntile):
            conv_ref[pl.ds(r0 + ti * SUBLANES, SUBLANES), :] = accs[ti]
        return carry

    lax.fori_loop(0, tl // rt, rows, 0)
    acc = conv_ref[...]
    mu = jnp.mean(acc, axis=-1, keepdims=True)
    xc = acc - mu
    var = jnp.mean(xc * xc, axis=-1, keepdims=True)
    y = xc * lax.rsqrt(var + EPS) * lng_ref[...] + lnb_ref[...]
    y = y * jax.nn.sigmoid(y)
    t = jnp.dot(y.astype(BF16), w2bf_ref[...], preferred_element_type=F32) + b2_ref[...]
    o_ref[0] = x_ref[0] + _rms(t, g_ref[...])


def _conv_dw_pw2(u, x, hist, wdw, bdw, lng, lnb, w2, b2, g, tl):
    nb, seq, _ = u.shape
    tok = pl.BlockSpec((1, tl, D), lambda b, l: (b, l, 0))
    return pl.pallas_call(
        functools.partial(_conv2_kernel, tl=tl, rt=min(tl, 32)),
        out_shape=jax.ShapeDtypeStruct((nb, seq, D), F32),
        grid=(nb, seq // tl),
        in_specs=[tok, tok, pl.BlockSpec((1, HIST, D), lambda b, l: (b, 0, 0)),
                  _const_spec((CONV_W, D)), _const_spec((1, D)), _const_spec((1, D)), _const_spec((1, D)),
                  _const_spec((D, D)), _const_spec((1, D)), _const_spec((1, D))],
        out_specs=tok,
        scratch_shapes=[pltpu.VMEM((tl + HIST, D), F32),
                        pltpu.VMEM((SUBLANES - 1, tl + HIST - SUBLANES, D), F32),
                        pltpu.VMEM((tl, D), F32), pltpu.VMEM((D, D), BF16),
                        pltpu.VMEM((CONV_W, SUBLANES, D), F32)],
        compiler_params=_params(("arbitrary", "arbitrary"), V7X_VMEM_LIMIT),
        name="conv_dw_pw2",
    )(u, x, hist, wdw, bdw, lng, lnb, w2, b2, g)


def _memkv_kernel(m_ref, g_ref, wk_ref, wv_ref, k_ref, v_ref, wkbf_ref, wvbf_ref):
    @pl.when(pl.program_id(1) == 0)
    def _():
        wkbf_ref[...] = wk_ref[0].astype(BF16)
        wvbf_ref[...] = wv_ref[0].astype(BF16)

    m = _rms(m_ref[0], g_ref[0]).astype(BF16)
    k = jnp.dot(m, wkbf_ref[...], preferred_element_type=F32)
    v = jnp.dot(m, wvbf_ref[...], preferred_element_type=F32)
    for hd in range(N_HEADS):
        sl = slice(hd * HEAD_DIM, (hd + 1) * HEAD_DIM)
        k_ref[0, :, hd, :] = k[:, sl]
        v_ref[0, :, hd, :] = v[:, sl]


def _mem_kv(mem, g, wk, wv):
    nb = mem.shape[0]
    nl = wk.shape[0]
    kv = jax.ShapeDtypeStruct((nl * nb, N_MEM, N_HEADS, HEAD_DIM), F32)
    out = pl.BlockSpec((1, N_MEM, N_HEADS, HEAD_DIM), lambda l, b: (l * nb + b, 0, 0, 0))
    w = pl.BlockSpec((1, D, D), lambda l, b: (l, 0, 0))
    return pl.pallas_call(
        _memkv_kernel,
        out_shape=(kv, kv),
        grid=(nl, nb),
        in_specs=[pl.BlockSpec((1, N_MEM, D), lambda l, b: (b, 0, 0)),
                  pl.BlockSpec((1, 1, D), lambda l, b: (l, 0, 0)), w, w],
        out_specs=(out, out),
        scratch_shapes=[pltpu.VMEM((D, D), BF16), pltpu.VMEM((D, D), BF16)],
        compiler_params=_params(("arbitrary", "arbitrary"), V7X_VMEM_LIMIT),
        name="mem_kv",
    )(mem, g, wk, wv)


def _attn_kernel(x_ref, k_ref, v_ref, wq_ref, wo_ref, g2_ref, g3_ref, o_ref, wqbf_ref, wobf_ref, kb_ref, vb_ref):
    @pl.when((pl.program_id(0) == 0) & (pl.program_id(1) == 0))
    def _():
        wqbf_ref[...] = wq_ref[...].astype(BF16)
        wobf_ref[...] = wo_ref[...].astype(BF16)

    @pl.when(pl.program_id(1) == 0)
    def _():
        for hd in range(N_HEADS):
            sl = slice(hd * HEAD_DIM, (hd + 1) * HEAD_DIM)
            kb_ref[:, sl] = k_ref[0, :, hd, :].astype(BF16)
            vb_ref[:, sl] = v_ref[0, :, hd, :].astype(BF16)

    x = x_ref[...]
    h = _rms(x, g2_ref[...]).astype(BF16)
    q = (jnp.dot(h, wqbf_ref[...], preferred_element_type=F32) * (HEAD_DIM ** -0.5)).astype(BF16)
    heads = []
    for hd in range(N_HEADS):
        sl = slice(hd * HEAD_DIM, (hd + 1) * HEAD_DIM)
        s = lax.dot_general(q[:, sl], kb_ref[:, sl], (((1,), (1,)), ((), ())), preferred_element_type=F32)
        p = jnp.exp(s - jnp.max(s, axis=-1, keepdims=True))
        p = p / jnp.sum(p, axis=-1, keepdims=True)
        heads.append(jnp.dot(p.astype(BF16), vb_ref[:, sl], preferred_element_type=F32))
    o = jnp.concatenate(heads, axis=1).astype(BF16)
    t = jnp.dot(o, wobf_ref[...], preferred_element_type=F32)
    o_ref[...] = x + _rms(t, g3_ref[...])


def _attn(x, k, v, kv_base, wq, wo, g2, g3, nb, seq, tm):
    tok = pl.BlockSpec((tm, D), lambda b, l: (b * (seq // tm) + l, 0))
    kv = pl.BlockSpec((1, N_MEM, N_HEADS, HEAD_DIM), lambda b, l: (kv_base + b, 0, 0, 0))
    return pl.pallas_call(
        _attn_kernel,
        out_shape=jax.ShapeDtypeStruct((nb * seq, D), F32),
        grid=(nb, seq // tm),
        in_specs=[tok, kv, kv, _const_spec((D, D)), _const_spec((D, D)), _const_spec((1, D)), _const_spec((1, D))],
        out_specs=tok,
        scratch_shapes=[pltpu.VMEM((D, D), BF16), pltpu.VMEM((D, D), BF16),
                        pltpu.VMEM((N_MEM, D), BF16), pltpu.VMEM((N_MEM, D), BF16)],
        compiler_params=_params(("arbitrary", "arbitrary"), V7X_VMEM_LIMIT),
        name="mem_attn",
    )(x, k, v, wq, wo, g2, g3)


def _cast_kernel(w_ref, o_ref):
    o_ref[...] = w_ref[...].astype(BF16)


def _to_bf16(w, nsplit):
    r, c = w.shape
    blk = pl.BlockSpec((r // nsplit, c), lambda i: (i, 0))
    return pl.pallas_call(
        _cast_kernel,
        out_shape=jax.ShapeDtypeStruct((r, c), BF16),
        grid=(nsplit,),
        in_specs=[blk],
        out_specs=blk,
        compiler_params=_params(("arbitrary",)),
        name="to_bf16",
    )(w)


def _ffn_kernel(x_ref, g4_ref, g5_ref, wg_ref, wu_ref, wd_ref, o_ref, h_ref, acc_ref):
    f = pl.program_id(1)

    @pl.when(f == 0)
    def _():
        h_ref[...] = _rms(x_ref[...], g4_ref[...]).astype(BF16)
        acc_ref[...] = jnp.zeros_like(acc_ref)

    h = h_ref[...]
    a = jnp.dot(h, wg_ref[...], preferred_element_type=F32)
    b = jnp.dot(h, wu_ref[...], preferred_element_type=F32)
    hh = (a * jax.nn.sigmoid(a) * b).astype(BF16)
    acc_ref[...] += jnp.dot(hh, wd_ref[...], preferred_element_type=F32)

    @pl.when(f == pl.num_programs(1) - 1)
    def _():
        o_ref[...] = x_ref[...] + _rms(acc_ref[...], g5_ref[...])


def _ffn(x, g4, g5, wg, wu, wd, tm, tf, out_shape, out_index):
    t = x.shape[0]
    ff = wg.shape[1]
    return pl.pallas_call(
        _ffn_kernel,
        out_shape=jax.ShapeDtypeStruct(out_shape, F32),
        grid=(t // tm, ff // tf),
        in_specs=[pl.BlockSpec((tm, D), lambda i, f: (i, 0)),
                  _const_spec((1, D)), _const_spec((1, D)),
                  pl.BlockSpec((D, tf), lambda i, f: (0, f)),
                  pl.BlockSpec((D, tf), lambda i, f: (0, f)),
                  pl.BlockSpec((tf, D), lambda i, f: (f, 0))],
        out_specs=pl.BlockSpec((tm, D), out_index),
        scratch_shapes=[pltpu.VMEM((tm, D), BF16), pltpu.VMEM((tm, D), F32)],
        compiler_params=_params(("arbitrary", "arbitrary"), V7X_VMEM_LIMIT),
        name="dense_ffn",
    )(x, g4, g5, wg, wu, wd)


def _ssm_prep_kernel(lr_ref, li_ref, ldt_ref, br_ref, bi_ref, abr_ref, abi_ref, bbr_ref, bbi_ref,
                     a2r_ref, a2i_ref, abbr_ref, abbi_ref):
    dt = jnp.exp(ldt_ref[...])
    lr = lr_ref[...]
    li = li_ref[...]
    mag = jnp.exp(lr * dt)
    ab_r = mag * jnp.cos(li * dt)
    ab_i = mag * jnp.sin(li * dt)
    den = lr * lr + li * li
    nr = ab_r - 1.0
    k_r = (nr * lr + ab_i * li) / den
    k_i = (ab_i * lr - nr * li) / den
    br = br_ref[...]
    bi = bi_ref[...]
    bb_r = k_r * br - k_i * bi
    bb_i = k_r * bi + k_i * br
    abr_ref[...] = ab_r
    abi_ref[...] = ab_i
    bbr_ref[...] = bb_r
    bbi_ref[...] = bb_i
    a2r_ref[...] = ab_r * ab_r - ab_i * ab_i
    a2i_ref[...] = 2.0 * (ab_r * ab_i)
    abbr_ref[...] = ab_r * bb_r - ab_i * bb_i
    abbi_ref[...] = ab_r * bb_i + ab_i * bb_r


def _ssm_prep(a_re, a_im, log_dt, b_re, b_im):
    n = SSM_P * SSM_C
    rep = lambda a: jnp.repeat(a, SSM_C, axis=1)
    shp = jax.ShapeDtypeStruct((SSM_G, n), F32)
    abr, abi, bbr, bbi, a2r, a2i, abbr, abbi = pl.pallas_call(
        _ssm_prep_kernel,
        out_shape=(shp,) * 8,
        name="ssm_prep",
    )(rep(a_re), rep(a_im), log_dt.reshape(SSM_G, 1), b_re.reshape(SSM_G, n), b_im.reshape(SSM_G, n))
    pick = lambda a: a.reshape(SSM_G, SSM_P, SSM_C)[:, :, 0].reshape(1, SSM_N)
    gpc = lambda a: a.reshape(SSM_G, SSM_P, SSM_C)
    return (pick(abr), pick(abi), pick(a2r), pick(a2i)), (gpc(bbr), gpc(bbi), gpc(abbr), gpc(abbi))


def _block_diag(m, gpb=SSM_CB // SSM_C):
    g, a, b = m.shape
    m = m.reshape(g // gpb, gpb, a, 1, b)
    eye = jnp.eye(gpb, dtype=m.dtype).reshape(1, gpb, 1, gpb, 1)
    return (m * eye).reshape(g // gpb, gpb * a, gpb * b)


def _ssm_kernel(x_ref, g0_ref, g1_ref, s0r_ref, s0i_ref, ar_ref, ai_ref, bmr_ref, bmi_ref, cmr_ref, cmi_ref,
                d_ref, wglu_ref, bglu_ref, o_ref, sr_ref, si_ref,
                bur_ref, bui_ref, *, nb, rb, lc):
    @pl.when(pl.program_id(0) == 0)
    def _():
        sr_ref[...] = s0r_ref[...]
        si_ref[...] = s0i_ref[...]

    tl = rb // nb
    x = x_ref[...].reshape(rb, D)
    h = _rms(x, g0_ref[...])
    r = lax.broadcasted_iota(I32, (rb, rb), 0)
    c = lax.broadcasted_iota(I32, (rb, rb), 1)
    to_time_major = (c == (r & (nb - 1)) * tl + (r >> (nb.bit_length() - 1))).astype(BF16)
    to_batch_major = (c == (r & (tl - 1)) * nb + (r >> (tl.bit_length() - 1))).astype(BF16)
    hb = jnp.dot(to_time_major, h.astype(BF16), preferred_element_type=F32).astype(BF16)
    rows = max(nb, SUBLANES)
    pair = nb < SUBLANES
    if pair:
        odd = ((r >> (nb.bit_length() - 1)) & 1) == 1
        prev_time_major = ((c == (r & (nb - 1)) * tl + (r >> (nb.bit_length() - 1)) - 1) & odd).astype(BF16)
        hprev = jnp.dot(prev_time_major, h.astype(BF16), preferred_element_type=F32).astype(BF16)
        low = lax.broadcasted_iota(I32, (SUBLANES, lc), 0) < nb

    def scan_chunk(cs):
        a_r = ar_ref[:, cs]
        a_i = ai_ref[:, cs]
        s_r = sr_ref[:, cs]
        s_i = si_ref[:, cs]
        for j in range(rb // rows):
            rs = slice(j * rows, (j + 1) * rows)
            n_r = a_r * s_r - a_i * s_i + bur_ref[rs, cs]
            n_i = a_r * s_i + a_i * s_r + bui_ref[rs, cs]
            bur_ref[rs, cs] = n_r
            bui_ref[rs, cs] = n_i
            if pair:
                s_r = jnp.where(low, pltpu.roll(n_r, nb, 0), n_r)
                s_i = jnp.where(low, pltpu.roll(n_i, nb, 0), n_i)
            else:
                s_r, s_i = n_r, n_i
        sr_ref[:, cs] = s_r
        si_ref[:, cs] = s_i

    ist = bmr_ref.shape[2]
    ich = bmr_ref.shape[1] // 2 if pair else bmr_ref.shape[1]
    ys = []
    for cb in range(SSM_NCB):
        ss = slice(cb * SSM_SB, (cb + 1) * SSM_SB)
        for ib in range(cb * (SSM_SB // ist), (cb + 1) * (SSM_SB // ist)):
            hs = hb[:, ib * ich:(ib + 1) * ich]
            if pair:
                hs = jnp.concatenate([hs, hprev[:, ib * ich:(ib + 1) * ich]], axis=1)
            bur_ref[:, ib * ist:(ib + 1) * ist] = jnp.dot(hs, bmr_ref[ib], preferred_element_type=F32)
            bui_ref[:, ib * ist:(ib + 1) * ist] = jnp.dot(hs, bmi_ref[ib], preferred_element_type=F32)
            for c in range(ist // lc):
                scan_chunk(slice(ib * ist + c * lc, ib * ist + (c + 1) * lc))
        ys.append(jnp.dot(bur_ref[:, ss].astype(BF16), cmr_ref[cb], preferred_element_type=F32)
                  + jnp.dot(bui_ref[:, ss].astype(BF16), cmi_ref[cb], preferred_element_type=F32))
    y_tm = jnp.concatenate(ys, axis=1)
    y1 = y_tm.astype(BF16)
    rem = y_tm - y1.astype(F32)
    y2 = rem.astype(BF16)
    y3 = (rem - y2.astype(F32)).astype(BF16)
    y = (jnp.dot(to_batch_major, y1, preferred_element_type=F32)
         + jnp.dot(to_batch_major, y2, preferred_element_type=F32)
         + jnp.dot(to_batch_major, y3, preferred_element_type=F32))
    y = y + d_ref[...] * h
    y = jax.nn.gelu(y).astype(BF16)
    z = jnp.dot(y, wglu_ref[...], preferred_element_type=F32) + bglu_ref[...]
    t = z[:, :D] * jax.nn.sigmoid(z[:, D:])
    o_ref[...] = (x + _rms(t, g1_ref[...])).reshape(nb, tl, D)


def _ssm(x, g0, g1, s0r, s0i, ab_r, ab_i, bm_r, bm_i, cm_r, cm_i, d, wglu, bglu, tl):
    nb, seq, _ = x.shape
    rb = nb * tl
    rows = max(nb, SUBLANES)
    st = jax.ShapeDtypeStruct((rows, SSM_N), F32)
    row = pl.BlockSpec((nb, tl, D), lambda i: (0, i, 0))
    return pl.pallas_call(
        functools.partial(_ssm_kernel, nb=nb, rb=rb, lc=512),
        out_shape=(jax.ShapeDtypeStruct((nb, seq, D), F32), st, st),
        grid=(seq // tl,),
        in_specs=[row, _const_spec((1, D)), _const_spec((1, D)),
                  _const_spec((rows, SSM_N)), _const_spec((rows, SSM_N)),
                  _const_spec((rows, SSM_N)), _const_spec((rows, SSM_N)),
                  _const_spec(bm_r.shape), _const_spec(bm_i.shape),
                  _const_spec((SSM_NCB, SSM_SB, SSM_CB)), _const_spec((SSM_NCB, SSM_SB, SSM_CB)),
                  _const_spec((1, D)), _const_spec((D, 2 * D)), _const_spec((1, 2 * D))],
        out_specs=(row, _const_spec((rows, SSM_N)), _const_spec((rows, SSM_N))),
        scratch_shapes=[pltpu.VMEM((rb, SSM_N), F32), pltpu.VMEM((rb, SSM_N), F32)],
        compiler_params=_params(("arbitrary",), V7X_VMEM_LIMIT),
        name="ssm",
    )(x, g0, g1, s0r, s0i, ab_r, ab_i, bm_r, bm_i, cm_r, cm_i, d, wglu, bglu)


def _route_kernel(xp_ref, xs_ref, g_ref, wrt_ref, h_ref, rk_ref, rkt_ref, cwt_ref, cnt_ref, carry_ref,
                  *, n_prompt_chunks):
    i = pl.program_id(0)

    @pl.when(i == 0)
    def _():
        carry_ref[...] = jnp.zeros_like(carry_ref)

    x = jnp.where(i < n_prompt_chunks, xp_ref[...], xs_ref[...])
    h = _rms(x, g_ref[...])
    h_ref[...] = h.astype(BF16)
    lg = lax.dot_general(wrt_ref[...], h, (((1,), (1,)), ((), ())), preferred_element_type=F32,
                         precision=lax.Precision.HIGHEST)
    ex = lax.broadcasted_iota(I32, lg.shape, 0)
    m1 = jnp.max(lg, axis=0, keepdims=True)
    i1 = jnp.min(jnp.where(lg == m1, ex, N_EXP), axis=0, keepdims=True)
    first = ex == i1
    lg2 = jnp.where(first, -jnp.inf, lg)
    m2 = jnp.max(lg2, axis=0, keepdims=True)
    i2 = jnp.min(jnp.where(lg2 == m2, ex, N_EXP), axis=0, keepdims=True)
    second = ex == i2
    e = jnp.exp(m2 - m1)
    den = 1.0 + e
    cwt_ref[...] = jnp.where(first, 1.0 / den, 0.0) + jnp.where(second, e / den, 0.0)
    assigned = first | second
    r = lax.broadcasted_iota(I32, (CHUNK, CHUNK), 0)
    c = lax.broadcasted_iota(I32, (CHUNK, CHUNK), 1)
    earlier = (r < c).astype(BF16)
    rank = jnp.dot(assigned.astype(BF16), earlier, preferred_element_type=F32) + carry_ref[...]
    rank = jnp.where(assigned, rank, -1.0)
    rkt_ref[...] = rank.astype(I32)
    rk_ref[...] = rank.T.astype(I32)
    carry_ref[...] += jnp.sum(assigned.astype(F32), axis=1, keepdims=True)
    cnt_ref[0] = carry_ref[...]


def _route(xp, xs, g, wrt):
    npc = xp.shape[0] // CHUNK
    nch = npc + xs.shape[0] // CHUNK
    t = nch * CHUNK
    exp_major = pl.BlockSpec((N_EXP, CHUNK), lambda i: (0, i))
    return pl.pallas_call(
        functools.partial(_route_kernel, n_prompt_chunks=npc),
        out_shape=(jax.ShapeDtypeStruct((t, D), BF16), jax.ShapeDtypeStruct((t, N_EXP), I32),
                   jax.ShapeDtypeStruct((N_EXP, t), I32), jax.ShapeDtypeStruct((N_EXP, t), F32),
                   jax.ShapeDtypeStruct((nch, N_EXP, 1), F32)),
        grid=(nch,),
        in_specs=[pl.BlockSpec((CHUNK, D), lambda i: (jnp.minimum(i, npc - 1), 0)),
                  pl.BlockSpec((CHUNK, D), lambda i: (jnp.maximum(i - npc, 0), 0)),
                  _const_spec((1, D)), _const_spec((N_EXP, D))],
        out_specs=(pl.BlockSpec((CHUNK, D), lambda i: (i, 0)),
                   pl.BlockSpec((CHUNK, N_EXP), lambda i: (i, 0)), exp_major, exp_major,
                   pl.BlockSpec((1, N_EXP, 1), lambda i: (i, 0, 0))),
        scratch_shapes=[pltpu.VMEM((N_EXP, 1), F32)],
        compiler_params=_params(("arbitrary",)),
        name="moe_route",
    )(xp, xs, g, wrt)


def _gather_kernel(be_ref, pb_ref, pn_ref, r0_ref, bef_ref, aft_ref, pc_ref, pe_ref, pk_ref, pt_ref,
                   h_hbm, pos_hbm, cw_hbm, o_ref, gs_ref, hbuf, pbuf, cbuf, acc_ref, gacc_ref, sem):
    d = pl.program_id(0)
    e = be_ref[d]
    base = pb_ref[d]
    r0 = r0_ref[d]
    total = pt_ref[0]
    ahead = GSLOTS - 2

    def copies(idx):
        slot = idx % GSLOTS
        c = pc_ref[idx]
        ee = pe_ref[idx]
        return (pltpu.make_async_copy(h_hbm.at[pl.ds(c * CHUNK, CHUNK)], hbuf.at[slot], sem.at[0, slot]),
                pltpu.make_async_copy(pos_hbm.at[ee, c], pbuf.at[slot], sem.at[1, slot]),
                pltpu.make_async_copy(cw_hbm.at[ee, c], cbuf.at[slot], sem.at[2, slot]))

    def start(idx):
        @pl.when(idx < total)
        def _():
            for cp in copies(idx):
                cp.start()

    @pl.when(d == 0)
    def _():
        for k in range(ahead):
            start(k)

    acc_ref[...] = jnp.zeros_like(acc_ref)
    gacc_ref[...] = jnp.zeros_like(gacc_ref)
    row = d * BLK + lax.broadcasted_iota(I32, (GWIN, CHUNK), 0)
    row2 = d * BLK + lax.broadcasted_iota(I32, (2 * GWIN, CHUNK), 0)

    def body(j, carry):
        plan = []
        for half in range(2):
            idx = base + 2 * j + half
            slot = idx % GSLOTS
            for cp in copies(idx):
                cp.wait()
            start(idx + ahead)
            c = pc_ref[idx]
            ok = pk_ref[idx] > 0
            first = jnp.where(ok, jnp.clip(bef_ref[c * N_EXP + e] - r0, 0, BLK), 0)
            last = jnp.where(ok, jnp.clip(aft_ref[c * N_EXP + e] - r0, 0, BLK), 0)
            plan.append((half, slot, ok, jnp.minimum(first // GWIN, BLK // GWIN - 2), last))

        def fill(half, slot, ok, w, rows_at):
            w0 = pl.multiple_of(w * GWIN, GWIN)
            n = rows_at.shape[0]
            hit = (pbuf[slot] == rows_at + w0) & ok
            acc_ref[half, pl.ds(w0, n), :] += jnp.dot(hit.astype(BF16), hbuf[slot], preferred_element_type=F32)
            gacc_ref[half, pl.ds(w0, n), :] += jnp.sum(jnp.where(hit, cbuf[slot], 0.0), axis=1, keepdims=True)

        for half, slot, ok, w_first, last in plan:
            fill(half, slot, ok, w_first, row2)

        for half, slot, ok, w_first, last in plan:
            def window(w, c2, half=half, slot=slot, ok=ok):
                fill(half, slot, ok, w, row)
                return c2

            lax.fori_loop(w_first + 2, (last + GWIN - 1) // GWIN, window, 0)
        return carry

    lax.fori_loop(0, pn_ref[d] // 2, body, 0)
    o_ref[...] = (acc_ref[0] + acc_ref[1]).astype(BF16)
    gs_ref[...] = gacc_ref[0] + gacc_ref[1]


def _gather(h, pos_t, cw_t, blk_e, pbase, pcount, r0, before, after, pair_c, pair_e, pair_ok, ptotal, nblk):
    hbm = pl.BlockSpec(memory_space=pl.ANY)
    return pl.pallas_call(
        _gather_kernel,
        out_shape=(jax.ShapeDtypeStruct((nblk * BLK, D), BF16), jax.ShapeDtypeStruct((nblk * BLK, 1), F32)),
        grid_spec=pltpu.PrefetchScalarGridSpec(
            num_scalar_prefetch=10,
            grid=(nblk,),
            in_specs=[hbm, hbm, hbm],
            out_specs=(pl.BlockSpec((BLK, D), lambda d, *_: (d, 0)), pl.BlockSpec((BLK, 1), lambda d, *_: (d, 0))),
            scratch_shapes=[pltpu.VMEM((GSLOTS, CHUNK, D), BF16), pltpu.VMEM((GSLOTS, 1, CHUNK), I32),
                            pltpu.VMEM((GSLOTS, 1, CHUNK), F32), pltpu.VMEM((2, BLK, D), F32),
                            pltpu.VMEM((2, BLK, 1), F32), pltpu.SemaphoreType.DMA((3, GSLOTS))]),
        compiler_params=_params(("arbitrary",)),
        name="moe_gather",
    )(blk_e, pbase, pcount, r0, before, after, pair_c, pair_e, pair_ok, ptotal, h, pos_t, cw_t)


def _expert_up_kernel(be_ref, nv_ref, x_ref, wg_ref, wu_ref, o_ref, wgbf_ref, wubf_ref):
    d = pl.program_id(1)
    changed = (d == 0) | (be_ref[d] != be_ref[jnp.maximum(d - 1, 0)])

    @pl.when(changed)
    def _():
        wgbf_ref[...] = wg_ref[0].astype(BF16)
        wubf_ref[...] = wu_ref[0].astype(BF16)

    def up(rows):
        x = x_ref[rows, :]
        a = jnp.dot(x, wgbf_ref[...], preferred_element_type=F32)
        b = jnp.dot(x, wubf_ref[...], preferred_element_type=F32)
        o_ref[rows, :] = (a * jax.nn.sigmoid(a) * b).astype(BF16)

    half = BLK // 2
    nrows = nv_ref[d]

    @pl.when(nrows > half)
    def _():
        up(slice(0, BLK))

    @pl.when((nrows > 0) & (nrows <= half))
    def _():
        up(slice(0, half))
        o_ref[half:, :] = jnp.zeros((BLK - half, o_ref.shape[1]), BF16)

    @pl.when(nrows == 0)
    def _():
        o_ref[...] = jnp.zeros_like(o_ref)


def _expert_up(xs, blk_e, nvalid, wg, wu, tf):
    p = xs.shape[0]
    ff = wg.shape[2]
    return pl.pallas_call(
        _expert_up_kernel,
        out_shape=jax.ShapeDtypeStruct((p, ff), BF16),
        grid_spec=pltpu.PrefetchScalarGridSpec(
            num_scalar_prefetch=2,
            grid=(ff // tf, p // BLK),
            in_specs=[pl.BlockSpec((BLK, D), lambda f, d, be, nv: (d, 0)),
                      pl.BlockSpec((1, D, tf), lambda f, d, be, nv: (be[d], 0, f)),
                      pl.BlockSpec((1, D, tf), lambda f, d, be, nv: (be[d], 0, f))],
            out_specs=pl.BlockSpec((BLK, tf), lambda f, d, be, nv: (d, f)),
            scratch_shapes=[pltpu.VMEM((D, tf), BF16), pltpu.VMEM((D, tf), BF16)]),
        compiler_params=_params(("arbitrary", "arbitrary"), V7X_VMEM_LIMIT),
        name="moe_up",
    )(blk_e, nvalid, xs, wg, wu)


def _expert_down_kernel(be_ref, nv_ref, h_ref, gs_ref, wd_ref, o_ref, wdbf_ref):
    d = pl.program_id(1)
    changed = (d == 0) | (be_ref[d] != be_ref[jnp.maximum(d - 1, 0)])

    @pl.when(changed)
    def _():
        wdbf_ref[...] = wd_ref[0].astype(BF16)

    def down(rows):
        y = jnp.dot(h_ref[rows, :], wdbf_ref[...], preferred_element_type=F32)
        o_ref[rows, :] = (y * gs_ref[rows, :]).astype(BF16)

    half = BLK // 2
    nrows = nv_ref[d]

    @pl.when(nrows > half)
    def _():
        down(slice(0, BLK))

    @pl.when((nrows > 0) & (nrows <= half))
    def _():
        down(slice(0, half))
        o_ref[half:, :] = jnp.zeros((BLK - half, o_ref.shape[1]), BF16)

    @pl.when(nrows == 0)
    def _():
        o_ref[...] = jnp.zeros_like(o_ref)


def _expert_down(hh, gsort, blk_e, nvalid, wd, tn):
    p, ff = hh.shape
    return pl.pallas_call(
        _expert_down_kernel,
        out_shape=jax.ShapeDtypeStruct((p, D), BF16),
        grid_spec=pltpu.PrefetchScalarGridSpec(
            num_scalar_prefetch=2,
            grid=(D // tn, p // BLK),
            in_specs=[pl.BlockSpec((BLK, ff), lambda n, d, be, nv: (d, 0)),
                      pl.BlockSpec((BLK, 1), lambda n, d, be, nv: (d, 0)),
                      pl.BlockSpec((1, ff, tn), lambda n, d, be, nv: (be[d], 0, n))],
            out_specs=pl.BlockSpec((BLK, tn), lambda n, d, be, nv: (d, n)),
            scratch_shapes=[pltpu.VMEM((ff, tn), BF16)]),
        compiler_params=_params(("arbitrary", "arbitrary"), V7X_VMEM_LIMIT),
        name="moe_down",
    )(blk_e, nvalid, hh, gsort, wd)


def _combine_kernel(rs_ref, rc_ref, xp_ref, xs_ref, pos_ref, g_ref, y_hbm, op_ref, os_ref, wbuf, obuf, acc_ref,
                    sem, osem, *, n_prompt_chunks):
    i = pl.program_id(0)
    nch = pl.num_programs(0)

    def window_start(chunk, e):
        return pl.multiple_of(rs_ref[chunk * N_EXP + e] // BF16_ROWS * BF16_ROWS, BF16_ROWS)

    def copy(chunk, e, slot):
        return pltpu.make_async_copy(y_hbm.at[pl.ds(window_start(chunk, e), CWIN)],
                                     wbuf.at[slot, pl.ds(e * CWIN, CWIN)], sem.at[slot, e])

    @pl.when(i == 0)
    def _():
        for e in range(N_EXP):
            copy(0, e, 0).start()

    slot = i % 2

    @pl.when(i + 1 < nch)
    def _():
        for e in range(N_EXP):
            copy(i + 1, e, 1 - slot).start()

    col = lax.broadcasted_iota(I32, (CHUNK, CWIN), 1)
    hits = []
    for e in range(N_EXP):
        copy(i, e, slot).wait()
        hits.append((pos_ref[:, e:e + 1] - window_start(i, e)) == col)
    onehot = jnp.concatenate(hits, axis=1).astype(BF16)
    acc_ref[...] = jnp.dot(onehot, wbuf[slot], preferred_element_type=F32)

    for e in range(N_EXP):
        for k in range(1, (CHUNK + BF16_ROWS - 1) // CWIN + 1):
            tail = window_start(i, e) + k * CWIN

            @pl.when(rs_ref[i * N_EXP + e] + rc_ref[i * N_EXP + e] > tail)
            def _():
                cp = pltpu.make_async_copy(y_hbm.at[pl.ds(tail, CWIN)], obuf, osem)
                cp.start()
                cp.wait()
                hit = ((pos_ref[:, e:e + 1] - tail) == col).astype(BF16)
                acc_ref[...] += jnp.dot(hit, obuf[...], preferred_element_type=F32)

    x = jnp.where(i < n_prompt_chunks, xp_ref[...], xs_ref[...])
    out = x + _rms(acc_ref[...], g_ref[...])

    @pl.when(i < n_prompt_chunks)
    def _():
        op_ref[...] = out

    @pl.when(i >= n_prompt_chunks)
    def _():
        os_ref[...] = out


def _combine(run_start, run_count, xp, xs, pos, g, ys):
    npc = xp.shape[0] // CHUNK
    nsc = xs.shape[0] // CHUNK
    nch = npc + nsc
    tok8 = pl.BlockSpec((CHUNK, N_EXP), lambda i, *_: (i, 0))
    pspec = pl.BlockSpec((CHUNK, D), lambda i, *_: (jnp.minimum(i, npc - 1), 0))
    sspec = pl.BlockSpec((CHUNK, D), lambda i, *_: (jnp.maximum(i - npc, 0), 0))
    return pl.pallas_call(
        functools.partial(_combine_kernel, n_prompt_chunks=npc),
        out_shape=(jax.ShapeDtypeStruct(xp.shape, F32), jax.ShapeDtypeStruct(xs.shape, F32)),
        grid_spec=pltpu.PrefetchScalarGridSpec(
            num_scalar_prefetch=2,
            grid=(nch,),
            in_specs=[pspec, sspec, tok8, pl.BlockSpec((1, D), lambda i, *_: (0, 0)),
                      pl.BlockSpec(memory_space=pl.ANY)],
            out_specs=(pspec, sspec),
            scratch_shapes=[pltpu.VMEM((2, N_EXP * CWIN, D), BF16), pltpu.VMEM((CWIN, D), BF16),
                            pltpu.VMEM((CHUNK, D), F32),
                            pltpu.SemaphoreType.DMA((2, N_EXP)), pltpu.SemaphoreType.DMA]),
        compiler_params=_params(("arbitrary",), V7X_VMEM_LIMIT),
        name="moe_combine",
    )(run_start, run_count, xp, xs, pos, g, ys)


def _moe(xp, xs, g4, g5, w_router, wg, wu, wd):
    npc = xp.shape[0] // CHUNK
    nch = npc + xs.shape[0] // CHUNK
    t = nch * CHUNK
    h, rk, rk_t, cw_t, cnt = _route(xp, xs, g4, w_router.T)

    after = cnt.reshape(nch, N_EXP).astype(I32)
    before = jnp.concatenate([jnp.zeros((1, N_EXP), I32), after[:-1]], axis=0)
    counts = after[-1]
    gsz = (counts + BLK - 1) // BLK * BLK
    gend = jnp.cumsum(gsz)
    gstart = gend - gsz
    nblk = (2 * t + N_EXP * (BLK - 1)) // BLK + 1 + (2 * CWIN + BLK - 1) // BLK
    bstart = jnp.arange(nblk, dtype=I32) * BLK
    blk_e = jnp.minimum(jnp.sum(gend[None, :] <= bstart[:, None], axis=1), N_EXP - 1).astype(I32)
    valid = bstart < gend[-1]

    def pick(table, idx, n):
        hit = idx[:, None] == jnp.arange(n, dtype=I32)[None, :]
        return jnp.sum(jnp.where(hit, table[..., None, :], 0), axis=-1)

    r0 = bstart - pick(gstart, blk_e, N_EXP)
    aft_e = pick(after, blk_e, N_EXP)
    bef_e = pick(before, blk_e, N_EXP)
    c_lo = jnp.where(valid, jnp.sum(aft_e <= r0[None, :], axis=0), 0).astype(I32)
    c_hi = jnp.where(valid, jnp.sum(bef_e < (r0 + BLK)[None, :], axis=0) - 1, -1).astype(I32)
    c_lo = jnp.minimum(c_lo, nch - 1)
    pos = jnp.where(rk >= 0, rk + gstart[None, :], -1).astype(I32)
    pos_t = jnp.where(rk_t >= 0, rk_t + gstart[:, None], -1).astype(I32).reshape(N_EXP, nch, 1, CHUNK)
    cw_t = cw_t.reshape(N_EXP, nch, 1, CHUNK)
    run_start = (gstart[None, :] + before).astype(I32).reshape(-1)
    run_count = (after - before).reshape(-1)
    nvalid = jnp.where(valid, jnp.clip(pick(counts, blk_e, N_EXP) - r0, 0, BLK), 0).astype(I32)
    nsrc = jnp.maximum(c_hi - c_lo + 1, 0)
    pcount = nsrc + (nsrc & 1)
    pend = jnp.cumsum(pcount)
    pbase = pend - pcount
    k = jnp.arange(2 * nblk + 2 * nch * N_EXP, dtype=I32)
    pair_blk = jnp.minimum(jnp.sum(pend[None, :] <= k[:, None], axis=1), nblk - 1)
    per_blk = pick(jnp.stack([pbase, nsrc, c_lo, blk_e]), pair_blk, nblk)
    k_local = k - per_blk[0]
    pair_ok = (k_local < per_blk[1]).astype(I32)
    pair_c = jnp.clip(per_blk[2] + jnp.minimum(k_local, per_blk[1] - 1), 0, nch - 1).astype(I32)
    pair_e = per_blk[3].astype(I32)

    xsort, gsort = _gather(h, pos_t, cw_t, blk_e, pbase.astype(I32), pcount.astype(I32), r0.astype(I32),
                           before.reshape(-1), after.reshape(-1), pair_c, pair_e, pair_ok,
                           pend[-1:].astype(I32), nblk)
    hh = _expert_up(xsort, blk_e, nvalid, wg, wu, tf=1792)
    ysort = _expert_down(hh, gsort, blk_e, nvalid, wd, tn=D)
    return _combine(run_start, run_count, xp, xs, pos, g5, ysort)


def kernel(x_prompt, x_sample, cache_conv, cache_mem_k, cache_mem_v, state_ssm_re, state_ssm_im, mem_prompt, norm_g, mem_norm_g, w_xq, w_xk, w_xv, w_xo, conv_w_pw1, conv_b_pw1, conv_w_dw, conv_b_dw, conv_ln_g, conv_ln_b, conv_w_pw2, conv_b_pw2, ssm_a_re, ssm_a_im, ssm_log_dt, ssm_b_re, ssm_b_im, ssm_c_re, ssm_c_im, ssm_d, ssm_w_glu, ssm_b_glu, ffn_w_gate, ffn_w_up, ffn_w_down, moe_w_router, moe_w_gate, moe_w_up, moe_w_down):
    nbp, seqp, _ = x_prompt.shape
    nbs, seqs, _ = x_sample.shape
    tp = nbp * seqp
    ts = nbs * seqs
    row = lambda a: a.reshape(1, -1)
    g = lambda i, k: norm_g[i, k].reshape(1, D)

    nl = w_xk.shape[0]
    pk, pv = _mem_kv(mem_prompt, mem_norm_g.reshape(nl, 1, D), w_xk, w_xv)
    p_mem_k = pk.reshape(nl, nbp, N_MEM, N_HEADS, HEAD_DIM)
    p_mem_v = pv.reshape(nl, nbp, N_MEM, N_HEADS, HEAD_DIM)
    ck = cache_mem_k.reshape(nl * nbs, N_MEM, N_HEADS, HEAD_DIM)
    cv = cache_mem_v.reshape(nl * nbs, N_MEM, N_HEADS, HEAD_DIM)

    conv_args = (conv_w_dw[0], row(conv_b_dw[0]), row(conv_ln_g[0]), row(conv_ln_b[0]),
                 conv_w_pw2[0], row(conv_b_pw2[0]), g(0, 1))
    xp = x_prompt.reshape(tp, D)
    xs = x_sample.reshape(ts, D)
    up = _conv_pw1(xp, g(0, 0), conv_w_pw1[0], row(conv_b_pw1[0]), tm=1024).reshape(nbp, seqp, D)
    us = _conv_pw1(xs, g(0, 0), conv_w_pw1[0], row(conv_b_pw1[0]), tm=ts).reshape(nbs, seqs, D)
    hist_p = jnp.zeros((nbp, HIST, D), F32)
    hist_s = jnp.pad(cache_conv[0], ((0, 0), (HIST - CONV_W + 1, 0), (0, 0)))
    xp = _conv_dw_pw2(up, x_prompt, hist_p, *conv_args, tl=512).reshape(tp, D)
    xs = _conv_dw_pw2(us, x_sample, hist_s, *conv_args, tl=seqs).reshape(ts, D)
    p_conv = up[:, seqp - (CONV_W - 1):][None]
    s_conv = jnp.concatenate([cache_conv[0], us], axis=1)[:, -(CONV_W - 1):][None]

    tma = 1024
    xp = _attn(xp, pk, pv, 0, w_xq[0], w_xo[0], g(0, 2), g(0, 3), nbp, seqp, tma)
    xs = _attn(xs, ck, cv, 0, w_xq[0], w_xo[0], g(0, 2), g(0, 3), nbs, seqs, seqs)

    ffn_w = (_to_bf16(ffn_w_gate[0], 4), _to_bf16(ffn_w_up[0], 4), _to_bf16(ffn_w_down[0], 4))
    tff = ffn_w_gate.shape[2] // 2
    xp = _ffn(xp, g(0, 4), g(0, 5), *ffn_w, tm=1024, tf=tff, out_shape=(tp, D), out_index=lambda i, f: (i, 0))
    xs = _ffn(xs, g(0, 4), g(0, 5), *ffn_w, tm=ts, tf=tff, out_shape=(ts, D), out_index=lambda i, f: (i, 0))

    (ab_r, ab_i, a2_r, a2_i), (bb_r, bb_i, abb_r, abb_i) = _ssm_prep(
        ssm_a_re[0], ssm_a_im[0], ssm_log_dt[0], ssm_b_re[0], ssm_b_im[0])
    bdiag = lambda m: _block_diag(m.transpose(0, 2, 1)).astype(BF16)
    bm_r, bm_i = bdiag(bb_r), bdiag(bb_i)
    cm_r, cm_i = bdiag(ssm_c_re[0]), bdiag(-ssm_c_im[0])
    tail_args = (cm_r, cm_i, row(ssm_d[0]), _to_bf16(ssm_w_glu[0], 4), row(ssm_b_glu[0]))
    half = SUBLANES // 2
    hdiag = lambda m: _block_diag(m.transpose(0, 2, 1), gpb=SSM_CB // SSM_C // 2).astype(BF16)
    pair_args = (jnp.concatenate([jnp.tile(ab_r, (half, 1)), jnp.tile(a2_r, (half, 1))]),
                 jnp.concatenate([jnp.tile(ab_i, (half, 1)), jnp.tile(a2_i, (half, 1))]),
                 jnp.concatenate([hdiag(bb_r), hdiag(abb_r)], axis=1),
                 jnp.concatenate([hdiag(bb_i), hdiag(abb_i)], axis=1))
    zero_state = jnp.zeros((SUBLANES, SSM_N), F32)
    xp, p_sr, p_si = _ssm(xp.reshape(nbp, seqp, D), g(1, 0), g(1, 1), zero_state, zero_state,
                          *pair_args, *tail_args, tl=64)
    xs, s_sr, s_si = _ssm(xs.reshape(nbs, seqs, D), g(1, 0), g(1, 1), state_ssm_re[0].reshape(nbs, SSM_N),
                          state_ssm_im[0].reshape(nbs, SSM_N), jnp.tile(ab_r, (nbs, 1)), jnp.tile(ab_i, (nbs, 1)),
                          bm_r, bm_i, *tail_args, tl=seqs)
    st = lambda a, n: a[:n].reshape(1, n, SSM_G, SSM_P)
    p_ssm_re, p_ssm_im = st(p_sr, nbp), st(p_si, nbp)
    s_ssm_re, s_ssm_im = st(s_sr, nbs), st(s_si, nbs)

    xp = _attn(xp.reshape(tp, D), pk, pv, nbp, w_xq[1], w_xo[1], g(1, 2), g(1, 3), nbp, seqp, tma)
    xs = _attn(xs.reshape(ts, D), ck, cv, nbs, w_xq[1], w_xo[1], g(1, 2), g(1, 3), nbs, seqs, seqs)

    yp, ysm = _moe(xp, xs, g(1, 4), g(1, 5), moe_w_router[0], moe_w_gate[0], moe_w_up[0], moe_w_down[0])
    return (yp.reshape(nbp, seqp, D), ysm.reshape(nbs, seqs, D), p_conv, p_mem_k, p_mem_v,
            p_ssm_re, p_ssm_im, s_conv, s_ssm_re, s_ssm_im)
```

```python
import functools

import jax
import jax.numpy as jnp
from jax import lax
from jax.experimental import pallas as pl
from jax.experimental.pallas import tpu as pltpu

F32 = jnp.float32
BF16 = jnp.bfloat16
I32 = jnp.int32

D = 1024
CONV_W = 31
HIST = 32
N_MEM = 256
N_HEADS = 4
HEAD_DIM = D // N_HEADS
SSM_G = 64
SSM_C = 16
SSM_P = 64
SSM_N = SSM_G * SSM_P
SSM_CB = 256
SSM_NCB = D // SSM_CB
SSM_SB = SSM_CB // SSM_C * SSM_P
N_EXP = 8
EPS = 1e-6

V7X_VMEM_LIMIT = 56 * 1024 * 1024
SUBLANES = 8

CHUNK = 256
BLK = 512
GWIN = 128
GSLOTS = 8
CWIN = 256
BF16_ROWS = 16


def _params(sem, vmem=None):
    return pltpu.CompilerParams(dimension_semantics=sem, vmem_limit_bytes=vmem)


def _rms(x, g):
    return x * lax.rsqrt(jnp.mean(x * x, axis=-1, keepdims=True) + EPS) * g


def _const_spec(shape):
    nd = len(shape)
    return pl.BlockSpec(shape, lambda *_: (0,) * nd)


def _pw1_kernel(x_ref, g_ref, w_ref, b_ref, u_ref, wbf_ref):
    @pl.when(pl.program_id(0) == 0)
    def _():
        wbf_ref[...] = w_ref[...].astype(BF16)

    h = _rms(x_ref[...], g_ref[...]).astype(BF16)
    z = jnp.dot(h, wbf_ref[...], preferred_element_type=F32) + b_ref[...]
    u_ref[...] = z[:, :D] * jax.nn.sigmoid(z[:, D:])


def _conv_pw1(x, g, w, b, tm):
    t = x.shape[0]
    return pl.pallas_call(
        _pw1_kernel,
        out_shape=jax.ShapeDtypeStruct((t, D), F32),
        grid=(t // tm,),
        in_specs=[pl.BlockSpec((tm, D), lambda i: (i, 0)),
                  _const_spec((1, D)), _const_spec((D, 2 * D)), _const_spec((1, 2 * D))],
        out_specs=pl.BlockSpec((tm, D), lambda i: (i, 0)),
        scratch_shapes=[pltpu.VMEM((D, 2 * D), BF16)],
        compiler_params=_params(("arbitrary",), V7X_VMEM_LIMIT),
        name="conv_pw1",
    )(x, g, w, b)


def _conv2_kernel(u_ref, x_ref, hist_ref, wdw_ref, bdw_ref, lng_ref, lnb_ref, w2_ref, b2_ref,
                  g_ref, o_ref, ext_ref, sh_ref, conv_ref, w2bf_ref, wtap_ref, *, tl, rt):
    bi = pl.program_id(0)
    li = pl.program_id(1)

    @pl.when((bi == 0) & (li == 0))
    def _():
        w2bf_ref[...] = w2_ref[...].astype(BF16)
        for k in range(CONV_W):
            wtap_ref[k] = jnp.broadcast_to(wdw_ref[k:k + 1, :], (SUBLANES, D))

    @pl.when(li == 0)
    def _():
        ext_ref[0:HIST, :] = hist_ref[0]

    @pl.when(li > 0)
    def _():
        ext_ref[0:HIST, :] = ext_ref[tl:tl + HIST, :]

    ext_ref[HIST:HIST + tl, :] = u_ref[0]
    for s in range(1, SUBLANES):
        sh_ref[s - 1] = ext_ref[pl.ds(s, tl + HIST - SUBLANES), :]

    def rows(i, carry):
        r0 = pl.multiple_of(i * rt, rt)
        ntile = rt // SUBLANES
        accs = [jnp.zeros((SUBLANES, D), F32) + bdw_ref[...] for _ in range(ntile)]
        for k in range(CONV_W):
            off = HIST - CONV_W + 1 + k
            s = off % SUBLANES
            base = off - s
            w = wtap_ref[k]
            for ti in range(ntile):
                rows_ti = pl.ds(r0 + base + ti * SUBLANES, SUBLANES)
                src = ext_ref[rows_ti, :] if s == 0 else sh_ref[s - 1, rows_ti, :]
                accs[ti] = accs[ti] + w * src
        for ti in range(ntile):
            conv_ref[pl.ds(r0 + ti * SUBLANES, SUBLANES), :] = accs[ti]
        return carry

    lax.fori_loop(0, tl // rt, rows, 0)
    acc = conv_ref[...]
    mu = jnp.mean(acc, axis=-1, keepdims=True)
    xc = acc - mu
    var = jnp.mean(xc * xc, axis=-1, keepdims=True)
    y = xc * lax.rsqrt(var + EPS) * lng_ref[...] + lnb_ref[...]
    y = y * jax.nn.sigmoid(y)
    t = jnp.dot(y.astype(BF16), w2bf_ref[...], preferred_element_type=F32) + b2_ref[...]
    o_ref[0] = x_ref[0] + _rms(t, g_ref[...])


def _conv_dw_pw2(u, x, hist, wdw, bdw, lng, lnb, w2, b2, g, tl):
    nb, seq, _ = u.shape
    tok = pl.BlockSpec((1, tl, D), lambda b, l: (b, l, 0))
    return pl.pallas_call(
        functools.partial(_conv2_kernel, tl=tl, rt=min(tl, 32)),
        out_shape=jax.ShapeDtypeStruct((nb, seq, D), F32),
        grid=(nb, seq // tl),
        in_specs=[tok, tok, pl.BlockSpec((1, HIST, D), lambda b, l: (b, 0, 0)),
                  _const_spec((CONV_W, D)), _const_spec((1, D)), _const_spec((1, D)), _const_spec((1, D)),
                  _const_spec((D, D)), _const_spec((1, D)), _const_spec((1, D))],
        out_specs=tok,
        scratch_shapes=[pltpu.VMEM((tl + HIST, D), F32),
                        pltpu.VMEM((SUBLANES - 1, tl + HIST - SUBLANES, D), F32),
                        pltpu.VMEM((tl, D), F32), pltpu.VMEM((D, D), BF16),
                        pltpu.VMEM((CONV_W, SUBLANES, D), F32)],
        compiler_params=_params(("arbitrary", "arbitrary"), V7X_VMEM_LIMIT),
        name="conv_dw_pw2",
    )(u, x, hist, wdw, bdw, lng, lnb, w2, b2, g)


def _memkv_kernel(m_ref, g_ref, wk_ref, wv_ref, k_ref, v_ref, wkbf_ref, wvbf_ref):
    @pl.when(pl.program_id(1) == 0)
    def _():
        wkbf_ref[...] = wk_ref[0].astype(BF16)
        wvbf_ref[...] = wv_ref[0].astype(BF16)

    m = _rms(m_ref[0], g_ref[0]).astype(BF16)
    k = jnp.dot(m, wkbf_ref[...], preferred_element_type=F32)
    v = jnp.dot(m, wvbf_ref[...], preferred_element_type=F32)
    for hd in range(N_HEADS):
        sl = slice(hd * HEAD_DIM, (hd + 1) * HEAD_DIM)
        k_ref[0, :, hd, :] = k[:, sl]
        v_ref[0, :, hd, :] = v[:, sl]


def _mem_kv(mem, g, wk, wv):
    nb = mem.shape[0]
    nl = wk.shape[0]
    kv = jax.ShapeDtypeStruct((nl * nb, N_MEM, N_HEADS, HEAD_DIM), F32)
    out = pl.BlockSpec((1, N_MEM, N_HEADS, HEAD_DIM), lambda l, b: (l * nb + b, 0, 0, 0))
    w = pl.BlockSpec((1, D, D), lambda l, b: (l, 0, 0))
    return pl.pallas_call(
        _memkv_kernel,
        out_shape=(kv, kv),
        grid=(nl, nb),
        in_specs=[pl.BlockSpec((1, N_MEM, D), lambda l, b: (b, 0, 0)),
                  pl.BlockSpec((1, 1, D), lambda l, b: (l, 0, 0)), w, w],
        out_specs=(out, out),
        scratch_shapes=[pltpu.VMEM((D, D), BF16), pltpu.VMEM((D, D), BF16)],
        compiler_params=_params(("arbitrary", "arbitrary"), V7X_VMEM_LIMIT),
        name="mem_kv",
    )(mem, g, wk, wv)


def _attn_kernel(x_ref, k_ref, v_ref, wq_ref, wo_ref, g2_ref, g3_ref, o_ref, wqbf_ref, wobf_ref, kb_ref, vb_ref):
    @pl.when((pl.program_id(0) == 0) & (pl.program_id(1) == 0))
    def _():
        wqbf_ref[...] = wq_ref[...].astype(BF16)
        wobf_ref[...] = wo_ref[...].astype(BF16)

    @pl.when(pl.program_id(1) == 0)
    def _():
        for hd in range(N_HEADS):
            sl = slice(hd * HEAD_DIM, (hd + 1) * HEAD_DIM)
            kb_ref[:, sl] = k_ref[0, :, hd, :].astype(BF16)
            vb_ref[:, sl] = v_ref[0, :, hd, :].astype(BF16)

    x = x_ref[...]
    h = _rms(x, g2_ref[...]).astype(BF16)
    q = (jnp.dot(h, wqbf_ref[...], preferred_element_type=F32) * (HEAD_DIM ** -0.5)).astype(BF16)
    heads = []
    for hd in range(N_HEADS):
        sl = slice(hd * HEAD_DIM, (hd + 1) * HEAD_DIM)
        s = lax.dot_general(q[:, sl], kb_ref[:, sl], (((1,), (1,)), ((), ())), preferred_element_type=F32)
        p = jnp.exp(s - jnp.max(s, axis=-1, keepdims=True))
        p = p / jnp.sum(p, axis=-1, keepdims=True)
        heads.append(jnp.dot(p.astype(BF16), vb_ref[:, sl], preferred_element_type=F32))
    o = jnp.concatenate(heads, axis=1).astype(BF16)
    t = jnp.dot(o, wobf_ref[...], preferred_element_type=F32)
    o_ref[...] = x + _rms(t, g3_ref[...])


def _attn(x, k, v, kv_base, wq, wo, g2, g3, nb, seq, tm):
    tok = pl.BlockSpec((tm, D), lambda b, l: (b * (seq // tm) + l, 0))
    kv = pl.BlockSpec((1, N_MEM, N_HEADS, HEAD_DIM), lambda b, l: (kv_base + b, 0, 0, 0))
    return pl.pallas_call(
        _attn_kernel,
        out_shape=jax.ShapeDtypeStruct((nb * seq, D), F32),
        grid=(nb, seq // tm),
        in_specs=[tok, kv, kv, _const_spec((D, D)), _const_spec((D, D)), _const_spec((1, D)), _const_spec((1, D))],
        out_specs=tok,
        scratch_shapes=[pltpu.VMEM((D, D), BF16), pltpu.VMEM((D, D), BF16),
                        pltpu.VMEM((N_MEM, D), BF16), pltpu.VMEM((N_MEM, D), BF16)],
        compiler_params=_params(("arbitrary", "arbitrary"), V7X_VMEM_LIMIT),
        name="mem_attn",
    )(x, k, v, wq, wo, g2, g3)


def _cast_kernel(w_ref, o_ref):
    o_ref[...] = w_ref[...].astype(BF16)


def _to_bf16(w, nsplit):
    r, c = w.shape
    blk = pl.BlockSpec((r // nsplit, c), lambda i: (i, 0))
    return pl.pallas_call(
        _cast_kernel,
        out_shape=jax.ShapeDtypeStruct((r, c), BF16),
        grid=(nsplit,),
        in_specs=[blk],
        out_specs=blk,
        compiler_params=_params(("arbitrary",)),
        name="to_bf16",
    )(w)


def _ffn_kernel(x_ref, g4_ref, g5_ref, wg_ref, wu_ref, wd_ref, o_ref, h_ref, acc_ref):
    f = pl.program_id(1)

    @pl.when(f == 0)
    def _():
        h_ref[...] = _rms(x_ref[...], g4_ref[...]).astype(BF16)
        acc_ref[...] = jnp.zeros_like(acc_ref)

    h = h_ref[...]
    a = jnp.dot(h, wg_ref[...], preferred_element_type=F32)
    b = jnp.dot(h, wu_ref[...], preferred_element_type=F32)
    hh = (a * jax.nn.sigmoid(a) * b).astype(BF16)
    acc_ref[...] += jnp.dot(hh, wd_ref[...], preferred_element_type=F32)

    @pl.when(f == pl.num_programs(1) - 1)
    def _():
        o_ref[...] = x_ref[...] + _rms(acc_ref[...], g5_ref[...])


def _ffn(x, g4, g5, wg, wu, wd, tm, tf, out_shape, out_index):
    t = x.shape[0]
    ff = wg.shape[1]
    return pl.pallas_call(
        _ffn_kernel,
        out_shape=jax.ShapeDtypeStruct(out_shape, F32),
        grid=(t // tm, ff // tf),
        in_specs=[pl.BlockSpec((tm, D), lambda i, f: (i, 0)),
                  _const_spec((1, D)), _const_spec((1, D)),
                  pl.BlockSpec((D, tf), lambda i, f: (0, f)),
                  pl.BlockSpec((D, tf), lambda i, f: (0, f)),
                  pl.BlockSpec((tf, D), lambda i, f: (f, 0))],
        out_specs=pl.BlockSpec((tm, D), out_index),
        scratch_shapes=[pltpu.VMEM((tm, D), BF16), pltpu.VMEM((tm, D), F32)],
        compiler_params=_params(("arbitrary", "arbitrary"), V7X_VMEM_LIMIT),
        name="dense_ffn",
    )(x, g4, g5, wg, wu, wd)


def _ssm_prep_kernel(lr_ref, li_ref, ldt_ref, br_ref, bi_ref, abr_ref, abi_ref, bbr_ref, bbi_ref,
                     a2r_ref, a2i_ref, abbr_ref, abbi_ref):
    dt = jnp.exp(ldt_ref[...])
    lr = lr_ref[...]
    li = li_ref[...]
    mag = jnp.exp(lr * dt)
    ab_r = mag * jnp.cos(li * dt)
    ab_i = mag * jnp.sin(li * dt)
    den = lr * lr + li * li
    nr = ab_r - 1.0
    k_r = (nr * lr + ab_i * li) / den
    k_i = (ab_i * lr - nr * li) / den
    br = br_ref[...]
    bi = bi_ref[...]
    bb_r = k_r * br - k_i * bi
    bb_i = k_r * bi + k_i * br
    abr_ref[...] = ab_r
    abi_ref[...] = ab_i
    bbr_ref[...] = bb_r
    bbi_ref[...] = bb_i
    a2r_ref[...] = ab_r * ab_r - ab_i * ab_i
    a2i_ref[...] = 2.0 * (ab_r * ab_i)
    abbr_ref[...] = ab_r * bb_r - ab_i * bb_i
    abbi_ref[...] = ab_r * bb_i + ab_i * bb_r


def _ssm_prep(a_re, a_im, log_dt, b_re, b_im):
    n = SSM_P * SSM_C
    rep = lambda a: jnp.repeat(a, SSM_C, axis=1)
    shp = jax.ShapeDtypeStruct((SSM_G, n), F32)
    abr, abi, bbr, bbi, a2r, a2i, abbr, abbi = pl.pallas_call(
        _ssm_prep_kernel,
        out_shape=(shp,) * 8,
        name="ssm_prep",
    )(rep(a_re), rep(a_im), log_dt.reshape(SSM_G, 1), b_re.reshape(SSM_G, n), b_im.reshape(SSM_G, n))
    pick = lambda a: a.reshape(SSM_G, SSM_P, SSM_C)[:, :, 0].reshape(1, SSM_N)
    gpc = lambda a: a.reshape(SSM_G, SSM_P, SSM_C)
    return (pick(abr), pick(abi), pick(a2r), pick(a2i)), (gpc(bbr), gpc(bbi), gpc(abbr), gpc(abbi))


def _block_diag(m, gpb=SSM_CB // SSM_C):
    g, a, b = m.shape
    m = m.reshape(g // gpb, gpb, a, 1, b)
    eye = jnp.eye(gpb, dtype=m.dtype).reshape(1, gpb, 1, gpb, 1)
    return (m * eye).reshape(g // gpb, gpb * a, gpb * b)


def _ssm_kernel(x_ref, g0_ref, g1_ref, s0r_ref, s0i_ref, ar_ref, ai_ref, bmr_ref, bmi_ref, cmr_ref, cmi_ref,
                d_ref, wglu_ref, bglu_ref, o_ref, sr_ref, si_ref,
                bur_ref, bui_ref, *, nb, rb, lc):
    @pl.when(pl.program_id(0) == 0)
    def _():
        sr_ref[...] = s0r_ref[...]
        si_ref[...] = s0i_ref[...]

    tl = rb // nb
    x = x_ref[...].reshape(rb, D)
    h = _rms(x, g0_ref[...])
    r = lax.broadcasted_iota(I32, (rb, rb), 0)
    c = lax.broadcasted_iota(I32, (rb, rb), 1)
    to_time_major = (c == (r & (nb - 1)) * tl + (r >> (nb.bit_length() - 1))).astype(BF16)
    to_batch_major = (c == (r & (tl - 1)) * nb + (r >> (tl.bit_length() - 1))).astype(BF16)
    hb = jnp.dot(to_time_major, h.astype(BF16), preferred_element_type=F32).astype(BF16)
    rows = max(nb, SUBLANES)
    pair = nb < SUBLANES
    if pair:
        odd = ((r >> (nb.bit_length() - 1)) & 1) == 1
        prev_time_major = ((c == (r & (nb - 1)) * tl + (r >> (nb.bit_length() - 1)) - 1) & odd).astype(BF16)
        hprev = jnp.dot(prev_time_major, h.astype(BF16), preferred_element_type=F32).astype(BF16)
        low = lax.broadcasted_iota(I32, (SUBLANES, lc), 0) < nb

    def scan_chunk(cs):
        a_r = ar_ref[:, cs]
        a_i = ai_ref[:, cs]
        s_r = sr_ref[:, cs]
        s_i = si_ref[:, cs]
        for j in range(rb // rows):
            rs = slice(j * rows, (j + 1) * rows)
            n_r = a_r * s_r - a_i * s_i + bur_ref[rs, cs]
            n_i = a_r * s_i + a_i * s_r + bui_ref[rs, cs]
            bur_ref[rs, cs] = n_r
            bui_ref[rs, cs] = n_i
            if pair:
                s_r = jnp.where(low, pltpu.roll(n_r, nb, 0), n_r)
                s_i = jnp.where(low, pltpu.roll(n_i, nb, 0), n_i)
            else:
                s_r, s_i = n_r, n_i
        sr_ref[:, cs] = s_r
        si_ref[:, cs] = s_i

    ist = bmr_ref.shape[2]
    ich = bmr_ref.shape[1] // 2 if pair else bmr_ref.shape[1]
    ys = []
    for cb in range(SSM_NCB):
        ss = slice(cb * SSM_SB, (cb + 1) * SSM_SB)
        for ib in range(cb * (SSM_SB // ist), (cb + 1) * (SSM_SB // ist)):
            hs = hb[:, ib * ich:(ib + 1) * ich]
            if pair:
                hs = jnp.concatenate([hs, hprev[:, ib * ich:(ib + 1) * ich]], axis=1)
            bur_ref[:, ib * ist:(ib + 1) * ist] = jnp.dot(hs, bmr_ref[ib], preferred_element_type=F32)
            bui_ref[:, ib * ist:(ib + 1) * ist] = jnp.dot(hs, bmi_ref[ib], preferred_element_type=F32)
            for c in range(ist // lc):
                scan_chunk(slice(ib * ist + c * lc, ib * ist + (c + 1) * lc))
        ys.append(jnp.dot(bur_ref[:, ss].astype(BF16), cmr_ref[cb], preferred_element_type=F32)
                  + jnp.dot(bui_ref[:, ss].astype(BF16), cmi_ref[cb], preferred_element_type=F32))
    y_tm = jnp.concatenate(ys, axis=1)
    y1 = y_tm.astype(BF16)
    rem = y_tm - y1.astype(F32)
    y2 = rem.astype(BF16)
    y3 = (rem - y2.astype(F32)).astype(BF16)
    y = (jnp.dot(to_batch_major, y1, preferred_element_type=F32)
         + jnp.dot(to_batch_major, y2, preferred_element_type=F32)
         + jnp.dot(to_batch_major, y3, preferred_element_type=F32))
    y = y + d_ref[...] * h
    y = jax.nn.gelu(y).astype(BF16)
    z = jnp.dot(y, wglu_ref[...], preferred_element_type=F32) + bglu_ref[...]
    t = z[:, :D] * jax.nn.sigmoid(z[:, D:])
    o_ref[...] = (x + _rms(t, g1_ref[...])).reshape(nb, tl, D)


def _ssm(x, g0, g1, s0r, s0i, ab_r, ab_i, bm_r, bm_i, cm_r, cm_i, d, wglu, bglu, tl):
    nb, seq, _ = x.shape
    rb = nb * tl
    rows = max(nb, SUBLANES)
    st = jax.ShapeDtypeStruct((rows, SSM_N), F32)
    row = pl.BlockSpec((nb, tl, D), lambda i: (0, i, 0))
    return pl.pallas_call(
        functools.partial(_ssm_kernel, nb=nb, rb=rb, lc=512),
        out_shape=(jax.ShapeDtypeStruct((nb, seq, D), F32), st, st),
        grid=(seq // tl,),
        in_specs=[row, _const_spec((1, D)), _const_spec((1, D)),
                  _const_spec((rows, SSM_N)), _const_spec((rows, SSM_N)),
                  _const_spec((rows, SSM_N)), _const_spec((rows, SSM_N)),
                  _const_spec(bm_r.shape), _const_spec(bm_i.shape),
                  _const_spec((SSM_NCB, SSM_SB, SSM_CB)), _const_spec((SSM_NCB, SSM_SB, SSM_CB)),
                  _const_spec((1, D)), _const_spec((D, 2 * D)), _const_spec((1, 2 * D))],
        out_specs=(row, _const_spec((rows, SSM_N)), _const_spec((rows, SSM_N))),
        scratch_shapes=[pltpu.VMEM((rb, SSM_N), F32), pltpu.VMEM((rb, SSM_N), F32)],
        compiler_params=_params(("arbitrary",), V7X_VMEM_LIMIT),
        name="ssm",
    )(x, g0, g1, s0r, s0i, ab_r, ab_i, bm_r, bm_i, cm_r, cm_i, d, wglu, bglu)


def _route_kernel(xp_ref, xs_ref, g_ref, wrt_ref, h_ref, rk_ref, rkt_ref, cwt_ref, cnt_ref, carry_ref,
                  *, n_prompt_chunks):
    i = pl.program_id(0)

    @pl.when(i == 0)
    def _():
        carry_ref[...] = jnp.zeros_like(carry_ref)

    x = jnp.where(i < n_prompt_chunks, xp_ref[...], xs_ref[...])
    h = _rms(x, g_ref[...])
    h_ref[...] = h.astype(BF16)
    lg = lax.dot_general(wrt_ref[...], h, (((1,), (1,)), ((), ())), preferred_element_type=F32,
                         precision=lax.Precision.HIGHEST)
    ex = lax.broadcasted_iota(I32, lg.shape, 0)
    m1 = jnp.max(lg, axis=0, keepdims=True)
    i1 = jnp.min(jnp.where(lg == m1, ex, N_EXP), axis=0, keepdims=True)
    first = ex == i1
    lg2 = jnp.where(first, -jnp.inf, lg)
    m2 = jnp.max(lg2, axis=0, keepdims=True)
    i2 = jnp.min(jnp.where(lg2 == m2, ex, N_EXP), axis=0, keepdims=True)
    second = ex == i2
    e = jnp.exp(m2 - m1)
    den = 1.0 + e
    cwt_ref[...] = jnp.where(first, 1.0 / den, 0.0) + jnp.where(second, e / den, 0.0)
    assigned = first | second
    r = lax.broadcasted_iota(I32, (CHUNK, CHUNK), 0)
    c = lax.broadcasted_iota(I32, (CHUNK, CHUNK), 1)
    earlier = (r < c).astype(BF16)
    rank = jnp.dot(assigned.astype(BF16), earlier, preferred_element_type=F32) + carry_ref[...]
    rank = jnp.where(assigned, rank, -1.0)
    rkt_ref[...] = rank.astype(I32)
    rk_ref[...] = rank.T.astype(I32)
    carry_ref[...] += jnp.sum(assigned.astype(F32), axis=1, keepdims=True)
    cnt_ref[0] = carry_ref[...]


def _route(xp, xs, g, wrt):
    npc = xp.shape[0] // CHUNK
    nch = npc + xs.shape[0] // CHUNK
    t = nch * CHUNK
    exp_major = pl.BlockSpec((N_EXP, CHUNK), lambda i: (0, i))
    return pl.pallas_call(
        functools.partial(_route_kernel, n_prompt_chunks=npc),
        out_shape=(jax.ShapeDtypeStruct((t, D), BF16), jax.ShapeDtypeStruct((t, N_EXP), I32),
                   jax.ShapeDtypeStruct((N_EXP, t), I32), jax.ShapeDtypeStruct((N_EXP, t), F32),
                   jax.ShapeDtypeStruct((nch, N_EXP, 1), F32)),
        grid=(nch,),
        in_specs=[pl.BlockSpec((CHUNK, D), lambda i: (jnp.minimum(i, npc - 1), 0)),
                  pl.BlockSpec((CHUNK, D), lambda i: (jnp.maximum(i - npc, 0), 0)),
                  _const_spec((1, D)), _const_spec((N_EXP, D))],
        out_specs=(pl.BlockSpec((CHUNK, D), lambda i: (i, 0)),
                   pl.BlockSpec((CHUNK, N_EXP), lambda i: (i, 0)), exp_major, exp_major,
                   pl.BlockSpec((1, N_EXP, 1), lambda i: (i, 0, 0))),
        scratch_shapes=[pltpu.VMEM((N_EXP, 1), F32)],
        compiler_params=_params(("arbitrary",)),
        name="moe_route",
    )(xp, xs, g, wrt)


def _gather_kernel(be_ref, pb_ref, pn_ref, r0_ref, bef_ref, aft_ref, pc_ref, pe_ref, pk_ref, pt_ref,
                   h_hbm, pos_hbm, cw_hbm, o_ref, gs_ref, hbuf, pbuf, cbuf, acc_ref, gacc_ref, sem):
    d = pl.program_id(0)
    e = be_ref[d]
    base = pb_ref[d]
    r0 = r0_ref[d]
    total = pt_ref[0]
    ahead = GSLOTS - 2

    def copies(idx):
        slot = idx % GSLOTS
        c = pc_ref[idx]
        ee = pe_ref[idx]
        return (pltpu.make_async_copy(h_hbm.at[pl.ds(c * CHUNK, CHUNK)], hbuf.at[slot], sem.at[0, slot]),
                pltpu.make_async_copy(pos_hbm.at[ee, c], pbuf.at[slot], sem.at[1, slot]),
                pltpu.make_async_copy(cw_hbm.at[ee, c], cbuf.at[slot], sem.at[2, slot]))

    def start(idx):
        @pl.when(idx < total)
        def _():
            for cp in copies(idx):
                cp.start()

    @pl.when(d == 0)
    def _():
        for k in range(ahead):
            start(k)

    acc_ref[...] = jnp.zeros_like(acc_ref)
    gacc_ref[...] = jnp.zeros_like(gacc_ref)
    row = d * BLK + lax.broadcasted_iota(I32, (GWIN, CHUNK), 0)
    row2 = d * BLK + lax.broadcasted_iota(I32, (2 * GWIN, CHUNK), 0)

    def body(j, carry):
        plan = []
        for half in range(2):
            idx = base + 2 * j + half
            slot = idx % GSLOTS
            for cp in copies(idx):
                cp.wait()
            start(idx + ahead)
            c = pc_ref[idx]
            ok = pk_ref[idx] > 0
            first = jnp.where(ok, jnp.clip(bef_ref[c * N_EXP + e] - r0, 0, BLK), 0)
            last = jnp.where(ok, jnp.clip(aft_ref[c * N_EXP + e] - r0, 0, BLK), 0)
            plan.append((half, slot, ok, jnp.minimum(first // GWIN, BLK // GWIN - 2), last))

        def fill(half, slot, ok, w, rows_at):
            w0 = pl.multiple_of(w * GWIN, GWIN)
            n = rows_at.shape[0]
            hit = (pbuf[slot] == rows_at + w0) & ok
            acc_ref[half, pl.ds(w0, n), :] += jnp.dot(hit.astype(BF16), hbuf[slot], preferred_element_type=F32)
            gacc_ref[half, pl.ds(w0, n), :] += jnp.sum(jnp.where(hit, cbuf[slot], 0.0), axis=1, keepdims=True)

        for half, slot, ok, w_first, last in plan:
            fill(half, slot, ok, w_first, row2)

        for half, slot, ok, w_first, last in plan:
            def window(w, c2, half=half, slot=slot, ok=ok):
                fill(half, slot, ok, w, row)
                return c2

            lax.fori_loop(w_first + 2, (last + GWIN - 1) // GWIN, window, 0)
        return carry

    lax.fori_loop(0, pn_ref[d] // 2, body, 0)
    o_ref[...] = (acc_ref[0] + acc_ref[1]).astype(BF16)
    gs_ref[...] = gacc_ref[0] + gacc_ref[1]


def _gather(h, pos_t, cw_t, blk_e, pbase, pcount, r0, before, after, pair_c, pair_e, pair_ok, ptotal, nblk):
    hbm = pl.BlockSpec(memory_space=pl.ANY)
    return pl.pallas_call(
        _gather_kernel,
        out_shape=(jax.ShapeDtypeStruct((nblk * BLK, D), BF16), jax.ShapeDtypeStruct((nblk * BLK, 1), F32)),
        grid_spec=pltpu.PrefetchScalarGridSpec(
            num_scalar_prefetch=10,
            grid=(nblk,),
            in_specs=[hbm, hbm, hbm],
            out_specs=(pl.BlockSpec((BLK, D), lambda d, *_: (d, 0)), pl.BlockSpec((BLK, 1), lambda d, *_: (d, 0))),
            scratch_shapes=[pltpu.VMEM((GSLOTS, CHUNK, D), BF16), pltpu.VMEM((GSLOTS, 1, CHUNK), I32),
                            pltpu.VMEM((GSLOTS, 1, CHUNK), F32), pltpu.VMEM((2, BLK, D), F32),
                            pltpu.VMEM((2, BLK, 1), F32), pltpu.SemaphoreType.DMA((3, GSLOTS))]),
        compiler_params=_params(("arbitrary",)),
        name="moe_gather",
    )(blk_e, pbase, pcount, r0, before, after, pair_c, pair_e, pair_ok, ptotal, h, pos_t, cw_t)


def _for_live_rows(nrows, compute, o_ref):
    quarter = BLK // 4
    for q in range(1, 5):
        hi = q * quarter
        cond = (nrows > hi - quarter) & (nrows <= hi) if q < 4 else nrows > hi - quarter

        @pl.when(cond)
        def _(hi=hi):
            compute(slice(0, hi))
            if hi < BLK:
                o_ref[hi:, :] = jnp.zeros((BLK - hi, o_ref.shape[1]), o_ref.dtype)

    @pl.when(nrows == 0)
    def _():
        o_ref[...] = jnp.zeros_like(o_ref)


def _expert_up_kernel(be_ref, nv_ref, x_ref, wg_ref, wu_ref, o_ref, wgbf_ref, wubf_ref):
    d = pl.program_id(1)
    changed = (d == 0) | (be_ref[d] != be_ref[jnp.maximum(d - 1, 0)])

    @pl.when(changed)
    def _():
        wgbf_ref[...] = wg_ref[0].astype(BF16)
        wubf_ref[...] = wu_ref[0].astype(BF16)

    def up(rows):
        x = x_ref[rows, :]
        a = jnp.dot(x, wgbf_ref[...], preferred_element_type=F32)
        b = jnp.dot(x, wubf_ref[...], preferred_element_type=F32)
        o_ref[rows, :] = (a * jax.nn.sigmoid(a) * b).astype(BF16)

    _for_live_rows(nv_ref[d], up, o_ref)


def _expert_up(xs, blk_e, nvalid, wg, wu, tf):
    p = xs.shape[0]
    ff = wg.shape[2]
    return pl.pallas_call(
        _expert_up_kernel,
        out_shape=jax.ShapeDtypeStruct((p, ff), BF16),
        grid_spec=pltpu.PrefetchScalarGridSpec(
            num_scalar_prefetch=2,
            grid=(ff // tf, p // BLK),
            in_specs=[pl.BlockSpec((BLK, D), lambda f, d, be, nv: (d, 0)),
                      pl.BlockSpec((1, D, tf), lambda f, d, be, nv: (be[d], 0, f)),
                      pl.BlockSpec((1, D, tf), lambda f, d, be, nv: (be[d], 0, f))],
            out_specs=pl.BlockSpec((BLK, tf), lambda f, d, be, nv: (d, f)),
            scratch_shapes=[pltpu.VMEM((D, tf), BF16), pltpu.VMEM((D, tf), BF16)]),
        compiler_params=_params(("arbitrary", "arbitrary"), V7X_VMEM_LIMIT),
        name="moe_up",
    )(blk_e, nvalid, xs, wg, wu)


def _expert_down_kernel(be_ref, nv_ref, h_ref, gs_ref, wd_ref, o_ref, wdbf_ref):
    d = pl.program_id(1)
    changed = (d == 0) | (be_ref[d] != be_ref[jnp.maximum(d - 1, 0)])

    @pl.when(changed)
    def _():
        wdbf_ref[...] = wd_ref[0].astype(BF16)

    def down(rows):
        y = jnp.dot(h_ref[rows, :], wdbf_ref[...], preferred_element_type=F32)
        o_ref[rows, :] = (y * gs_ref[rows, :]).astype(BF16)

    _for_live_rows(nv_ref[d], down, o_ref)


def _expert_down(hh, gsort, blk_e, nvalid, wd, tn):
    p, ff = hh.shape
    return pl.pallas_call(
        _expert_down_kernel,
        out_shape=jax.ShapeDtypeStruct((p, D), BF16),
        grid_spec=pltpu.PrefetchScalarGridSpec(
            num_scalar_prefetch=2,
            grid=(D // tn, p // BLK),
            in_specs=[pl.BlockSpec((BLK, ff), lambda n, d, be, nv: (d, 0)),
                      pl.BlockSpec((BLK, 1), lambda n, d, be, nv: (d, 0)),
                      pl.BlockSpec((1, ff, tn), lambda n, d, be, nv: (be[d], 0, n))],
            out_specs=pl.BlockSpec((BLK, tn), lambda n, d, be, nv: (d, n)),
            scratch_shapes=[pltpu.VMEM((ff, tn), BF16)]),
        compiler_params=_params(("arbitrary", "arbitrary"), V7X_VMEM_LIMIT),
        name="moe_down",
    )(blk_e, nvalid, hh, gsort, wd)


def _combine_kernel(rs_ref, rc_ref, xp_ref, xs_ref, pos_ref, g_ref, y_hbm, op_ref, os_ref, wbuf, obuf, acc_ref,
                    sem, osem, *, n_prompt_chunks):
    i = pl.program_id(0)
    nch = pl.num_programs(0)

    def window_start(chunk, e):
        return pl.multiple_of(rs_ref[chunk * N_EXP + e] // BF16_ROWS * BF16_ROWS, BF16_ROWS)

    def copy(chunk, e, slot):
        return pltpu.make_async_copy(y_hbm.at[pl.ds(window_start(chunk, e), CWIN)],
                                     wbuf.at[slot, pl.ds(e * CWIN, CWIN)], sem.at[slot, e])

    @pl.when(i == 0)
    def _():
        for e in range(N_EXP):
            copy(0, e, 0).start()

    slot = i % 2

    @pl.when(i + 1 < nch)
    def _():
        for e in range(N_EXP):
            copy(i + 1, e, 1 - slot).start()

    col = lax.broadcasted_iota(I32, (CHUNK, CWIN), 1)
    hits = []
    for e in range(N_EXP):
        copy(i, e, slot).wait()
        hits.append((pos_ref[:, e:e + 1] - window_start(i, e)) == col)
    onehot = jnp.concatenate(hits, axis=1).astype(BF16)
    acc_ref[...] = jnp.dot(onehot, wbuf[slot], preferred_element_type=F32)

    for e in range(N_EXP):
        for k in range(1, (CHUNK + BF16_ROWS - 1) // CWIN + 1):
            tail = window_start(i, e) + k * CWIN

            @pl.when(rs_ref[i * N_EXP + e] + rc_ref[i * N_EXP + e] > tail)
            def _():
                cp = pltpu.make_async_copy(y_hbm.at[pl.ds(tail, CWIN)], obuf, osem)
                cp.start()
                cp.wait()
                hit = ((pos_ref[:, e:e + 1] - tail) == col).astype(BF16)
                acc_ref[...] += jnp.dot(hit, obuf[...], preferred_element_type=F32)

    x = jnp.where(i < n_prompt_chunks, xp_ref[...], xs_ref[...])
    out = x + _rms(acc_ref[...], g_ref[...])

    @pl.when(i < n_prompt_chunks)
    def _():
        op_ref[...] = out

    @pl.when(i >= n_prompt_chunks)
    def _():
        os_ref[...] = out


def _combine(run_start, run_count, xp, xs, pos, g, ys):
    npc = xp.shape[0] // CHUNK
    nsc = xs.shape[0] // CHUNK
    nch = npc + nsc
    tok8 = pl.BlockSpec((CHUNK, N_EXP), lambda i, *_: (i, 0))
    pspec = pl.BlockSpec((CHUNK, D), lambda i, *_: (jnp.minimum(i, npc - 1), 0))
    sspec = pl.BlockSpec((CHUNK, D), lambda i, *_: (jnp.maximum(i - npc, 0), 0))
    return pl.pallas_call(
        functools.partial(_combine_kernel, n_prompt_chunks=npc),
        out_shape=(jax.ShapeDtypeStruct(xp.shape, F32), jax.ShapeDtypeStruct(xs.shape, F32)),
        grid_spec=pltpu.PrefetchScalarGridSpec(
            num_scalar_prefetch=2,
            grid=(nch,),
            in_specs=[pspec, sspec, tok8, pl.BlockSpec((1, D), lambda i, *_: (0, 0)),
                      pl.BlockSpec(memory_space=pl.ANY)],
            out_specs=(pspec, sspec),
            scratch_shapes=[pltpu.VMEM((2, N_EXP * CWIN, D), BF16), pltpu.VMEM((CWIN, D), BF16),
                            pltpu.VMEM((CHUNK, D), F32),
                            pltpu.SemaphoreType.DMA((2, N_EXP)), pltpu.SemaphoreType.DMA]),
        compiler_params=_params(("arbitrary",), V7X_VMEM_LIMIT),
        name="moe_combine",
    )(run_start, run_count, xp, xs, pos, g, ys)


def _moe(xp, xs, g4, g5, w_router, wg, wu, wd):
    npc = xp.shape[0] // CHUNK
    nch = npc + xs.shape[0] // CHUNK
    t = nch * CHUNK
    h, rk, rk_t, cw_t, cnt = _route(xp, xs, g4, w_router.T)

    after = cnt.reshape(nch, N_EXP).astype(I32)
    before = jnp.concatenate([jnp.zeros((1, N_EXP), I32), after[:-1]], axis=0)
    counts = after[-1]
    gsz = (counts + BLK - 1) // BLK * BLK
    gend = jnp.cumsum(gsz)
    gstart = gend - gsz
    nblk = (2 * t + N_EXP * (BLK - 1)) // BLK + 1 + (2 * CWIN + BLK - 1) // BLK
    bstart = jnp.arange(nblk, dtype=I32) * BLK
    blk_e = jnp.minimum(jnp.sum(gend[None, :] <= bstart[:, None], axis=1), N_EXP - 1).astype(I32)
    valid = bstart < gend[-1]

    def pick(table, idx, n):
        hit = idx[:, None] == jnp.arange(n, dtype=I32)[None, :]
        return jnp.sum(jnp.where(hit, table[..., None, :], 0), axis=-1)

    r0 = bstart - pick(gstart, blk_e, N_EXP)
    aft_e = pick(after, blk_e, N_EXP)
    bef_e = pick(before, blk_e, N_EXP)
    c_lo = jnp.where(valid, jnp.sum(aft_e <= r0[None, :], axis=0), 0).astype(I32)
    c_hi = jnp.where(valid, jnp.sum(bef_e < (r0 + BLK)[None, :], axis=0) - 1, -1).astype(I32)
    c_lo = jnp.minimum(c_lo, nch - 1)
    pos = jnp.where(rk >= 0, rk + gstart[None, :], -1).astype(I32)
    pos_t = jnp.where(rk_t >= 0, rk_t + gstart[:, None], -1).astype(I32).reshape(N_EXP, nch, 1, CHUNK)
    cw_t = cw_t.reshape(N_EXP, nch, 1, CHUNK)
    run_start = (gstart[None, :] + before).astype(I32).reshape(-1)
    run_count = (after - before).reshape(-1)
    nvalid = jnp.where(valid, jnp.clip(pick(counts, blk_e, N_EXP) - r0, 0, BLK), 0).astype(I32)
    nsrc = jnp.maximum(c_hi - c_lo + 1, 0)
    pcount = nsrc + (nsrc & 1)
    pend = jnp.cumsum(pcount)
    pbase = pend - pcount
    k = jnp.arange(2 * nblk + 2 * nch * N_EXP, dtype=I32)
    pair_blk = jnp.minimum(jnp.sum(pend[None, :] <= k[:, None], axis=1), nblk - 1)
    per_blk = pick(jnp.stack([pbase, nsrc, c_lo, blk_e]), pair_blk, nblk)
    k_local = k - per_blk[0]
    pair_ok = (k_local < per_blk[1]).astype(I32)
    pair_c = jnp.clip(per_blk[2] + jnp.minimum(k_local, per_blk[1] - 1), 0, nch - 1).astype(I32)
    pair_e = per_blk[3].astype(I32)

    xsort, gsort = _gather(h, pos_t, cw_t, blk_e, pbase.astype(I32), pcount.astype(I32), r0.astype(I32),
                           before.reshape(-1), after.reshape(-1), pair_c, pair_e, pair_ok,
                           pend[-1:].astype(I32), nblk)
    hh = _expert_up(xsort, blk_e, nvalid, wg, wu, tf=1792)
    ysort = _expert_down(hh, gsort, blk_e, nvalid, wd, tn=D)
    return _combine(run_start, run_count, xp, xs, pos, g5, ysort)


def kernel(x_prompt, x_sample, cache_conv, cache_mem_k, cache_mem_v, state_ssm_re, state_ssm_im, mem_prompt, norm_g, mem_norm_g, w_xq, w_xk, w_xv, w_xo, conv_w_pw1, conv_b_pw1, conv_w_dw, conv_b_dw, conv_ln_g, conv_ln_b, conv_w_pw2, conv_b_pw2, ssm_a_re, ssm_a_im, ssm_log_dt, ssm_b_re, ssm_b_im, ssm_c_re, ssm_c_im, ssm_d, ssm_w_glu, ssm_b_glu, ffn_w_gate, ffn_w_up, ffn_w_down, moe_w_router, moe_w_gate, moe_w_up, moe_w_down):
    nbp, seqp, _ = x_prompt.shape
    nbs, seqs, _ = x_sample.shape
    tp = nbp * seqp
    ts = nbs * seqs
    row = lambda a: a.reshape(1, -1)
    g = lambda i, k: norm_g[i, k].reshape(1, D)

    nl = w_xk.shape[0]
    pk, pv = _mem_kv(mem_prompt, mem_norm_g.reshape(nl, 1, D), w_xk, w_xv)
    p_mem_k = pk.reshape(nl, nbp, N_MEM, N_HEADS, HEAD_DIM)
    p_mem_v = pv.reshape(nl, nbp, N_MEM, N_HEADS, HEAD_DIM)
    ck = cache_mem_k.reshape(nl * nbs, N_MEM, N_HEADS, HEAD_DIM)
    cv = cache_mem_v.reshape(nl * nbs, N_MEM, N_HEADS, HEAD_DIM)

    conv_args = (conv_w_dw[0], row(conv_b_dw[0]), row(conv_ln_g[0]), row(conv_ln_b[0]),
                 conv_w_pw2[0], row(conv_b_pw2[0]), g(0, 1))
    xp = x_prompt.reshape(tp, D)
    xs = x_sample.reshape(ts, D)
    up = _conv_pw1(xp, g(0, 0), conv_w_pw1[0], row(conv_b_pw1[0]), tm=1024).reshape(nbp, seqp, D)
    us = _conv_pw1(xs, g(0, 0), conv_w_pw1[0], row(conv_b_pw1[0]), tm=ts).reshape(nbs, seqs, D)
    hist_p = jnp.zeros((nbp, HIST, D), F32)
    hist_s = jnp.pad(cache_conv[0], ((0, 0), (HIST - CONV_W + 1, 0), (0, 0)))
    xp = _conv_dw_pw2(up, x_prompt, hist_p, *conv_args, tl=512).reshape(tp, D)
    xs = _conv_dw_pw2(us, x_sample, hist_s, *conv_args, tl=seqs).reshape(ts, D)
    p_conv = up[:, seqp - (CONV_W - 1):][None]
    s_conv = jnp.concatenate([cache_conv[0], us], axis=1)[:, -(CONV_W - 1):][None]

    tma = 1024
    xp = _attn(xp, pk, pv, 0, w_xq[0], w_xo[0], g(0, 2), g(0, 3), nbp, seqp, tma)
    xs = _attn(xs, ck, cv, 0, w_xq[0], w_xo[0], g(0, 2), g(0, 3), nbs, seqs, seqs)

    ffn_w = (_to_bf16(ffn_w_gate[0], 4), _to_bf16(ffn_w_up[0], 4), _to_bf16(ffn_w_down[0], 4))
    tff = ffn_w_gate.shape[2] // 2
    xp = _ffn(xp, g(0, 4), g(0, 5), *ffn_w, tm=1024, tf=tff, out_shape=(tp, D), out_index=lambda i, f: (i, 0))
    xs = _ffn(xs, g(0, 4), g(0, 5), *ffn_w, tm=ts, tf=tff, out_shape=(ts, D), out_index=lambda i, f: (i, 0))

    (ab_r, ab_i, a2_r, a2_i), (bb_r, bb_i, abb_r, abb_i) = _ssm_prep(
        ssm_a_re[0], ssm_a_im[0], ssm_log_dt[0], ssm_b_re[0], ssm_b_im[0])
    bdiag = lambda m: _block_diag(m.transpose(0, 2, 1)).astype(BF16)
    bm_r, bm_i = bdiag(bb_r), bdiag(bb_i)
    cm_r, cm_i = bdiag(ssm_c_re[0]), bdiag(-ssm_c_im[0])
    tail_args = (cm_r, cm_i, row(ssm_d[0]), _to_bf16(ssm_w_glu[0], 4), row(ssm_b_glu[0]))
    half = SUBLANES // 2
    hdiag = lambda m: _block_diag(m.transpose(0, 2, 1), gpb=SSM_CB // SSM_C // 2).astype(BF16)
    pair_args = (jnp.concatenate([jnp.tile(ab_r, (half, 1)), jnp.tile(a2_r, (half, 1))]),
                 jnp.concatenate([jnp.tile(ab_i, (half, 1)), jnp.tile(a2_i, (half, 1))]),
                 jnp.concatenate([hdiag(bb_r), hdiag(abb_r)], axis=1),
                 jnp.concatenate([hdiag(bb_i), hdiag(abb_i)], axis=1))
    zero_state = jnp.zeros((SUBLANES, SSM_N), F32)
    xp, p_sr, p_si = _ssm(xp.reshape(nbp, seqp, D), g(1, 0), g(1, 1), zero_state, zero_state,
                          *pair_args, *tail_args, tl=64)
    xs, s_sr, s_si = _ssm(xs.reshape(nbs, seqs, D), g(1, 0), g(1, 1), state_ssm_re[0].reshape(nbs, SSM_N),
                          state_ssm_im[0].reshape(nbs, SSM_N), jnp.tile(ab_r, (nbs, 1)), jnp.tile(ab_i, (nbs, 1)),
                          bm_r, bm_i, *tail_args, tl=seqs)
    st = lambda a, n: a[:n].reshape(1, n, SSM_G, SSM_P)
    p_ssm_re, p_ssm_im = st(p_sr, nbp), st(p_si, nbp)
    s_ssm_re, s_ssm_im = st(s_sr, nbs), st(s_si, nbs)

    xp = _attn(xp.reshape(tp, D), pk, pv, nbp, w_xq[1], w_xo[1], g(1, 2), g(1, 3), nbp, seqp, tma)
    xs = _attn(xs.reshape(ts, D), ck, cv, nbs, w_xq[1], w_xo[1], g(1, 2), g(1, 3), nbs, seqs, seqs)

    yp, ysm = _moe(xp, xs, g(1, 4), g(1, 5), moe_w_router[0], moe_w_gate[0], moe_w_up[0], moe_w_down[0])
    return (yp.reshape(nbp, seqp, D), ysm.reshape(nbs, seqs, D), p_conv, p_mem_k, p_mem_v,
            p_ssm_re, p_ssm_im, s_conv, s_ssm_re, s_ssm_im)
```

```python
import functools

import jax
import jax.numpy as jnp
from jax import lax
from jax.experimental import pallas as pl
from jax.experimental.pallas import tpu as pltpu

F32 = jnp.float32
BF16 = jnp.bfloat16
I32 = jnp.int32

D = 1024
CONV_W = 31
HIST = 32
N_MEM = 256
N_HEADS = 4
HEAD_DIM = D // N_HEADS
SSM_G = 64
SSM_C = 16
SSM_P = 64
SSM_N = SSM_G * SSM_P
SSM_CB = 256
SSM_NCB = D // SSM_CB
SSM_SB = SSM_CB // SSM_C * SSM_P
N_EXP = 8
EPS = 1e-6

V7X_VMEM_LIMIT = 56 * 1024 * 1024
SUBLANES = 8

CHUNK = 256
BLK = 512
GWIN = 128
GSLOTS = 8
CWIN = 256
BF16_ROWS = 16


def _params(sem, vmem=None):
    return pltpu.CompilerParams(dimension_semantics=sem, vmem_limit_bytes=vmem)


def _rms(x, g):
    return x * lax.rsqrt(jnp.mean(x * x, axis=-1, keepdims=True) + EPS) * g


def _const_spec(shape):
    nd = len(shape)
    return pl.BlockSpec(shape, lambda *_: (0,) * nd)


def _pw1_kernel(x_ref, g_ref, w_ref, b_ref, u_ref, wbf_ref):
    @pl.when(pl.program_id(0) == 0)
    def _():
        wbf_ref[...] = w_ref[...].astype(BF16)

    h = _rms(x_ref[...], g_ref[...]).astype(BF16)
    z = jnp.dot(h, wbf_ref[...], preferred_element_type=F32) + b_ref[...]
    u_ref[...] = z[:, :D] * jax.nn.sigmoid(z[:, D:])


def _conv_pw1(x, g, w, b, tm):
    t = x.shape[0]
    return pl.pallas_call(
        _pw1_kernel,
        out_shape=jax.ShapeDtypeStruct((t, D), F32),
        grid=(t // tm,),
        in_specs=[pl.BlockSpec((tm, D), lambda i: (i, 0)),
                  _const_spec((1, D)), _const_spec((D, 2 * D)), _const_spec((1, 2 * D))],
        out_specs=pl.BlockSpec((tm, D), lambda i: (i, 0)),
        scratch_shapes=[pltpu.VMEM((D, 2 * D), BF16)],
        compiler_params=_params(("arbitrary",), V7X_VMEM_LIMIT),
        name="conv_pw1",
    )(x, g, w, b)


def _conv2_kernel(u_ref, x_ref, hist_ref, wdw_ref, bdw_ref, lng_ref, lnb_ref, w2_ref, b2_ref,
                  g_ref, o_ref, ext_ref, sh_ref, conv_ref, w2bf_ref, wtap_ref, *, tl, rt):
    bi = pl.program_id(0)
    li = pl.program_id(1)

    @pl.when((bi == 0) & (li == 0))
    def _():
        w2bf_ref[...] = w2_ref[...].astype(BF16)
        for k in range(CONV_W):
            wtap_ref[k] = jnp.broadcast_to(wdw_ref[k:k + 1, :], (SUBLANES, D))

    @pl.when(li == 0)
    def _():
        ext_ref[0:HIST, :] = hist_ref[0]

    @pl.when(li > 0)
    def _():
        ext_ref[0:HIST, :] = ext_ref[tl:tl + HIST, :]

    ext_ref[HIST:HIST + tl, :] = u_ref[0]
    for s in range(1, SUBLANES):
        sh_ref[s - 1] = ext_ref[pl.ds(s, tl + HIST - SUBLANES), :]

    def rows(i, carry):
        r0 = pl.multiple_of(i * rt, rt)
        ntile = rt // SUBLANES
        accs = [jnp.zeros((SUBLANES, D), F32) + bdw_ref[...] for _ in range(ntile)]
        for k in range(CONV_W):
            off = HIST - CONV_W + 1 + k
            s = off % SUBLANES
            base = off - s
            w = wtap_ref[k]
            for ti in range(ntile):
                rows_ti = pl.ds(r0 + base + ti * SUBLANES, SUBLANES)
                src = ext_ref[rows_ti, :] if s == 0 else sh_ref[s - 1, rows_ti, :]
                accs[ti] = accs[ti] + w * src
        for ti in range(ntile):
            conv_ref[pl.ds(r0 + ti * SUBLANES, SUBLANES), :] = accs[ti]
        return carry

    lax.fori_loop(0, tl // rt, rows, 0)
    acc = conv_ref[...]
    mu = jnp.mean(acc, axis=-1, keepdims=True)
    xc = acc - mu
    var = jnp.mean(xc * xc, axis=-1, keepdims=True)
    y = xc * lax.rsqrt(var + EPS) * lng_ref[...] + lnb_ref[...]
    y = y * jax.nn.sigmoid(y)
    t = jnp.dot(y.astype(BF16), w2bf_ref[...], preferred_element_type=F32) + b2_ref[...]
    o_ref[0] = x_ref[0] + _rms(t, g_ref[...])


def _conv_dw_pw2(u, x, hist, wdw, bdw, lng, lnb, w2, b2, g, tl):
    nb, seq, _ = u.shape
    tok = pl.BlockSpec((1, tl, D), lambda b, l: (b, l, 0))
    return pl.pallas_call(
        functools.partial(_conv2_kernel, tl=tl, rt=min(tl, 32)),
        out_shape=jax.ShapeDtypeStruct((nb, seq, D), F32),
        grid=(nb, seq // tl),
        in_specs=[tok, tok, pl.BlockSpec((1, HIST, D), lambda b, l: (b, 0, 0)),
                  _const_spec((CONV_W, D)), _const_spec((1, D)), _const_spec((1, D)), _const_spec((1, D)),
                  _const_spec((D, D)), _const_spec((1, D)), _const_spec((1, D))],
        out_specs=tok,
        scratch_shapes=[pltpu.VMEM((tl + HIST, D), F32),
                        pltpu.VMEM((SUBLANES - 1, tl + HIST - SUBLANES, D), F32),
                        pltpu.VMEM((tl, D), F32), pltpu.VMEM((D, D), BF16),
                        pltpu.VMEM((CONV_W, SUBLANES, D), F32)],
        compiler_params=_params(("arbitrary", "arbitrary"), V7X_VMEM_LIMIT),
        name="conv_dw_pw2",
    )(u, x, hist, wdw, bdw, lng, lnb, w2, b2, g)


def _memkv_kernel(m_ref, g_ref, wk_ref, wv_ref, k_ref, v_ref, wkbf_ref, wvbf_ref):
    @pl.when(pl.program_id(1) == 0)
    def _():
        wkbf_ref[...] = wk_ref[0].astype(BF16)
        wvbf_ref[...] = wv_ref[0].astype(BF16)

    m = _rms(m_ref[0], g_ref[0]).astype(BF16)
    k = jnp.dot(m, wkbf_ref[...], preferred_element_type=F32)
    v = jnp.dot(m, wvbf_ref[...], preferred_element_type=F32)
    for hd in range(N_HEADS):
        sl = slice(hd * HEAD_DIM, (hd + 1) * HEAD_DIM)
        k_ref[0, :, hd, :] = k[:, sl]
        v_ref[0, :, hd, :] = v[:, sl]


def _mem_kv(mem, g, wk, wv):
    nb = mem.shape[0]
    nl = wk.shape[0]
    kv = jax.ShapeDtypeStruct((nl * nb, N_MEM, N_HEADS, HEAD_DIM), F32)
    out = pl.BlockSpec((1, N_MEM, N_HEADS, HEAD_DIM), lambda l, b: (l * nb + b, 0, 0, 0))
    w = pl.BlockSpec((1, D, D), lambda l, b: (l, 0, 0))
    return pl.pallas_call(
        _memkv_kernel,
        out_shape=(kv, kv),
        grid=(nl, nb),
        in_specs=[pl.BlockSpec((1, N_MEM, D), lambda l, b: (b, 0, 0)),
                  pl.BlockSpec((1, 1, D), lambda l, b: (l, 0, 0)), w, w],
        out_specs=(out, out),
        scratch_shapes=[pltpu.VMEM((D, D), BF16), pltpu.VMEM((D, D), BF16)],
        compiler_params=_params(("arbitrary", "arbitrary"), V7X_VMEM_LIMIT),
        name="mem_kv",
    )(mem, g, wk, wv)


def _attn_kernel(x_ref, k_ref, v_ref, wq_ref, wo_ref, g2_ref, g3_ref, o_ref, wqbf_ref, wobf_ref, kb_ref, vb_ref):
    @pl.when((pl.program_id(0) == 0) & (pl.program_id(1) == 0))
    def _():
        wqbf_ref[...] = wq_ref[...].astype(BF16)
        wobf_ref[...] = wo_ref[...].astype(BF16)

    @pl.when(pl.program_id(1) == 0)
    def _():
        for hd in range(N_HEADS):
            sl = slice(hd * HEAD_DIM, (hd + 1) * HEAD_DIM)
            kb_ref[:, sl] = k_ref[0, :, hd, :].astype(BF16)
            vb_ref[:, sl] = v_ref[0, :, hd, :].astype(BF16)

    x = x_ref[...]
    h = _rms(x, g2_ref[...]).astype(BF16)
    q = (jnp.dot(h, wqbf_ref[...], preferred_element_type=F32) * (HEAD_DIM ** -0.5)).astype(BF16)
    heads = []
    for hd in range(N_HEADS):
        sl = slice(hd * HEAD_DIM, (hd + 1) * HEAD_DIM)
        s = lax.dot_general(q[:, sl], kb_ref[:, sl], (((1,), (1,)), ((), ())), preferred_element_type=F32)
        p = jnp.exp(s - jnp.max(s, axis=-1, keepdims=True))
        p = p / jnp.sum(p, axis=-1, keepdims=True)
        heads.append(jnp.dot(p.astype(BF16), vb_ref[:, sl], preferred_element_type=F32))
    o = jnp.concatenate(heads, axis=1).astype(BF16)
    t = jnp.dot(o, wobf_ref[...], preferred_element_type=F32)
    o_ref[...] = x + _rms(t, g3_ref[...])


def _attn(x, k, v, kv_base, wq, wo, g2, g3, nb, seq, tm):
    tok = pl.BlockSpec((tm, D), lambda b, l: (b * (seq // tm) + l, 0))
    kv = pl.BlockSpec((1, N_MEM, N_HEADS, HEAD_DIM), lambda b, l: (kv_base + b, 0, 0, 0))
    return pl.pallas_call(
        _attn_kernel,
        out_shape=jax.ShapeDtypeStruct((nb * seq, D), F32),
        grid=(nb, seq // tm),
        in_specs=[tok, kv, kv, _const_spec((D, D)), _const_spec((D, D)), _const_spec((1, D)), _const_spec((1, D))],
        out_specs=tok,
        scratch_shapes=[pltpu.VMEM((D, D), BF16), pltpu.VMEM((D, D), BF16),
                        pltpu.VMEM((N_MEM, D), BF16), pltpu.VMEM((N_MEM, D), BF16)],
        compiler_params=_params(("arbitrary", "arbitrary"), V7X_VMEM_LIMIT),
        name="mem_attn",
    )(x, k, v, wq, wo, g2, g3)


def _cast_kernel(w_ref, o_ref):
    o_ref[...] = w_ref[...].astype(BF16)


def _to_bf16(w, nsplit):
    r, c = w.shape
    blk = pl.BlockSpec((r // nsplit, c), lambda i: (i, 0))
    return pl.pallas_call(
        _cast_kernel,
        out_shape=jax.ShapeDtypeStruct((r, c), BF16),
        grid=(nsplit,),
        in_specs=[blk],
        out_specs=blk,
        compiler_params=_params(("arbitrary",)),
        name="to_bf16",
    )(w)


def _ffn_kernel(x_ref, g4_ref, g5_ref, wg_ref, wu_ref, wd_ref, o_ref, h_ref, acc_ref):
    f = pl.program_id(1)

    @pl.when(f == 0)
    def _():
        h_ref[...] = _rms(x_ref[...], g4_ref[...]).astype(BF16)
        acc_ref[...] = jnp.zeros_like(acc_ref)

    h = h_ref[...]
    a = jnp.dot(h, wg_ref[...], preferred_element_type=F32)
    b = jnp.dot(h, wu_ref[...], preferred_element_type=F32)
    hh = (a * jax.nn.sigmoid(a) * b).astype(BF16)
    acc_ref[...] += jnp.dot(hh, wd_ref[...], preferred_element_type=F32)

    @pl.when(f == pl.num_programs(1) - 1)
    def _():
        o_ref[...] = x_ref[...] + _rms(acc_ref[...], g5_ref[...])


def _ffn(x, g4, g5, wg, wu, wd, tm, tf, out_shape, out_index):
    t = x.shape[0]
    ff = wg.shape[1]
    return pl.pallas_call(
        _ffn_kernel,
        out_shape=jax.ShapeDtypeStruct(out_shape, F32),
        grid=(t // tm, ff // tf),
        in_specs=[pl.BlockSpec((tm, D), lambda i, f: (i, 0)),
                  _const_spec((1, D)), _const_spec((1, D)),
                  pl.BlockSpec((D, tf), lambda i, f: (0, f)),
                  pl.BlockSpec((D, tf), lambda i, f: (0, f)),
                  pl.BlockSpec((tf, D), lambda i, f: (f, 0))],
        out_specs=pl.BlockSpec((tm, D), out_index),
        scratch_shapes=[pltpu.VMEM((tm, D), BF16), pltpu.VMEM((tm, D), F32)],
        compiler_params=_params(("arbitrary", "arbitrary"), V7X_VMEM_LIMIT),
        name="dense_ffn",
    )(x, g4, g5, wg, wu, wd)


def _ssm_prep_kernel(lr_ref, li_ref, ldt_ref, br_ref, bi_ref, abr_ref, abi_ref, bbr_ref, bbi_ref,
                     a2r_ref, a2i_ref, abbr_ref, abbi_ref):
    dt = jnp.exp(ldt_ref[...])
    lr = lr_ref[...]
    li = li_ref[...]
    mag = jnp.exp(lr * dt)
    ab_r = mag * jnp.cos(li * dt)
    ab_i = mag * jnp.sin(li * dt)
    den = lr * lr + li * li
    nr = ab_r - 1.0
    k_r = (nr * lr + ab_i * li) / den
    k_i = (ab_i * lr - nr * li) / den
    br = br_ref[...]
    bi = bi_ref[...]
    bb_r = k_r * br - k_i * bi
    bb_i = k_r * bi + k_i * br
    abr_ref[...] = ab_r
    abi_ref[...] = ab_i
    bbr_ref[...] = bb_r
    bbi_ref[...] = bb_i
    a2r_ref[...] = ab_r * ab_r - ab_i * ab_i
    a2i_ref[...] = 2.0 * (ab_r * ab_i)
    abbr_ref[...] = ab_r * bb_r - ab_i * bb_i
    abbi_ref[...] = ab_r * bb_i + ab_i * bb_r


def _ssm_prep(a_re, a_im, log_dt, b_re, b_im):
    n = SSM_P * SSM_C
    rep = lambda a: jnp.repeat(a, SSM_C, axis=1)
    shp = jax.ShapeDtypeStruct((SSM_G, n), F32)
    abr, abi, bbr, bbi, a2r, a2i, abbr, abbi = pl.pallas_call(
        _ssm_prep_kernel,
        out_shape=(shp,) * 8,
        name="ssm_prep",
    )(rep(a_re), rep(a_im), log_dt.reshape(SSM_G, 1), b_re.reshape(SSM_G, n), b_im.reshape(SSM_G, n))
    pick = lambda a: a.reshape(SSM_G, SSM_P, SSM_C)[:, :, 0].reshape(1, SSM_N)
    gpc = lambda a: a.reshape(SSM_G, SSM_P, SSM_C)
    return (pick(abr), pick(abi), pick(a2r), pick(a2i)), (gpc(bbr), gpc(bbi), gpc(abbr), gpc(abbi))


def _block_diag(m, gpb=SSM_CB // SSM_C):
    g, a, b = m.shape
    m = m.reshape(g // gpb, gpb, a, 1, b)
    eye = jnp.eye(gpb, dtype=m.dtype).reshape(1, gpb, 1, gpb, 1)
    return (m * eye).reshape(g // gpb, gpb * a, gpb * b)


def _ssm_kernel(x_ref, g0_ref, g1_ref, s0r_ref, s0i_ref, ar_ref, ai_ref, bmr_ref, bmi_ref, cmr_ref, cmi_ref,
                d_ref, wglu_ref, bglu_ref, o_ref, sr_ref, si_ref,
                bur_ref, bui_ref, *, nb, rb, lc):
    @pl.when(pl.program_id(0) == 0)
    def _():
        sr_ref[...] = s0r_ref[...]
        si_ref[...] = s0i_ref[...]

    tl = rb // nb
    x = x_ref[...].reshape(rb, D)
    h = _rms(x, g0_ref[...])
    r = lax.broadcasted_iota(I32, (rb, rb), 0)
    c = lax.broadcasted_iota(I32, (rb, rb), 1)
    to_time_major = (c == (r & (nb - 1)) * tl + (r >> (nb.bit_length() - 1))).astype(BF16)
    to_batch_major = (c == (r & (tl - 1)) * nb + (r >> (tl.bit_length() - 1))).astype(BF16)
    hb = jnp.dot(to_time_major, h.astype(BF16), preferred_element_type=F32).astype(BF16)
    rows = max(nb, SUBLANES)
    pair = nb < SUBLANES
    if pair:
        odd = ((r >> (nb.bit_length() - 1)) & 1) == 1
        prev_time_major = ((c == (r & (nb - 1)) * tl + (r >> (nb.bit_length() - 1)) - 1) & odd).astype(BF16)
        hprev = jnp.dot(prev_time_major, h.astype(BF16), preferred_element_type=F32).astype(BF16)
        low = lax.broadcasted_iota(I32, (SUBLANES, lc), 0) < nb

    def scan_chunk(cs):
        a_r = ar_ref[:, cs]
        a_i = ai_ref[:, cs]
        s_r = sr_ref[:, cs]
        s_i = si_ref[:, cs]
        for j in range(rb // rows):
            rs = slice(j * rows, (j + 1) * rows)
            n_r = a_r * s_r - a_i * s_i + bur_ref[rs, cs]
            n_i = a_r * s_i + a_i * s_r + bui_ref[rs, cs]
            bur_ref[rs, cs] = n_r
            bui_ref[rs, cs] = n_i
            if pair:
                s_r = jnp.where(low, pltpu.roll(n_r, nb, 0), n_r)
                s_i = jnp.where(low, pltpu.roll(n_i, nb, 0), n_i)
            else:
                s_r, s_i = n_r, n_i
        sr_ref[:, cs] = s_r
        si_ref[:, cs] = s_i

    ist = bmr_ref.shape[2]
    ich = bmr_ref.shape[1] // 2 if pair else bmr_ref.shape[1]
    ys = []
    for cb in range(SSM_NCB):
        ss = slice(cb * SSM_SB, (cb + 1) * SSM_SB)
        for ib in range(cb * (SSM_SB // ist), (cb + 1) * (SSM_SB // ist)):
            hs = hb[:, ib * ich:(ib + 1) * ich]
            if pair:
                hs = jnp.concatenate([hs, hprev[:, ib * ich:(ib + 1) * ich]], axis=1)
            bur_ref[:, ib * ist:(ib + 1) * ist] = jnp.dot(hs, bmr_ref[ib], preferred_element_type=F32)
            bui_ref[:, ib * ist:(ib + 1) * ist] = jnp.dot(hs, bmi_ref[ib], preferred_element_type=F32)
            for c in range(ist // lc):
                scan_chunk(slice(ib * ist + c * lc, ib * ist + (c + 1) * lc))
        ys.append(jnp.dot(bur_ref[:, ss].astype(BF16), cmr_ref[cb], preferred_element_type=F32)
                  + jnp.dot(bui_ref[:, ss].astype(BF16), cmi_ref[cb], preferred_element_type=F32))
    y_tm = jnp.concatenate(ys, axis=1)
    y1 = y_tm.astype(BF16)
    rem = y_tm - y1.astype(F32)
    y2 = rem.astype(BF16)
    y3 = (rem - y2.astype(F32)).astype(BF16)
    y = (jnp.dot(to_batch_major, y1, preferred_element_type=F32)
         + jnp.dot(to_batch_major, y2, preferred_element_type=F32)
         + jnp.dot(to_batch_major, y3, preferred_element_type=F32))
    y = y + d_ref[...] * h
    y = jax.nn.gelu(y).astype(BF16)
    z = jnp.dot(y, wglu_ref[...], preferred_element_type=F32) + bglu_ref[...]
    t = z[:, :D] * jax.nn.sigmoid(z[:, D:])
    o_ref[...] = (x + _rms(t, g1_ref[...])).reshape(nb, tl, D)


def _ssm(x, g0, g1, s0r, s0i, ab_r, ab_i, bm_r, bm_i, cm_r, cm_i, d, wglu, bglu, tl):
    nb, seq, _ = x.shape
    rb = nb * tl
    rows = max(nb, SUBLANES)
    st = jax.ShapeDtypeStruct((rows, SSM_N), F32)
    row = pl.BlockSpec((nb, tl, D), lambda i: (0, i, 0))
    return pl.pallas_call(
        functools.partial(_ssm_kernel, nb=nb, rb=rb, lc=512),
        out_shape=(jax.ShapeDtypeStruct((nb, seq, D), F32), st, st),
        grid=(seq // tl,),
        in_specs=[row, _const_spec((1, D)), _const_spec((1, D)),
                  _const_spec((rows, SSM_N)), _const_spec((rows, SSM_N)),
                  _const_spec((rows, SSM_N)), _const_spec((rows, SSM_N)),
                  _const_spec(bm_r.shape), _const_spec(bm_i.shape),
                  _const_spec((SSM_NCB, SSM_SB, SSM_CB)), _const_spec((SSM_NCB, SSM_SB, SSM_CB)),
                  _const_spec((1, D)), _const_spec((D, 2 * D)), _const_spec((1, 2 * D))],
        out_specs=(row, _const_spec((rows, SSM_N)), _const_spec((rows, SSM_N))),
        scratch_shapes=[pltpu.VMEM((rb, SSM_N), F32), pltpu.VMEM((rb, SSM_N), F32)],
        compiler_params=_params(("arbitrary",), V7X_VMEM_LIMIT),
        name="ssm",
    )(x, g0, g1, s0r, s0i, ab_r, ab_i, bm_r, bm_i, cm_r, cm_i, d, wglu, bglu)


def _route_kernel(xp_ref, xs_ref, g_ref, wrt_ref, h_ref, rk_ref, rkt_ref, cwt_ref, cnt_ref, carry_ref,
                  *, n_prompt_chunks):
    i = pl.program_id(0)

    @pl.when(i == 0)
    def _():
        carry_ref[...] = jnp.zeros_like(carry_ref)

    x = jnp.where(i < n_prompt_chunks, xp_ref[...], xs_ref[...])
    h = _rms(x, g_ref[...])
    h_ref[...] = h.astype(BF16)
    lg = lax.dot_general(wrt_ref[...], h, (((1,), (1,)), ((), ())), preferred_element_type=F32,
                         precision=lax.Precision.HIGHEST)
    ex = lax.broadcasted_iota(I32, lg.shape, 0)
    m1 = jnp.max(lg, axis=0, keepdims=True)
    i1 = jnp.min(jnp.where(lg == m1, ex, N_EXP), axis=0, keepdims=True)
    first = ex == i1
    lg2 = jnp.where(first, -jnp.inf, lg)
    m2 = jnp.max(lg2, axis=0, keepdims=True)
    i2 = jnp.min(jnp.where(lg2 == m2, ex, N_EXP), axis=0, keepdims=True)
    second = ex == i2
    e = jnp.exp(m2 - m1)
    den = 1.0 + e
    cwt_ref[...] = jnp.where(first, 1.0 / den, 0.0) + jnp.where(second, e / den, 0.0)
    assigned = first | second
    r = lax.broadcasted_iota(I32, (CHUNK, CHUNK), 0)
    c = lax.broadcasted_iota(I32, (CHUNK, CHUNK), 1)
    earlier = (r < c).astype(BF16)
    rank = jnp.dot(assigned.astype(BF16), earlier, preferred_element_type=F32) + carry_ref[...]
    rank = jnp.where(assigned, rank, -1.0)
    rkt_ref[...] = rank.astype(I32)
    rk_ref[...] = rank.T.astype(I32)
    carry_ref[...] += jnp.sum(assigned.astype(F32), axis=1, keepdims=True)
    cnt_ref[0] = carry_ref[...]


def _route(xp, xs, g, wrt):
    npc = xp.shape[0] // CHUNK
    nch = npc + xs.shape[0] // CHUNK
    t = nch * CHUNK
    exp_major = pl.BlockSpec((N_EXP, CHUNK), lambda i: (0, i))
    return pl.pallas_call(
        functools.partial(_route_kernel, n_prompt_chunks=npc),
        out_shape=(jax.ShapeDtypeStruct((t, D), BF16), jax.ShapeDtypeStruct((t, N_EXP), I32),
                   jax.ShapeDtypeStruct((N_EXP, t), I32), jax.ShapeDtypeStruct((N_EXP, t), F32),
                   jax.ShapeDtypeStruct((nch, N_EXP, 1), F32)),
        grid=(nch,),
        in_specs=[pl.BlockSpec((CHUNK, D), lambda i: (jnp.minimum(i, npc - 1), 0)),
                  pl.BlockSpec((CHUNK, D), lambda i: (jnp.maximum(i - npc, 0), 0)),
                  _const_spec((1, D)), _const_spec((N_EXP, D))],
        out_specs=(pl.BlockSpec((CHUNK, D), lambda i: (i, 0)),
                   pl.BlockSpec((CHUNK, N_EXP), lambda i: (i, 0)), exp_major, exp_major,
                   pl.BlockSpec((1, N_EXP, 1), lambda i: (i, 0, 0))),
        scratch_shapes=[pltpu.VMEM((N_EXP, 1), F32)],
        compiler_params=_params(("arbitrary",)),
        name="moe_route",
    )(xp, xs, g, wrt)


def _gather_kernel(be_ref, pb_ref, pn_ref, r0_ref, bef_ref, aft_ref, pc_ref, pe_ref, pk_ref, pt_ref,
                   h_hbm, pos_hbm, cw_hbm, o_ref, gs_ref, hbuf, pbuf, cbuf, acc_ref, gacc_ref, sem):
    d = pl.program_id(0)
    e = be_ref[d]
    base = pb_ref[d]
    r0 = r0_ref[d]
    total = pt_ref[0]
    ahead = GSLOTS - 2

    def copies(idx):
        slot = idx % GSLOTS
        c = pc_ref[idx]
        ee = pe_ref[idx]
        return (pltpu.make_async_copy(h_hbm.at[pl.ds(c * CHUNK, CHUNK)], hbuf.at[slot], sem.at[0, slot]),
                pltpu.make_async_copy(pos_hbm.at[ee, c], pbuf.at[slot], sem.at[1, slot]),
                pltpu.make_async_copy(cw_hbm.at[ee, c], cbuf.at[slot], sem.at[2, slot]))

    def start(idx):
        @pl.when(idx < total)
        def _():
            for kind, cp in enumerate(copies(idx)):
                cp.start(priority=min(kind, 1))

    @pl.when(d == 0)
    def _():
        for k in range(ahead):
            start(k)

    acc_ref[...] = jnp.zeros_like(acc_ref)
    gacc_ref[...] = jnp.zeros_like(gacc_ref)
    row = d * BLK + lax.broadcasted_iota(I32, (GWIN, CHUNK), 0)
    row2 = d * BLK + lax.broadcasted_iota(I32, (2 * GWIN, CHUNK), 0)

    def body(j, carry):
        plan = []
        for half in range(2):
            idx = base + 2 * j + half
            slot = idx % GSLOTS
            for cp in copies(idx):
                cp.wait()
            start(idx + ahead)
            c = pc_ref[idx]
            ok = pk_ref[idx] > 0
            first = jnp.where(ok, jnp.clip(bef_ref[c * N_EXP + e] - r0, 0, BLK), 0)
            last = jnp.where(ok, jnp.clip(aft_ref[c * N_EXP + e] - r0, 0, BLK), 0)
            plan.append((half, slot, ok, jnp.minimum(first // GWIN, BLK // GWIN - 2), last))

        def fill(half, slot, ok, w, rows_at):
            w0 = pl.multiple_of(w * GWIN, GWIN)
            n = rows_at.shape[0]
            hit = (pbuf[slot] == rows_at + w0) & ok
            acc_ref[half, pl.ds(w0, n), :] += jnp.dot(hit.astype(BF16), hbuf[slot], preferred_element_type=F32)
            gacc_ref[half, pl.ds(w0, n), :] += jnp.sum(jnp.where(hit, cbuf[slot], 0.0), axis=1, keepdims=True)

        for half, slot, ok, w_first, last in plan:
            fill(half, slot, ok, w_first, row2)

        for half, slot, ok, w_first, last in plan:
            def window(w, c2, half=half, slot=slot, ok=ok):
                fill(half, slot, ok, w, row)
                return c2

            lax.fori_loop(w_first + 2, (last + GWIN - 1) // GWIN, window, 0)
        return carry

    lax.fori_loop(0, pn_ref[d] // 2, body, 0)
    o_ref[...] = (acc_ref[0] + acc_ref[1]).astype(BF16)
    gs_ref[...] = gacc_ref[0] + gacc_ref[1]


def _gather(h, pos_t, cw_t, blk_e, pbase, pcount, r0, before, after, pair_c, pair_e, pair_ok, ptotal, nblk):
    hbm = pl.BlockSpec(memory_space=pl.ANY)
    return pl.pallas_call(
        _gather_kernel,
        out_shape=(jax.ShapeDtypeStruct((nblk * BLK, D), BF16), jax.ShapeDtypeStruct((nblk * BLK, 1), F32)),
        grid_spec=pltpu.PrefetchScalarGridSpec(
            num_scalar_prefetch=10,
            grid=(nblk,),
            in_specs=[hbm, hbm, hbm],
            out_specs=(pl.BlockSpec((BLK, D), lambda d, *_: (d, 0)), pl.BlockSpec((BLK, 1), lambda d, *_: (d, 0))),
            scratch_shapes=[pltpu.VMEM((GSLOTS, CHUNK, D), BF16), pltpu.VMEM((GSLOTS, 1, CHUNK), I32),
                            pltpu.VMEM((GSLOTS, 1, CHUNK), F32), pltpu.VMEM((2, BLK, D), F32),
                            pltpu.VMEM((2, BLK, 1), F32), pltpu.SemaphoreType.DMA((3, GSLOTS))]),
        compiler_params=_params(("arbitrary",)),
        name="moe_gather",
    )(blk_e, pbase, pcount, r0, before, after, pair_c, pair_e, pair_ok, ptotal, h, pos_t, cw_t)


def _expert_up_kernel(be_ref, nv_ref, x_ref, wg_ref, wu_ref, o_ref, wgbf_ref, wubf_ref):
    d = pl.program_id(1)
    changed = (d == 0) | (be_ref[d] != be_ref[jnp.maximum(d - 1, 0)])

    @pl.when(changed)
    def _():
        wgbf_ref[...] = wg_ref[0].astype(BF16)
        wubf_ref[...] = wu_ref[0].astype(BF16)

    def up(rows):
        x = x_ref[rows, :]
        a = jnp.dot(x, wgbf_ref[...], preferred_element_type=F32)
        b = jnp.dot(x, wubf_ref[...], preferred_element_type=F32)
        o_ref[rows, :] = (a * jax.nn.sigmoid(a) * b).astype(BF16)

    half = BLK // 2
    nrows = nv_ref[d]

    @pl.when(nrows > half)
    def _():
        up(slice(0, BLK))

    @pl.when((nrows > 0) & (nrows <= half))
    def _():
        up(slice(0, half))
        o_ref[half:, :] = jnp.zeros((BLK - half, o_ref.shape[1]), BF16)

    @pl.when(nrows == 0)
    def _():
        o_ref[...] = jnp.zeros_like(o_ref)


def _expert_up(xs, blk_e, nvalid, wg, wu, tf):
    p = xs.shape[0]
    ff = wg.shape[2]
    return pl.pallas_call(
        _expert_up_kernel,
        out_shape=jax.ShapeDtypeStruct((p, ff), BF16),
        grid_spec=pltpu.PrefetchScalarGridSpec(
            num_scalar_prefetch=2,
            grid=(ff // tf, p // BLK),
            in_specs=[pl.BlockSpec((BLK, D), lambda f, d, be, nv: (d, 0)),
                      pl.BlockSpec((1, D, tf), lambda f, d, be, nv: (be[d], 0, f)),
                      pl.BlockSpec((1, D, tf), lambda f, d, be, nv: (be[d], 0, f))],
            out_specs=pl.BlockSpec((BLK, tf), lambda f, d, be, nv: (d, f)),
            scratch_shapes=[pltpu.VMEM((D, tf), BF16), pltpu.VMEM((D, tf), BF16)]),
        compiler_params=_params(("arbitrary", "arbitrary"), V7X_VMEM_LIMIT),
        name="moe_up",
    )(blk_e, nvalid, xs, wg, wu)


def _expert_down_kernel(be_ref, nv_ref, h_ref, gs_ref, wd_ref, o_ref, wdbf_ref):
    d = pl.program_id(1)
    changed = (d == 0) | (be_ref[d] != be_ref[jnp.maximum(d - 1, 0)])

    @pl.when(changed)
    def _():
        wdbf_ref[...] = wd_ref[0].astype(BF16)

    def down(rows):
        y = jnp.dot(h_ref[rows, :], wdbf_ref[...], preferred_element_type=F32)
        o_ref[rows, :] = (y * gs_ref[rows, :]).astype(BF16)

    half = BLK // 2
    nrows = nv_ref[d]

    @pl.when(nrows > half)
    def _():
        down(slice(0, BLK))

    @pl.when((nrows > 0) & (nrows <= half))
    def _():
        down(slice(0, half))
        o_ref[half:, :] = jnp.zeros((BLK - half, o_ref.shape[1]), BF16)

    @pl.when(nrows == 0)
    def _():
        o_ref[...] = jnp.zeros_like(o_ref)


def _expert_down(hh, gsort, blk_e, nvalid, wd, tn):
    p, ff = hh.shape
    return pl.pallas_call(
        _expert_down_kernel,
        out_shape=jax.ShapeDtypeStruct((p, D), BF16),
        grid_spec=pltpu.PrefetchScalarGridSpec(
            num_scalar_prefetch=2,
            grid=(D // tn, p // BLK),
            in_specs=[pl.BlockSpec((BLK, ff), lambda n, d, be, nv: (d, 0)),
                      pl.BlockSpec((BLK, 1), lambda n, d, be, nv: (d, 0)),
                      pl.BlockSpec((1, ff, tn), lambda n, d, be, nv: (be[d], 0, n))],
            out_specs=pl.BlockSpec((BLK, tn), lambda n, d, be, nv: (d, n)),
            scratch_shapes=[pltpu.VMEM((ff, tn), BF16)]),
        compiler_params=_params(("arbitrary", "arbitrary"), V7X_VMEM_LIMIT),
        name="moe_down",
    )(blk_e, nvalid, hh, gsort, wd)


def _combine_kernel(rs_ref, rc_ref, xp_ref, xs_ref, pos_ref, g_ref, y_hbm, op_ref, os_ref, wbuf, obuf, acc_ref,
                    sem, osem, *, n_prompt_chunks):
    i = pl.program_id(0)
    nch = pl.num_programs(0)

    def window_start(chunk, e):
        return pl.multiple_of(rs_ref[chunk * N_EXP + e] // BF16_ROWS * BF16_ROWS, BF16_ROWS)

    def copy(chunk, e, slot):
        return pltpu.make_async_copy(y_hbm.at[pl.ds(window_start(chunk, e), CWIN)],
                                     wbuf.at[slot, pl.ds(e * CWIN, CWIN)], sem.at[slot, e])

    @pl.when(i == 0)
    def _():
        for e in range(N_EXP):
            copy(0, e, 0).start(priority=e % 2)

    slot = i % 2

    @pl.when(i + 1 < nch)
    def _():
        for e in range(N_EXP):
            copy(i + 1, e, 1 - slot).start(priority=e % 2)

    col = lax.broadcasted_iota(I32, (CHUNK, CWIN), 1)
    hits = []
    for e in range(N_EXP):
        copy(i, e, slot).wait()
        hits.append((pos_ref[:, e:e + 1] - window_start(i, e)) == col)
    onehot = jnp.concatenate(hits, axis=1).astype(BF16)
    acc_ref[...] = jnp.dot(onehot, wbuf[slot], preferred_element_type=F32)

    for e in range(N_EXP):
        for k in range(1, (CHUNK + BF16_ROWS - 1) // CWIN + 1):
            tail = window_start(i, e) + k * CWIN

            @pl.when(rs_ref[i * N_EXP + e] + rc_ref[i * N_EXP + e] > tail)
            def _():
                cp = pltpu.make_async_copy(y_hbm.at[pl.ds(tail, CWIN)], obuf, osem)
                cp.start()
                cp.wait()
                hit = ((pos_ref[:, e:e + 1] - tail) == col).astype(BF16)
                acc_ref[...] += jnp.dot(hit, obuf[...], preferred_element_type=F32)

    x = jnp.where(i < n_prompt_chunks, xp_ref[...], xs_ref[...])
    out = x + _rms(acc_ref[...], g_ref[...])

    @pl.when(i < n_prompt_chunks)
    def _():
        op_ref[...] = out

    @pl.when(i >= n_prompt_chunks)
    def _():
        os_ref[...] = out


def _combine(run_start, run_count, xp, xs, pos, g, ys):
    npc = xp.shape[0] // CHUNK
    nsc = xs.shape[0] // CHUNK
    nch = npc + nsc
    tok8 = pl.BlockSpec((CHUNK, N_EXP), lambda i, *_: (i, 0))
    pspec = pl.BlockSpec((CHUNK, D), lambda i, *_: (jnp.minimum(i, npc - 1), 0))
    sspec = pl.BlockSpec((CHUNK, D), lambda i, *_: (jnp.maximum(i - npc, 0), 0))
    return pl.pallas_call(
        functools.partial(_combine_kernel, n_prompt_chunks=npc),
        out_shape=(jax.ShapeDtypeStruct(xp.shape, F32), jax.ShapeDtypeStruct(xs.shape, F32)),
        grid_spec=pltpu.PrefetchScalarGridSpec(
            num_scalar_prefetch=2,
            grid=(nch,),
            in_specs=[pspec, sspec, tok8, pl.BlockSpec((1, D), lambda i, *_: (0, 0)),
                      pl.BlockSpec(memory_space=pl.ANY)],
            out_specs=(pspec, sspec),
            scratch_shapes=[pltpu.VMEM((2, N_EXP * CWIN, D), BF16), pltpu.VMEM((CWIN, D), BF16),
                            pltpu.VMEM((CHUNK, D), F32),
                            pltpu.SemaphoreType.DMA((2, N_EXP)), pltpu.SemaphoreType.DMA]),
        compiler_params=_params(("arbitrary",), V7X_VMEM_LIMIT),
        name="moe_combine",
    )(run_start, run_count, xp, xs, pos, g, ys)


def _moe(xp, xs, g4, g5, w_router, wg, wu, wd):
    npc = xp.shape[0] // CHUNK
    nch = npc + xs.shape[0] // CHUNK
    t = nch * CHUNK
    h, rk, rk_t, cw_t, cnt = _route(xp, xs, g4, w_router.T)

    after = cnt.reshape(nch, N_EXP).astype(I32)
    before = jnp.concatenate([jnp.zeros((1, N_EXP), I32), after[:-1]], axis=0)
    counts = after[-1]
    gsz = (counts + BLK - 1) // BLK * BLK
    gend = jnp.cumsum(gsz)
    gstart = gend - gsz
    nblk = (2 * t + N_EXP * (BLK - 1)) // BLK + 1 + (2 * CWIN + BLK - 1) // BLK
    bstart = jnp.arange(nblk, dtype=I32) * BLK
    blk_e = jnp.minimum(jnp.sum(gend[None, :] <= bstart[:, None], axis=1), N_EXP - 1).astype(I32)
    valid = bstart < gend[-1]

    def pick(table, idx, n):
        hit = idx[:, None] == jnp.arange(n, dtype=I32)[None, :]
        return jnp.sum(jnp.where(hit, table[..., None, :], 0), axis=-1)

    r0 = bstart - pick(gstart, blk_e, N_EXP)
    aft_e = pick(after, blk_e, N_EXP)
    bef_e = pick(before, blk_e, N_EXP)
    c_lo = jnp.where(valid, jnp.sum(aft_e <= r0[None, :], axis=0), 0).astype(I32)
    c_hi = jnp.where(valid, jnp.sum(bef_e < (r0 + BLK)[None, :], axis=0) - 1, -1).astype(I32)
    c_lo = jnp.minimum(c_lo, nch - 1)
    pos = jnp.where(rk >= 0, rk + gstart[None, :], -1).astype(I32)
    pos_t = jnp.where(rk_t >= 0, rk_t + gstart[:, None], -1).astype(I32).reshape(N_EXP, nch, 1, CHUNK)
    cw_t = cw_t.reshape(N_EXP, nch, 1, CHUNK)
    run_start = (gstart[None, :] + before).astype(I32).reshape(-1)
    run_count = (after - before).reshape(-1)
    nvalid = jnp.where(valid, jnp.clip(pick(counts, blk_e, N_EXP) - r0, 0, BLK), 0).astype(I32)
    nsrc = jnp.maximum(c_hi - c_lo + 1, 0)
    pcount = nsrc + (nsrc & 1)
    pend = jnp.cumsum(pcount)
    pbase = pend - pcount
    k = jnp.arange(2 * nblk + 2 * nch * N_EXP, dtype=I32)
    pair_blk = jnp.minimum(jnp.sum(pend[None, :] <= k[:, None], axis=1), nblk - 1)
    per_blk = pick(jnp.stack([pbase, nsrc, c_lo, blk_e]), pair_blk, nblk)
    k_local = k - per_blk[0]
    pair_ok = (k_local < per_blk[1]).astype(I32)
    pair_c = jnp.clip(per_blk[2] + jnp.minimum(k_local, per_blk[1] - 1), 0, nch - 1).astype(I32)
    pair_e = per_blk[3].astype(I32)

    xsort, gsort = _gather(h, pos_t, cw_t, blk_e, pbase.astype(I32), pcount.astype(I32), r0.astype(I32),
                           before.reshape(-1), after.reshape(-1), pair_c, pair_e, pair_ok,
                           pend[-1:].astype(I32), nblk)
    hh = _expert_up(xsort, blk_e, nvalid, wg, wu, tf=1792)
    ysort = _expert_down(hh, gsort, blk_e, nvalid, wd, tn=D)
    return _combine(run_start, run_count, xp, xs, pos, g5, ysort)


def kernel(x_prompt, x_sample, cache_conv, cache_mem_k, cache_mem_v, state_ssm_re, state_ssm_im, mem_prompt, norm_g, mem_norm_g, w_xq, w_xk, w_xv, w_xo, conv_w_pw1, conv_b_pw1, conv_w_dw, conv_b_dw, conv_ln_g, conv_ln_b, conv_w_pw2, conv_b_pw2, ssm_a_re, ssm_a_im, ssm_log_dt, ssm_b_re, ssm_b_im, ssm_c_re, ssm_c_im, ssm_d, ssm_w_glu, ssm_b_glu, ffn_w_gate, ffn_w_up, ffn_w_down, moe_w_router, moe_w_gate, moe_w_up, moe_w_down):
    nbp, seqp, _ = x_prompt.shape
    nbs, seqs, _ = x_sample.shape
    tp = nbp * seqp
    ts = nbs * seqs
    row = lambda a: a.reshape(1, -1)
    g = lambda i, k: norm_g[i, k].reshape(1, D)

    nl = w_xk.shape[0]
    pk, pv = _mem_kv(mem_prompt, mem_norm_g.reshape(nl, 1, D), w_xk, w_xv)
    p_mem_k = pk.reshape(nl, nbp, N_MEM, N_HEADS, HEAD_DIM)
    p_mem_v = pv.reshape(nl, nbp, N_MEM, N_HEADS, HEAD_DIM)
    ck = cache_mem_k.reshape(nl * nbs, N_MEM, N_HEADS, HEAD_DIM)
    cv = cache_mem_v.reshape(nl * nbs, N_MEM, N_HEADS, HEAD_DIM)

    conv_args = (conv_w_dw[0], row(conv_b_dw[0]), row(conv_ln_g[0]), row(conv_ln_b[0]),
                 conv_w_pw2[0], row(conv_b_pw2[0]), g(0, 1))
    xp = x_prompt.reshape(tp, D)
    xs = x_sample.reshape(ts, D)
    up = _conv_pw1(xp, g(0, 0), conv_w_pw1[0], row(conv_b_pw1[0]), tm=1024).reshape(nbp, seqp, D)
    us = _conv_pw1(xs, g(0, 0), conv_w_pw1[0], row(conv_b_pw1[0]), tm=ts).reshape(nbs, seqs, D)
    hist_p = jnp.zeros((nbp, HIST, D), F32)
    hist_s = jnp.pad(cache_conv[0], ((0, 0), (HIST - CONV_W + 1, 0), (0, 0)))
    xp = _conv_dw_pw2(up, x_prompt, hist_p, *conv_args, tl=512).reshape(tp, D)
    xs = _conv_dw_pw2(us, x_sample, hist_s, *conv_args, tl=seqs).reshape(ts, D)
    p_conv = up[:, seqp - (CONV_W - 1):][None]
    s_conv = jnp.concatenate([cache_conv[0], us], axis=1)[:, -(CONV_W - 1):][None]

    tma = 1024
    xp = _attn(xp, pk, pv, 0, w_xq[0], w_xo[0], g(0, 2), g(0, 3), nbp, seqp, tma)
    xs = _attn(xs, ck, cv, 0, w_xq[0], w_xo[0], g(0, 2), g(0, 3), nbs, seqs, seqs)

    ffn_w = (_to_bf16(ffn_w_gate[0], 4), _to_bf16(ffn_w_up[0], 4), _to_bf16(ffn_w_down[0], 4))
    tff = ffn_w_gate.shape[2] // 2
    xp = _ffn(xp, g(0, 4), g(0, 5), *ffn_w, tm=1024, tf=tff, out_shape=(tp, D), out_index=lambda i, f: (i, 0))
    xs = _ffn(xs, g(0, 4), g(0, 5), *ffn_w, tm=ts, tf=tff, out_shape=(ts, D), out_index=lambda i, f: (i, 0))

    (ab_r, ab_i, a2_r, a2_i), (bb_r, bb_i, abb_r, abb_i) = _ssm_prep(
        ssm_a_re[0], ssm_a_im[0], ssm_log_dt[0], ssm_b_re[0], ssm_b_im[0])
    bdiag = lambda m: _block_diag(m.transpose(0, 2, 1)).astype(BF16)
    bm_r, bm_i = bdiag(bb_r), bdiag(bb_i)
    cm_r, cm_i = bdiag(ssm_c_re[0]), bdiag(-ssm_c_im[0])
    tail_args = (cm_r, cm_i, row(ssm_d[0]), _to_bf16(ssm_w_glu[0], 4), row(ssm_b_glu[0]))
    half = SUBLANES // 2
    hdiag = lambda m: _block_diag(m.transpose(0, 2, 1), gpb=SSM_CB // SSM_C // 2).astype(BF16)
    pair_args = (jnp.concatenate([jnp.tile(ab_r, (half, 1)), jnp.tile(a2_r, (half, 1))]),
                 jnp.concatenate([jnp.tile(ab_i, (half, 1)), jnp.tile(a2_i, (half, 1))]),
                 jnp.concatenate([hdiag(bb_r), hdiag(abb_r)], axis=1),
                 jnp.concatenate([hdiag(bb_i), hdiag(abb_i)], axis=1))
    zero_state = jnp.zeros((SUBLANES, SSM_N), F32)
    xp, p_sr, p_si = _ssm(xp.reshape(nbp, seqp, D), g(1, 0), g(1, 1), zero_state, zero_state,
                          *pair_args, *tail_args, tl=64)
    xs, s_sr, s_si = _ssm(xs.reshape(nbs, seqs, D), g(1, 0), g(1, 1), state_ssm_re[0].reshape(nbs, SSM_N),
                          state_ssm_im[0].reshape(nbs, SSM_N), jnp.tile(ab_r, (nbs, 1)), jnp.tile(ab_i, (nbs, 1)),
                          bm_r, bm_i, *tail_args, tl=seqs)
    st = lambda a, n: a[:n].reshape(1, n, SSM_G, SSM_P)
    p_ssm_re, p_ssm_im = st(p_sr, nbp), st(p_si, nbp)
    s_ssm_re, s_ssm_im = st(s_sr, nbs), st(s_si, nbs)

    xp = _attn(xp.reshape(tp, D), pk, pv, nbp, w_xq[1], w_xo[1], g(1, 2), g(1, 3), nbp, seqp, tma)
    xs = _attn(xs.reshape(ts, D), ck, cv, nbs, w_xq[1], w_xo[1], g(1, 2), g(1, 3), nbs, seqs, seqs)

    yp, ysm = _moe(xp, xs, g(1, 4), g(1, 5), moe_w_router[0], moe_w_gate[0], moe_w_up[0], moe_w_down[0])
    return (yp.reshape(nbp, seqp, D), ysm.reshape(nbs, seqs, D), p_conv, p_mem_k, p_mem_v,
            p_ssm_re, p_ssm_im, s_conv, s_ssm_re, s_ssm_im)
```
